```python
import math, functools
import jax, jax.numpy as jnp
from jax import lax
import numpy as np

D_MODEL = 1024
BATCH = 32
SEQ = 256
DEPTH = 4
DEC_BATCH = 4
DEC_SEQ = 2048
PAST_LEN = 256

GRID_W = 64
N_MIXERS = 3
N_SSD_LAYERS = (DEPTH + 2) // 3
N_CONV_LAYERS = (DEPTH + 1) // 3
N_DIFF_LAYERS = DEPTH // 3
SSM_D_INNER = 2 * D_MODEL
SSM_HEAD_DIM = 64
SSM_HEADS = SSM_D_INNER // SSM_HEAD_DIM
SSM_GROUPS = 8
SSM_STATE = 128
SSM_CONV = 3
SSM_CHUNK = 128
SSM_CONV_DIM = SSM_D_INNER + 2 * SSM_GROUPS * SSM_STATE
SSM_IN_DIM = SSM_D_INNER + SSM_CONV_DIM + 2 * SSM_HEADS
SHORT_CONV = 3
DIFF_HEAD_DIM = 64
DIFF_HEADS = D_MODEL // (2 * DIFF_HEAD_DIM)
Q_BLOCK = 128
ROPE_THETA = 10000.0
ROT_PAIRS_PER_AXIS = DIFF_HEAD_DIM // 4
FFN_DIM = 2816
FFN_CONV = 3
NORM_EPS = 1e-6

kernel_name = 'hybrid_diffusion_ssd_conv_diffattn_step'


def rmsnorm(x, g):
    xf = x.astype(jnp.float32)
    y = xf * lax.rsqrt(jnp.mean(xf * xf, axis=-1, keepdims=True) + NORM_EPS)
    return (y * g.astype(jnp.float32)).astype(x.dtype)


def modulation(cond, w, b):
    return jax.nn.silu(cond) @ w + b


def sandwich_layer(x, mod, g, mix_fn, ffn_fn):
    sh1, sc1, gt1, sh2, sc2, gt2 = jnp.split(mod, 6, axis=-1)
    m, aux = mix_fn(rmsnorm(x, g[0]) * (1 + sc1) + sh1)
    x = x + gt1 * rmsnorm(m, g[1])
    f = ffn_fn(rmsnorm(x, g[2]) * (1 + sc2) + sh2)
    x = x + gt2 * rmsnorm(f, g[3])
    return x, aux


def dwconv(x, w):
    k = w.shape[0]
    pad = k // 2
    L = x.shape[1]
    xp = jnp.pad(x, ((0, 0), (pad, pad), (0, 0)))
    out = xp[:, 0:L] * w[0]
    for i in range(1, k):
        out = out + xp[:, i:i + L] * w[i]
    return out


def conv_ffn(h, w_up, conv_w, w_down):
    u = dwconv(h @ w_up, conv_w)
    gate, val = jnp.split(u, 2, axis=-1)
    return (jax.nn.silu(gate) * val) @ w_down


def short_conv_mixer(h, w_in, conv_w, w_out):
    bg, cg, u = jnp.split(h @ w_in, 3, axis=-1)
    return (bg * dwconv(cg * u, conv_w)) @ w_out, None


def ssd_scan(x, dt, a, b_in, c_in, h0):
    bsz, L, H, P = x.shape
    G, N = b_in.shape[2], b_in.shape[3]
    R = H // G
    nc = L // SSM_CHUNK
    f32 = jnp.float32
    xf = x.astype(f32).reshape(bsz, nc, SSM_CHUNK, G, R, P)
    dtc = dt.reshape(bsz, nc, SSM_CHUNK, G, R)
    bc = b_in.astype(f32).reshape(bsz, nc, SSM_CHUNK, G, N)
    cc = c_in.astype(f32).reshape(bsz, nc, SSM_CHUNK, G, N)
    acs = jnp.cumsum(dtc * a.reshape(G, R), axis=2)
    mask = jnp.tril(jnp.ones((SSM_CHUNK, SSM_CHUNK), bool))[:, :, None, None]
    decay = jnp.exp(jnp.where(mask, acs[:, :, :, None] - acs[:, :, None], -jnp.inf))
    cb = jnp.einsum('bcqgn,bcsgn->bcqsg', cc, bc)
    y_diag = jnp.einsum('bcqsgr,bcsgrp->bcqgrp', cb[..., None] * decay * dtc[:, :, None], xf)
    xw = xf * (jnp.exp(acs[:, :, -1:] - acs) * dtc)[..., None]
    chunk_states = jnp.einsum('bcsgn,bcsgrp->bcgrpn', bc, xw)
    chunk_decay = jnp.exp(acs[:, :, -1])

    def step(h, inp):
        st, dec = inp
        return h * dec[..., None, None] + st, h

    h_last, h_in = lax.scan(step, h0.astype(f32).reshape(bsz, G, R, P, N),
                            (jnp.swapaxes(chunk_states, 0, 1), jnp.swapaxes(chunk_decay, 0, 1)))
    h_in = jnp.swapaxes(h_in, 0, 1)
    y_off = jnp.einsum('bcqgn,bcgrpn->bcqgrp', cc, h_in) * jnp.exp(acs)[..., None]
    return (y_diag + y_off).reshape(bsz, L, H, P), h_last.reshape(bsz, H, P, N)


def ssd_mixer(h, h0, w_in, conv_w, conv_b, dt_bias, a_log, d_skip, norm_g, w_out):
    bsz, L, _ = h.shape
    z, xbc, dt_raw = jnp.split(h @ w_in, [SSM_D_INNER, SSM_D_INNER + SSM_CONV_DIM], axis=-1)
    xbc = jax.nn.silu(dwconv(xbc, conv_w) + conv_b)
    xs, bm, cm = jnp.split(xbc, [SSM_D_INNER, SSM_D_INNER + SSM_GROUPS * SSM_STATE], axis=-1)
    xs = xs.reshape(bsz, L, SSM_HEADS, SSM_HEAD_DIM)
    bm = bm.reshape(bsz, L, SSM_GROUPS, SSM_STATE)
    cm = cm.reshape(bsz, L, SSM_GROUPS, SSM_STATE)
    dt = jax.nn.softplus(dt_raw.astype(jnp.float32) + dt_bias.reshape(-1).astype(jnp.float32))
    dt_f, dt_b = jnp.split(dt, 2, axis=-1)
    a = -jnp.exp(a_log.astype(jnp.float32))
    y_f, s_f = ssd_scan(xs, dt_f, a[0], bm, cm, h0[:, 0])
    flip = lambda t: jnp.flip(t, axis=1)
    y_b, s_b = ssd_scan(flip(xs), flip(dt_b), a[1], flip(bm), flip(cm), h0[:, 1])
    y = y_f + flip(y_b) + d_skip.astype(jnp.float32)[:, None] * xs.astype(jnp.float32)
    y = rmsnorm(y.reshape(bsz, L, SSM_D_INNER) * jax.nn.silu(z.astype(jnp.float32)), norm_g)
    return y.astype(h.dtype) @ w_out, jnp.stack([s_f, s_b], axis=1)


def axial_rotary(n_tokens):
    rows = n_tokens // GRID_W
    row = jnp.repeat(jnp.arange(rows, dtype=jnp.float32), GRID_W)
    col = jnp.tile(jnp.arange(GRID_W, dtype=jnp.float32), rows)
    inv = ROPE_THETA ** (-jnp.arange(ROT_PAIRS_PER_AXIS, dtype=jnp.float32) / ROT_PAIRS_PER_AXIS)
    ang = jnp.concatenate([row[:, None] * inv, col[:, None] * inv], axis=-1)
    return jnp.cos(ang), jnp.sin(ang)


def apply_rotary(x, cos, sin):
    half = x.shape[-1] // 2
    x1, x2 = x[..., :half], x[..., half:]
    c = cos[None, :, None].astype(x.dtype)
    s = sin[None, :, None].astype(x.dtype)
    return jnp.concatenate([x1 * c - x2 * s, x2 * c + x1 * s], axis=-1)


def diff_qkv(h, w_qkv):
    bsz, L, _ = h.shape
    q, k, v = jnp.split(h @ w_qkv, 3, axis=-1)
    return (q.reshape(bsz, L, 2 * DIFF_HEADS, DIFF_HEAD_DIM),
            k.reshape(bsz, L, 2 * DIFF_HEADS, DIFF_HEAD_DIM),
            v.reshape(bsz, L, DIFF_HEADS, 2 * DIFF_HEAD_DIM))


def diff_lambda(lp, lam_init):
    lp = lp.astype(jnp.float32)
    return jnp.exp(jnp.sum(lp[0] * lp[1])) - jnp.exp(jnp.sum(lp[2] * lp[3])) + lam_init


def block_diff_attention(q, k, v, lam):
    bsz, Lq, H2, d = q.shape
    nb = Lq // Q_BLOCK
    qb = jnp.swapaxes(q.reshape(bsz, nb, Q_BLOCK, H2, d), 0, 1)
    scale = d ** -0.5

    def one_block(qi):
        s = jnp.einsum('bqhd,bkhd->bhqk', qi, k).astype(jnp.float32) * scale
        p = jax.nn.softmax(s, axis=-1).reshape(bsz, H2 // 2, 2, Q_BLOCK, -1)
        att = p[:, :, 0] - lam * p[:, :, 1]
        return jnp.einsum('bhqk,bkhe->bqhe', att.astype(v.dtype), v)

    o = lax.map(one_block, qb)
    return jnp.swapaxes(o, 0, 1).reshape(bsz, Lq, H2 // 2, 2 * d)


def diff_out(o, lam_init, subln_g, w_out):
    bsz, L = o.shape[0], o.shape[1]
    o = rmsnorm(o, subln_g) * (1.0 - lam_init)
    return o.reshape(bsz, L, D_MODEL) @ w_out


def setup_inputs(seed: int = 0) -> dict:
    key = jax.random.key(seed)
    ks = iter(jax.random.split(key, 40))
    f32 = jnp.float32
    D = D_MODEL

    def nrm(shape, scale):
        return jax.random.normal(next(ks), shape, f32) * scale

    def gain(shape):
        return 1.0 + nrm(shape, 0.05)

    x_prompt = nrm((BATCH, SEQ, D), 1.0)
    x_sample = nrm((DEC_BATCH, DEC_SEQ, D), 1.0)
    state_ssm = nrm((DEC_BATCH, N_SSD_LAYERS, 2, SSM_HEADS, SSM_HEAD_DIM, SSM_STATE), 0.3)
    cache_k = nrm((DEC_BATCH, N_DIFF_LAYERS, PAST_LEN, 2 * DIFF_HEADS, DIFF_HEAD_DIM), 1.0)
    cache_v = nrm((DEC_BATCH, N_DIFF_LAYERS, PAST_LEN, DIFF_HEADS, 2 * DIFF_HEAD_DIM), 1.0)
    c = nrm((DEC_BATCH, D), 1.0)
    c_ctx = nrm((D,), 1.0)
    w_mod = nrm((DEPTH, D, 6 * D), 0.5 * D ** -0.5)
    b_mod = nrm((DEPTH, 6 * D), 0.02)
    norm_g = gain((DEPTH, 4, D))
    ssd_w_in = nrm((N_SSD_LAYERS, D, SSM_IN_DIM), D ** -0.5)
    ssd_conv_w = nrm((N_SSD_LAYERS, SSM_CONV, SSM_CONV_DIM), SSM_CONV ** -0.5)
    ssd_conv_b = nrm((N_SSD_LAYERS, SSM_CONV_DIM), 0.02)
    dt0 = jnp.exp(jax.random.uniform(next(ks), (N_SSD_LAYERS, 2, SSM_HEADS), f32,
                                     math.log(1e-3), math.log(1e-1)))
    ssd_dt_bias = dt0 + jnp.log(-jnp.expm1(-dt0))
    ssd_a_log = jnp.log(jax.random.uniform(next(ks), (N_SSD_LAYERS, 2, SSM_HEADS), f32, 1.0, 16.0))
    ssd_d = 1.0 + nrm((N_SSD_LAYERS, SSM_HEADS), 0.1)
    ssd_norm_g = gain((N_SSD_LAYERS, SSM_D_INNER))
    ssd_w_out = nrm((N_SSD_LAYERS, SSM_D_INNER, D), SSM_D_INNER ** -0.5)
    sc_w_in = nrm((N_CONV_LAYERS, D, 3 * D), D ** -0.5)
    sc_conv_w = nrm((N_CONV_LAYERS, SHORT_CONV, D), SHORT_CONV ** -0.5)
    sc_w_out = nrm((N_CONV_LAYERS, D, D), D ** -0.5)
    da_w_qkv = nrm((N_DIFF_LAYERS, D, 3 * D), D ** -0.5)
    da_lambda = nrm((N_DIFF_LAYERS, 4, DIFF_HEAD_DIM), 0.1)
    da_subln_g = gain((N_DIFF_LAYERS, 2 * DIFF_HEAD_DIM))
    da_w_out = nrm((N_DIFF_LAYERS, D, D), D ** -0.5)
    ffn_w_up = nrm((DEPTH, D, 2 * FFN_DIM), D ** -0.5)
    ffn_conv_w = nrm((DEPTH, FFN_CONV, 2 * FFN_DIM), FFN_CONV ** -0.5)
    ffn_w_down = nrm((DEPTH, FFN_DIM, D), FFN_DIM ** -0.5)
    return {'x_prompt': x_prompt, 'x_sample': x_sample, 'state_ssm': state_ssm,
            'cache_k': cache_k, 'cache_v': cache_v, 'c': c, 'c_ctx': c_ctx,
            'w_mod': w_mod, 'b_mod': b_mod, 'norm_g': norm_g,
            'ssd_w_in': ssd_w_in, 'ssd_conv_w': ssd_conv_w, 'ssd_conv_b': ssd_conv_b,
            'ssd_dt_bias': ssd_dt_bias, 'ssd_a_log': ssd_a_log, 'ssd_d': ssd_d,
            'ssd_norm_g': ssd_norm_g, 'ssd_w_out': ssd_w_out,
            'sc_w_in': sc_w_in, 'sc_conv_w': sc_conv_w, 'sc_w_out': sc_w_out,
            'da_w_qkv': da_w_qkv, 'da_lambda': da_lambda, 'da_subln_g': da_subln_g,
            'da_w_out': da_w_out, 'ffn_w_up': ffn_w_up, 'ffn_conv_w': ffn_conv_w,
            'ffn_w_down': ffn_w_down}


def reference(x_prompt, x_sample, state_ssm, cache_k, cache_v, c, c_ctx, w_mod, b_mod, norm_g,
              ssd_w_in, ssd_conv_w, ssd_conv_b, ssd_dt_bias, ssd_a_log, ssd_d, ssd_norm_g, ssd_w_out,
              sc_w_in, sc_conv_w, sc_w_out, da_w_qkv, da_lambda, da_subln_g, da_w_out,
              ffn_w_up, ffn_conv_w, ffn_w_down):
    yp, ys = x_prompt, x_sample
    bp = x_prompt.shape[0]
    cos, sin = axial_rotary(x_sample.shape[1])
    new_ssm, new_k, new_v = [], [], []
    for l in range(DEPTH):
        j = l // N_MIXERS
        kind = l % N_MIXERS
        mod_p = modulation(c_ctx, w_mod[l], b_mod[l])[None, None]
        mod_s = modulation(c, w_mod[l], b_mod[l])[:, None]
        ffn = functools.partial(conv_ffn, w_up=ffn_w_up[l], conv_w=ffn_conv_w[l], w_down=ffn_w_down[l])
        if kind == 0:
            ssd_args = (ssd_w_in[j], ssd_conv_w[j], ssd_conv_b[j], ssd_dt_bias[j], ssd_a_log[j],
                        ssd_d[j], ssd_norm_g[j], ssd_w_out[j])
            zeros = jnp.zeros((bp, 2, SSM_HEADS, SSM_HEAD_DIM, SSM_STATE), jnp.float32)
            mix_p = lambda h: ssd_mixer(h, zeros, *ssd_args)
            mix_s = lambda h: ssd_mixer(h, state_ssm[:, j], *ssd_args)
        elif kind == 1:
            mix_p = functools.partial(short_conv_mixer, w_in=sc_w_in[j], conv_w=sc_conv_w[j], w_out=sc_w_out[j])
            mix_s = mix_p
        else:
            lam_init = 0.8 - 0.6 * math.exp(-0.3 * l)
            lam = diff_lambda(da_lambda[j], lam_init)
            w_qkv, g_sub, w_o = da_w_qkv[j], da_subln_g[j], da_w_out[j]

            def mix_p(h):
                q, k, v = diff_qkv(h, w_qkv)
                return diff_out(block_diff_attention(q, k, v, lam), lam_init, g_sub, w_o), (k, v)

            def mix_s(h):
                q, k, v = diff_qkv(h, w_qkv)
                q = apply_rotary(q, cos, sin)
                k = apply_rotary(k, cos, sin)
                k_all = jnp.concatenate([cache_k[:, j].astype(k.dtype), k], axis=1)
                v_all = jnp.concatenate([cache_v[:, j].astype(v.dtype), v], axis=1)
                return diff_out(block_diff_attention(q, k_all, v_all, lam), lam_init, g_sub, w_o), None

        yp, aux = sandwich_layer(yp, mod_p, norm_g[l], mix_p, ffn)
        ys, _ = sandwich_layer(ys, mod_s, norm_g[l], mix_s, ffn)
        if kind == 0:
            new_ssm.append(aux.astype(x_prompt.dtype))
        elif kind == 2:
            new_k.append(aux[0])
            new_v.append(aux[1])
    new_state_ssm = jnp.stack(new_ssm, axis=1)
    new_cache_k = jnp.stack(new_k, axis=1)
    new_cache_v = jnp.stack(new_v, axis=1)
    return (yp, ys, new_state_ssm, new_cache_k, new_cache_v)
```

```python
import functools
import math

import jax
import jax.numpy as jnp
from jax import lax
from jax.experimental import pallas as pl
from jax.experimental.pallas import tpu as pltpu

D_MODEL = 1024
DEPTH = 4
GRID_W = 64
N_MIXERS = 3
SSM_D_INNER = 2 * D_MODEL
SSM_HEAD_DIM = 64
SSM_HEADS = SSM_D_INNER // SSM_HEAD_DIM
SSM_GROUPS = 8
SSM_STATE = 128
SSM_CONV_DIM = SSM_D_INNER + 2 * SSM_GROUPS * SSM_STATE
HEADS_PER_GROUP = SSM_HEADS // SSM_GROUPS
GROUP_ROWS = HEADS_PER_GROUP * SSM_HEAD_DIM
DIFF_HEAD_DIM = 64
DIFF_HEADS = D_MODEL // (2 * DIFF_HEAD_DIM)
ROPE_THETA = 10000.0
ROT_PAIRS_PER_AXIS = DIFF_HEAD_DIM // 4
FFN_DIM = 2816
NORM_EPS = 1e-6

LANES = 128
HALO = 8
SCAN_CHUNK = 128
ROW_TILE = 256
COL_CHUNK = 256
VMEM_LIMIT = 56 * 1024 * 1024

F32 = jnp.float32
BF16 = jnp.bfloat16


def _cparams(n_axes):
    return pltpu.CompilerParams(dimension_semantics=("arbitrary",) * n_axes,
                                vmem_limit_bytes=VMEM_LIMIT)


def _rms(x, g):
    ms = jnp.mean(x * x, axis=-1, keepdims=True)
    return x * lax.rsqrt(ms + NORM_EPS) * g


def _silu(x):
    return x * jax.nn.sigmoid(x)


def _mod_slices(mod, first):
    d = D_MODEL
    return (mod[:, first * d:(first + 1) * d], mod[:, (first + 1) * d:(first + 2) * d],
            mod[:, (first + 2) * d:(first + 3) * d])


def _ext_norm(xp_ref, x_ref, xn_ref, g, sc, sh):
    pieces = [(_rms(r[...], g) * (1.0 + sc) + sh) for r in (xp_ref, x_ref, xn_ref)]
    return jnp.concatenate(pieces, axis=0).astype(BF16)


def _seq_masks(tile_rows, cols, seq_len):
    row = lax.broadcasted_iota(jnp.int32, (tile_rows, cols), 0) + pl.program_id(0) * tile_rows
    pos = row & (seq_len - 1)
    return pos != 0, pos != seq_len - 1


def _conv3(u_ext, w3, has_prev, has_next, tile_rows):
    n_ext = tile_rows + 2 * HALO
    up = pltpu.roll(u_ext, 1, 0)[HALO:HALO + tile_rows]
    un = pltpu.roll(u_ext, n_ext - 1, 0)[HALO:HALO + tile_rows]
    um = u_ext[HALO:HALO + tile_rows]
    return (jnp.where(has_prev, up, 0.0) * w3[0:1] + um * w3[1:2]
            + jnp.where(has_next, un, 0.0) * w3[2:3])


def _halo_specs(n_rows, tile_rows, width):
    per = tile_rows // HALO
    last = n_rows // HALO - 1
    return [
        pl.BlockSpec((HALO, width), lambda i: (jnp.maximum(i * per - 1, 0), 0)),
        pl.BlockSpec((tile_rows, width), lambda i: (i, 0)),
        pl.BlockSpec((HALO, width), lambda i: (jnp.minimum((i + 1) * per, last), 0)),
    ]


def _mod_spec(n_mod, tile_rows, seq_len):
    if n_mod == 1:
        return pl.BlockSpec((1, 1, 6 * D_MODEL), lambda i: (0, 0, 0))
    return pl.BlockSpec((1, 1, 6 * D_MODEL), lambda i: ((i * tile_rows) // seq_len, 0, 0))


def _const_spec(shape):
    nd = len(shape)
    return pl.BlockSpec(shape, lambda *_: (0,) * nd)


def _mod_kernel(cond_ref, w_ref, b_ref, o_ref):
    a = _silu(cond_ref[...]).astype(BF16)
    o_ref[0] = jnp.dot(a, w_ref[0].astype(BF16), preferred_element_type=F32) + b_ref[0]


def _modulation(cond, w_mod, b_mod):
    n_cols = 6 * D_MODEL
    blk = n_cols // 4
    return pl.pallas_call(
        _mod_kernel,
        grid=(DEPTH, n_cols // blk),
        in_specs=[_const_spec(cond.shape),
                  pl.BlockSpec((1, D_MODEL, blk), lambda l, j: (l, 0, j)),
                  pl.BlockSpec((1, 1, blk), lambda l, j: (l, 0, j))],
        out_specs=pl.BlockSpec((1, cond.shape[0], blk), lambda l, j: (l, 0, j)),
        out_shape=jax.ShapeDtypeStruct((DEPTH, cond.shape[0], n_cols), F32),
        compiler_params=_cparams(2),
        name="modulation",
    )(cond, w_mod, b_mod.reshape(DEPTH, 1, n_cols))


def _ffn_kernel(xp_ref, x_ref, xn_ref, mod_ref, g_ref, wup_ref, cw_ref, wdn_ref, o_ref,
                hext_ref, acc_ref, *, seq_len):
    t = x_ref.shape[0]
    sh, sc, gt = _mod_slices(mod_ref[0], 3)
    hext_ref[...] = _ext_norm(xp_ref, x_ref, xn_ref, g_ref[2:3], sc, sh)
    has_prev, has_next = _seq_masks(t, COL_CHUNK, seq_len)
    acc_ref[...] = jnp.zeros_like(acc_ref)

    def body(j, carry):
        cg = pl.multiple_of(j * COL_CHUNK, COL_CHUNK)
        cv = pl.multiple_of(FFN_DIM + j * COL_CHUNK, LANES)
        h = hext_ref[...]
        ug = jnp.dot(h, wup_ref[:, pl.ds(cg, COL_CHUNK)], preferred_element_type=F32)
        uv = jnp.dot(h, wup_ref[:, pl.ds(cv, COL_CHUNK)], preferred_element_type=F32)
        gate = _conv3(ug, cw_ref[:, pl.ds(cg, COL_CHUNK)], has_prev, has_next, t)
        val = _conv3(uv, cw_ref[:, pl.ds(cv, COL_CHUNK)], has_prev, has_next, t)
        a = (_silu(gate) * val).astype(BF16)
        acc_ref[...] += jnp.dot(a, wdn_ref[pl.ds(cg, COL_CHUNK), :], preferred_element_type=F32)
        return carry

    lax.fori_loop(0, FFN_DIM // COL_CHUNK, body, 0)
    o_ref[...] = x_ref[...] + gt * _rms(acc_ref[...], g_ref[3:4])


def _ffn(x, mod, g, w_up, conv_w, w_down, seq_len):
    n = x.shape[0]
    t = ROW_TILE
    return pl.pallas_call(
        functools.partial(_ffn_kernel, seq_len=seq_len),
        grid=(n // t,),
        in_specs=_halo_specs(n, t, D_MODEL) + [
            _mod_spec(mod.shape[0], t, seq_len), _const_spec(g.shape), _const_spec(w_up.shape),
            _const_spec(conv_w.shape), _const_spec(w_down.shape)],
        out_specs=pl.BlockSpec((t, D_MODEL), lambda i: (i, 0)),
        out_shape=jax.ShapeDtypeStruct((n, D_MODEL), F32),
        scratch_shapes=[pltpu.VMEM((t + 2 * HALO, D_MODEL), BF16), pltpu.VMEM((t, D_MODEL), F32)],
        compiler_params=_cparams(1),
        name="conv_ffn",
    )(x, x, x, mod, g, w_up, conv_w, w_down)


def _sconv_kernel(xp_ref, x_ref, xn_ref, mod_ref, g_ref, win_ref, cw_ref, wout_ref, o_ref,
                  hext_ref, acc_ref, *, seq_len):
    t = x_ref.shape[0]
    d = D_MODEL
    sh, sc, gt = _mod_slices(mod_ref[0], 0)
    hext_ref[...] = _ext_norm(xp_ref, x_ref, xn_ref, g_ref[0:1], sc, sh)
    has_prev, has_next = _seq_masks(t, COL_CHUNK, seq_len)
    acc_ref[...] = jnp.zeros_like(acc_ref)

    def body(j, carry):
        c0 = pl.multiple_of(j * COL_CHUNK, COL_CHUNK)
        h = hext_ref[...]
        bg = jnp.dot(h[HALO:HALO + t], win_ref[:, pl.ds(c0, COL_CHUNK)], preferred_element_type=F32)
        cg = jnp.dot(h, win_ref[:, pl.ds(pl.multiple_of(d + c0, COL_CHUNK), COL_CHUNK)],
                     preferred_element_type=F32)
        u = jnp.dot(h, win_ref[:, pl.ds(pl.multiple_of(2 * d + c0, COL_CHUNK), COL_CHUNK)],
                    preferred_element_type=F32)
        conv = _conv3(cg * u, cw_ref[:, pl.ds(c0, COL_CHUNK)], has_prev, has_next, t)
        a = (bg * conv).astype(BF16)
        acc_ref[...] += jnp.dot(a, wout_ref[pl.ds(c0, COL_CHUNK), :], preferred_element_type=F32)
        return carry

    lax.fori_loop(0, d // COL_CHUNK, body, 0)
    o_ref[...] = x_ref[...] + gt * _rms(acc_ref[...], g_ref[1:2])


def _sconv(x, mod, g, w_in, conv_w, w_out, seq_len):
    n = x.shape[0]
    t = ROW_TILE
    return pl.pallas_call(
        functools.partial(_sconv_kernel, seq_len=seq_len),
        grid=(n // t,),
        in_specs=_halo_specs(n, t, D_MODEL) + [
            _mod_spec(mod.shape[0], t, seq_len), _const_spec(g.shape), _const_spec(w_in.shape),
            _const_spec(conv_w.shape), _const_spec(w_out.shape)],
        out_specs=pl.BlockSpec((t, D_MODEL), lambda i: (i, 0)),
        out_shape=jax.ShapeDtypeStruct((n, D_MODEL), F32),
        scratch_shapes=[pltpu.VMEM((t + 2 * HALO, D_MODEL), BF16), pltpu.VMEM((t, D_MODEL), F32)],
        compiler_params=_cparams(1),
        name="short_conv_mixer",
    )(x, x, x, mod, g, w_in, conv_w, w_out)


def _ssd_in_kernel(xp_ref, x_ref, xn_ref, mod_ref, g_ref, win_ref, cw_ref, cb_ref, dtb_ref,
                   zT_ref, xT_ref, bm_ref, cmT_ref, dt_ref, hext_ref, *, seq_len):
    t = x_ref.shape[0]
    di = SSM_D_INNER
    gn = SSM_GROUPS * SSM_STATE
    sh, sc, _ = _mod_slices(mod_ref[0], 0)
    hext_ref[...] = _ext_norm(xp_ref, x_ref, xn_ref, g_ref[0:1], sc, sh)
    has_prev, has_next = _seq_masks(t, COL_CHUNK, seq_len)

    def z_body(j, carry):
        c0 = pl.multiple_of(j * COL_CHUNK, COL_CHUNK)
        h = hext_ref[pl.ds(HALO, t), :]
        z = jnp.dot(h, win_ref[:, pl.ds(c0, COL_CHUNK)], preferred_element_type=F32)
        zT_ref[pl.ds(c0, COL_CHUNK), :] = z.T
        return carry

    lax.fori_loop(0, di // COL_CHUNK, z_body, 0)

    def conv_act(c0):
        u = jnp.dot(hext_ref[...], win_ref[:, pl.ds(pl.multiple_of(di + c0, COL_CHUNK), COL_CHUNK)],
                    preferred_element_type=F32)
        conv = _conv3(u, cw_ref[:, pl.ds(c0, COL_CHUNK)], has_prev, has_next, t)
        return _silu(conv + cb_ref[:, pl.ds(c0, COL_CHUNK)])

    def x_body(j, carry):
        c0 = pl.multiple_of(j * COL_CHUNK, COL_CHUNK)
        xT_ref[pl.ds(c0, COL_CHUNK), :] = conv_act(c0).T
        return carry

    lax.fori_loop(0, di // COL_CHUNK, x_body, 0)

    def b_body(j, carry):
        c0 = pl.multiple_of(j * COL_CHUNK, COL_CHUNK)
        bm_ref[:, pl.ds(c0, COL_CHUNK)] = conv_act(pl.multiple_of(di + c0, COL_CHUNK)).astype(BF16)
        return carry

    lax.fori_loop(0, gn // COL_CHUNK, b_body, 0)

    def c_body(j, carry):
        c0 = pl.multiple_of(j * COL_CHUNK, COL_CHUNK)
        cmT_ref[pl.ds(c0, COL_CHUNK), :] = conv_act(
            pl.multiple_of(di + gn + c0, COL_CHUNK)).T.astype(BF16)
        return carry

    lax.fori_loop(0, gn // COL_CHUNK, c_body, 0)

    raw = jnp.dot(hext_ref[pl.ds(HALO, t), :], win_ref[:, di + SSM_CONV_DIM:di + SSM_CONV_DIM + LANES],
                  preferred_element_type=F32) + dtb_ref[...]
    dt_ref[...] = jnp.maximum(raw, 0.0) + jnp.log1p(jnp.exp(-jnp.abs(raw)))


def _ssd_in(x, mod, g, w_in, conv_w, conv_b, dt_bias, seq_len):
    n = x.shape[0]
    t = ROW_TILE
    gn = SSM_GROUPS * SSM_STATE
    return pl.pallas_call(
        functools.partial(_ssd_in_kernel, seq_len=seq_len),
        grid=(n // t,),
        in_specs=_halo_specs(n, t, D_MODEL) + [
            _mod_spec(mod.shape[0], t, seq_len), _const_spec(g.shape), _const_spec(w_in.shape),
            _const_spec(conv_w.shape), _const_spec(conv_b.shape), _const_spec(dt_bias.shape)],
        out_specs=[pl.BlockSpec((SSM_D_INNER, t), lambda i: (0, i)),
                   pl.BlockSpec((SSM_D_INNER, t), lambda i: (0, i)),
                   pl.BlockSpec((t, gn), lambda i: (i, 0)),
                   pl.BlockSpec((gn, t), lambda i: (0, i)),
                   pl.BlockSpec((t, LANES), lambda i: (i, 0))],
        out_shape=[jax.ShapeDtypeStruct((SSM_D_INNER, n), F32),
                   jax.ShapeDtypeStruct((SSM_D_INNER, n), F32),
                   jax.ShapeDtypeStruct((n, gn), BF16),
                   jax.ShapeDtypeStruct((gn, n), BF16),
                   jax.ShapeDtypeStruct((n, LANES), F32)],
        scratch_shapes=[pltpu.VMEM((t + 2 * HALO, D_MODEL), BF16)],
        compiler_params=_cparams(1),
        name="ssd_in_proj",
    )(x, x, x, mod, g, w_in, conv_w, conv_b, dt_bias)


def _split3(v):
    hi = v.astype(BF16)
    r1 = v - hi.astype(F32)
    mid = r1.astype(BF16)
    lo = (r1 - mid.astype(F32)).astype(BF16)
    return hi, mid, lo


def _ssd_scan_kernel(*refs, reverse, has_h0, finish, emit_state, n_chunks):
    refs = list(refs)
    xT_ref, bm_ref, cmT_ref, dt_ref, alog_ref = refs[:5]
    pos = 5
    h0_ref = None
    if has_h0:
        h0_ref = refs[pos]
        pos += 1
    if finish:
        (zT_ref, ybT_ref, x_ref, mod_ref, g_ref, dexp_ref, ng_ref, woutT_ref) = refs[pos:pos + 8]
        pos += 8
    y_out_ref = refs[pos]
    pos += 1
    state_out_ref = None
    if emit_state:
        state_out_ref = refs[pos]
        pos += 1
    st_ref = refs[pos]
    yT_ref = refs[pos + 1] if finish else y_out_ref

    q = SCAN_CHUNK
    c = pl.program_id(1)

    @pl.when(c == 0)
    def _():
        if has_h0:
            st_ref[...] = h0_ref[0]
        else:
            st_ref[...] = jnp.zeros_like(st_ref)

    dt = dt_ref[...]
    da = dt * (-jnp.exp(alog_ref[...]))
    r_i = lax.broadcasted_iota(jnp.int32, (q, q), 0)
    c_i = lax.broadcasted_iota(jnp.int32, (q, q), 1)
    tri = jnp.where((c_i >= r_i) if reverse else (c_i <= r_i), 1.0, 0.0).astype(BF16)
    acs = sum(jnp.dot(tri, part, preferred_element_type=F32) for part in _split3(da))
    mask = (r_i >= c_i) if reverse else (r_i <= c_i)
    acsT = acs.T
    dtT = dt.T
    last = 0 if reverse else q - 1
    totT = acsT[:, last:last + 1]
    wT = jnp.exp(totT - acsT) * dtT
    eaT = jnp.exp(acsT)
    etot = jnp.broadcast_to(jnp.exp(totT), (LANES, SSM_STATE))
    head0 = SSM_HEADS if reverse else 0

    for g in range(SSM_GROUPS):
        bm_g = bm_ref[:, g * SSM_STATE:(g + 1) * SSM_STATE]
        cmT_g = cmT_ref[g * SSM_STATE:(g + 1) * SSM_STATE, :]
        cbT = jnp.dot(bm_g, cmT_g, preferred_element_type=F32)
        rows = slice(g * GROUP_ROWS, (g + 1) * GROUP_ROWS)
        st_g = st_ref[rows, :]
        y_off = jnp.dot(st_g.astype(BF16), cmT_g, preferred_element_type=F32)
        xT_g = xT_ref[rows, :]
        xw, scale = [], []
        for r in range(HEADS_PER_GROUP):
            col = head0 + g * HEADS_PER_GROUP + r
            hr = slice(r * SSM_HEAD_DIM, (r + 1) * SSM_HEAD_DIM)
            decay = jnp.exp(acsT[col:col + 1, :] - acs[:, col:col + 1])
            mT = jnp.where(mask, cbT * decay * dt[:, col:col + 1], 0.0).astype(BF16)
            xh = xT_g[hr, :]
            yh = jnp.dot(xh.astype(BF16), mT, preferred_element_type=F32)
            yh = yh + y_off[hr, :] * eaT[col:col + 1, :]
            yT_ref[g * GROUP_ROWS + r * SSM_HEAD_DIM:g * GROUP_ROWS + (r + 1) * SSM_HEAD_DIM, :] = yh
            xw.append((xh * wT[col:col + 1, :]).astype(BF16))
            scale.append(jnp.broadcast_to(etot[col:col + 1, :], (SSM_HEAD_DIM, SSM_STATE)))
        d_state = jnp.dot(jnp.concatenate(xw, axis=0), bm_g, preferred_element_type=F32)
        st_ref[rows, :] = st_g * jnp.concatenate(scale, axis=0) + d_state

    if emit_state:
        @pl.when(c == n_chunks - 1)
        def _():
            state_out_ref[0] = st_ref[...]

    if finish:
        y = yT_ref[...] + ybT_ref[...] + dexp_ref[...] * xT_ref[...]
        y = y * _silu(zT_ref[...])
        ms = jnp.mean(y * y, axis=0, keepdims=True)
        y = y * lax.rsqrt(ms + NORM_EPS) * ng_ref[...]
        outT = jnp.dot(woutT_ref[...], y.astype(BF16), preferred_element_type=F32)
        _, _, gt = _mod_slices(mod_ref[0], 0)
        y_out_ref[...] = x_ref[...] + gt * _rms(outT.T, g_ref[1:2])


def _ssd_scan(xT, bm, cmT, dt, a_log, h0, fin, *, batch, reverse, emit_state):
    n = xT.shape[1]
    q = SCAN_CHUNK
    nc = n // batch // q
    gn = SSM_GROUPS * SSM_STATE
    finish = fin is not None

    def tok(b, c):
        return b * nc + ((nc - 1 - c) if reverse else c)

    in_specs = [pl.BlockSpec((SSM_D_INNER, q), lambda b, c: (0, tok(b, c))),
                pl.BlockSpec((q, gn), lambda b, c: (tok(b, c), 0)),
                pl.BlockSpec((gn, q), lambda b, c: (0, tok(b, c))),
                pl.BlockSpec((q, LANES), lambda b, c: (tok(b, c), 0)),
                _const_spec(a_log.shape)]
    args = [xT, bm, cmT, dt, a_log]
    if h0 is not None:
        in_specs.append(pl.BlockSpec((1, SSM_D_INNER, SSM_STATE), lambda b, c: (b, 0, 0)))
        args.append(h0)
    if finish:
        zT, ybT, x, mod, g, dexp, ng, woutT = fin
        n_mod = mod.shape[0]
        in_specs += [pl.BlockSpec((SSM_D_INNER, q), lambda b, c: (0, tok(b, c))),
                     pl.BlockSpec((SSM_D_INNER, q), lambda b, c: (0, tok(b, c))),
                     pl.BlockSpec((q, D_MODEL), lambda b, c: (tok(b, c), 0)),
                     pl.BlockSpec((1, 1, 6 * D_MODEL),
                                  (lambda b, c: (b, 0, 0)) if n_mod > 1 else (lambda b, c: (0, 0, 0))),
                     _const_spec(g.shape), _const_spec(dexp.shape), _const_spec(ng.shape),
                     _const_spec(woutT.shape)]
        args += [zT, ybT, x, mod, g, dexp, ng, woutT]
        out_specs = [pl.BlockSpec((q, D_MODEL), lambda b, c: (tok(b, c), 0))]
        out_shape = [jax.ShapeDtypeStruct((n, D_MODEL), F32)]
    else:
        out_specs = [pl.BlockSpec((SSM_D_INNER, q), lambda b, c: (0, tok(b, c)))]
        out_shape = [jax.ShapeDtypeStruct((SSM_D_INNER, n), F32)]
    if emit_state:
        out_specs.append(pl.BlockSpec((1, SSM_D_INNER, SSM_STATE), lambda b, c: (b, 0, 0)))
        out_shape.append(jax.ShapeDtypeStruct((batch, SSM_D_INNER, SSM_STATE), F32))
    scratch = [pltpu.VMEM((SSM_D_INNER, SSM_STATE), F32)]
    if finish:
        scratch.append(pltpu.VMEM((SSM_D_INNER, q), F32))
    outs = pl.pallas_call(
        functools.partial(_ssd_scan_kernel, reverse=reverse, has_h0=h0 is not None, finish=finish,
                          emit_state=emit_state, n_chunks=nc),
        grid=(batch, nc),
        in_specs=in_specs, out_specs=out_specs, out_shape=out_shape,
        scratch_shapes=scratch,
        compiler_params=_cparams(2),
        name="ssd_scan_bwd" if reverse else "ssd_scan_fwd",
    )(*args)
    return outs if emit_state else (outs[0], None)


def _ssd_layer(x, mod, g, p, h0, batch, seq_len):
    zT, xT, bm, cmT, dt = _ssd_in(x, mod, g, p["w_in"], p["conv_w"], p["conv_b"], p["dt_bias"], seq_len)
    emit = h0 is None
    h0f = None if h0 is None else h0[:, 0]
    h0b = None if h0 is None else h0[:, 1]
    ybT, s_b = _ssd_scan(xT, bm, cmT, dt, p["a_log"], h0b, None, batch=batch, reverse=True,
                         emit_state=emit)
    fin = (zT, ybT, x, mod, g, p["d_exp"], p["norm_g"], p["w_outT"])
    x_new, s_f = _ssd_scan(xT, bm, cmT, dt, p["a_log"], h0f, fin, batch=batch, reverse=False,
                           emit_state=emit)
    return x_new, (s_f, s_b)


def _qkv_kernel(*refs, rotary):
    if rotary:
        x_ref, mod_ref, g_ref, w_ref, cos_ref, sin_ref, q_ref, k_ref, v_ref = refs
    else:
        x_ref, mod_ref, g_ref, w_ref, q_ref, k_ref, v_ref = refs
    d = D_MODEL
    sh, sc, _ = _mod_slices(mod_ref[0], 0)
    h = (_rms(x_ref[...], g_ref[0:1]) * (1.0 + sc) + sh).astype(BF16)
    if rotary:
        lane = lax.broadcasted_iota(jnp.int32, (x_ref.shape[0], LANES), 1)
        first_half = (lane & (DIFF_HEAD_DIM - 1)) < DIFF_HEAD_DIM // 2
        cos = cos_ref[...]
        sin = sin_ref[...]

    def rot(a):
        if not rotary:
            return a
        partner = jnp.where(first_half, pltpu.roll(a, LANES - DIFF_HEAD_DIM // 2, 1),
                            pltpu.roll(a, DIFF_HEAD_DIM // 2, 1))
        return a * cos + partner * sin

    scale = DIFF_HEAD_DIM ** -0.5
    for j in range(d // LANES):
        cs = slice(j * LANES, (j + 1) * LANES)
        qj = jnp.dot(h, w_ref[:, j * LANES:(j + 1) * LANES], preferred_element_type=F32)
        kj = jnp.dot(h, w_ref[:, d + j * LANES:d + (j + 1) * LANES], preferred_element_type=F32)
        q_ref[:, cs] = (rot(qj) * scale).astype(q_ref.dtype)
        k_ref[:, cs] = rot(kj).astype(k_ref.dtype)
    v_ref[...] = jnp.dot(h, w_ref[:, 2 * d:3 * d], preferred_element_type=F32).astype(v_ref.dtype)


def _qkv(x, mod, g, w_qkv, rot_tables, seq_len, kv_dtype):
    n = x.shape[0]
    t = ROW_TILE
    rotary = rot_tables is not None
    in_specs = [pl.BlockSpec((t, D_MODEL), lambda i: (i, 0)), _mod_spec(mod.shape[0], t, seq_len),
                _const_spec(g.shape), _const_spec(w_qkv.shape)]
    args = [x, mod, g, w_qkv]
    if rotary:
        per_seq = seq_len // t
        in_specs += [pl.BlockSpec((t, LANES), lambda i: (i % per_seq, 0))] * 2
        args += list(rot_tables)
    row_spec = pl.BlockSpec((t, D_MODEL), lambda i: (i, 0))
    return pl.pallas_call(
        functools.partial(_qkv_kernel, rotary=rotary),
        grid=(n // t,),
        in_specs=in_specs,
        out_specs=[row_spec, row_spec, row_spec],
        out_shape=[jax.ShapeDtypeStruct((n, D_MODEL), BF16),
                   jax.ShapeDtypeStruct((n, D_MODEL), kv_dtype),
                   jax.ShapeDtypeStruct((n, D_MODEL), kv_dtype)],
        compiler_params=_cparams(1),
        name="diff_qkv",
    )(*args)


def _attn_kernel(*refs, has_cache, lam_init):
    if has_cache:
        (q_ref, k_ref, v_ref, ck_ref, cv_ref, x_ref, mod_ref, g_ref, lp_ref, sg_ref, wo_ref,
         o_ref, oall_ref) = refs
    else:
        q_ref, k_ref, v_ref, x_ref, mod_ref, g_ref, lp_ref, sg_ref, wo_ref, o_ref, oall_ref = refs
    lp = lp_ref[...]
    lam = (jnp.exp(jnp.sum(lp[0:1] * lp[1:2], axis=-1, keepdims=True))
           - jnp.exp(jnp.sum(lp[2:3] * lp[3:4], axis=-1, keepdims=True)) + lam_init)
    dn = (((1,), (1,)), ((), ()))
    hd = DIFF_HEAD_DIM

    def probs(hh):
        hs = slice(hh * hd, (hh + 1) * hd)
        qh = q_ref[:, hs]
        s_new = lax.dot_general(qh, k_ref[:, hs].astype(BF16), dn, preferred_element_type=F32)
        m = jnp.max(s_new, axis=-1, keepdims=True)
        if has_cache:
            s_old = lax.dot_general(qh, ck_ref[0, :, hs].astype(BF16), dn, preferred_element_type=F32)
            m = jnp.maximum(m, jnp.max(s_old, axis=-1, keepdims=True))
            p_old = jnp.exp(s_old - m)
        p_new = jnp.exp(s_new - m)
        l = jnp.sum(p_new, axis=-1, keepdims=True)
        if has_cache:
            l = l + jnp.sum(p_old, axis=-1, keepdims=True)
            return p_old, p_new, 1.0 / l
        return None, p_new, 1.0 / l

    for hp in range(DIFF_HEADS):
        po0, pn0, r0 = probs(2 * hp)
        po1, pn1, r1 = probs(2 * hp + 1)
        r1 = r1 * lam
        vs = slice(hp * 2 * hd, (hp + 1) * 2 * hd)
        att = (pn0 * r0 - pn1 * r1).astype(BF16)
        o = jnp.dot(att, v_ref[:, vs].astype(BF16), preferred_element_type=F32)
        if has_cache:
            att_old = (po0 * r0 - po1 * r1).astype(BF16)
            o = o + jnp.dot(att_old, cv_ref[0, :, vs].astype(BF16), preferred_element_type=F32)
        oall_ref[:, vs] = (_rms(o, sg_ref[...]) * (1.0 - lam_init)).astype(BF16)

    m_out = jnp.dot(oall_ref[...], wo_ref[...], preferred_element_type=F32)
    _, _, gt = _mod_slices(mod_ref[0], 0)
    o_ref[...] = x_ref[...] + gt * _rms(m_out, g_ref[1:2])


def _attention(q, k, v, cache, x, mod, g, lam_p, subln_g, w_o, *, batch, lam_init):
    n = x.shape[0]
    seq = n // batch
    tq = ROW_TILE
    nq = seq // tq
    has_cache = cache is not None
    kv_spec = pl.BlockSpec((seq, D_MODEL), lambda b, t: (b, 0))
    in_specs = [pl.BlockSpec((tq, D_MODEL), lambda b, t: (b * nq + t, 0)), kv_spec, kv_spec]
    args = [q, k, v]
    if has_cache:
        past = cache[0].shape[1]
        c_spec = pl.BlockSpec((1, past, D_MODEL), lambda b, t: (b, 0, 0))
        in_specs += [c_spec, c_spec]
        args += list(cache)
    n_mod = mod.shape[0]
    in_specs += [pl.BlockSpec((tq, D_MODEL), lambda b, t: (b * nq + t, 0)),
                 pl.BlockSpec((1, 1, 6 * D_MODEL),
                              (lambda b, t: (b, 0, 0)) if n_mod > 1 else (lambda b, t: (0, 0, 0))),
                 _const_spec(g.shape), _const_spec(lam_p.shape), _const_spec(subln_g.shape),
                 _const_spec(w_o.shape)]
    args += [x, mod, g, lam_p, subln_g, w_o]
    return pl.pallas_call(
        functools.partial(_attn_kernel, has_cache=has_cache, lam_init=lam_init),
        grid=(batch, nq),
        in_specs=in_specs,
        out_specs=pl.BlockSpec((tq, D_MODEL), lambda b, t: (b * nq + t, 0)),
        out_shape=jax.ShapeDtypeStruct((n, D_MODEL), F32),
        scratch_shapes=[pltpu.VMEM((tq, D_MODEL), BF16)],
        compiler_params=_cparams(2),
        name="diff_attention",
    )(*args)


def _rotary_tables(n_tokens):
    rows = n_tokens // GRID_W
    row = jnp.repeat(jnp.arange(rows, dtype=F32), GRID_W)
    col = jnp.tile(jnp.arange(GRID_W, dtype=F32), rows)
    inv = ROPE_THETA ** (-jnp.arange(ROT_PAIRS_PER_AXIS, dtype=F32) / ROT_PAIRS_PER_AXIS)
    ang = jnp.concatenate([row[:, None] * inv, col[:, None] * inv], axis=-1)
    cos, sin = jnp.cos(ang), jnp.sin(ang)
    reps = LANES // DIFF_HEAD_DIM
    return (jnp.tile(jnp.concatenate([cos, cos], axis=-1), (1, reps)),
            jnp.tile(jnp.concatenate([-sin, sin], axis=-1), (1, reps)))


def kernel(x_prompt, x_sample, state_ssm, cache_k, cache_v, c, c_ctx, w_mod, b_mod, norm_g, ssd_w_in, ssd_conv_w, ssd_conv_b, ssd_dt_bias, ssd_a_log, ssd_d, ssd_norm_g, ssd_w_out, sc_w_in, sc_conv_w, sc_w_out, da_w_qkv, da_lambda, da_subln_g, da_w_out, ffn_w_up, ffn_conv_w, ffn_w_down):
    bp, lp_, d = x_prompt.shape
    bs, ls, _ = x_sample.shape
    xp = x_prompt.reshape(bp * lp_, d)
    xs = x_sample.reshape(bs * ls, d)

    n_cond = 1 + bs
    cond = jnp.concatenate([c_ctx[None], c, jnp.zeros((HALO - n_cond, d), F32)], axis=0)
    mod_all = _modulation(cond, w_mod, b_mod)

    rot = _rotary_tables(ls)
    new_ssm, new_k, new_v = [], [], []
    for l in range(DEPTH):
        j = l // N_MIXERS
        kind = l % N_MIXERS
        mod_p = mod_all[l, 0:1][:, None]
        mod_s = mod_all[l, 1:n_cond][:, None]
        g = norm_g[l]
        if kind == 0:
            pad = LANES - 2 * SSM_HEADS
            p = {
                "w_in": jnp.pad(ssd_w_in[j], ((0, 0), (0, pad))).astype(BF16),
                "conv_w": ssd_conv_w[j],
                "conv_b": ssd_conv_b[j][None],
                "dt_bias": jnp.pad(ssd_dt_bias[j].reshape(1, -1), ((0, 0), (0, pad))),
                "a_log": jnp.pad(ssd_a_log[j].reshape(1, -1), ((0, 0), (0, pad))),
                "d_exp": jnp.broadcast_to(jnp.repeat(ssd_d[j], SSM_HEAD_DIM)[:, None],
                                          (SSM_D_INNER, SCAN_CHUNK)),
                "norm_g": jnp.broadcast_to(ssd_norm_g[j][:, None], (SSM_D_INNER, SCAN_CHUNK)),
                "w_outT": ssd_w_out[j].T.astype(BF16),
            }
            h0 = state_ssm[:, j].reshape(bs, 2, SSM_D_INNER, SSM_STATE)
            xp, (s_f, s_b) = _ssd_layer(xp, mod_p, g, p, None, bp, lp_)
            xs, _ = _ssd_layer(xs, mod_s, g, p, h0, bs, ls)
            new_ssm.append(jnp.stack([s_f, s_b], axis=1).reshape(
                bp, 2, SSM_HEADS, SSM_HEAD_DIM, SSM_STATE))
        elif kind == 1:
            w_in, w_out = sc_w_in[j].astype(BF16), sc_w_out[j].astype(BF16)
            xp = _sconv(xp, mod_p, g, w_in, sc_conv_w[j], w_out, lp_)
            xs = _sconv(xs, mod_s, g, w_in, sc_conv_w[j], w_out, ls)
        else:
            lam_init = 0.8 - 0.6 * math.exp(-0.3 * l)
            w_qkv, w_o = da_w_qkv[j].astype(BF16), da_w_out[j].astype(BF16)
            sg = da_subln_g[j][None]
            qp, kp, vp = _qkv(xp, mod_p, g, w_qkv, None, lp_, F32)
            xp = _attention(qp, kp, vp, None, xp, mod_p, g, da_lambda[j], sg, w_o,
                            batch=bp, lam_init=lam_init)
            new_k.append(kp.reshape(bp, lp_, 2 * DIFF_HEADS, DIFF_HEAD_DIM))
            new_v.append(vp.reshape(bp, lp_, DIFF_HEADS, 2 * DIFF_HEAD_DIM))
            qs, ks, vs = _qkv(xs, mod_s, g, w_qkv, rot, ls, BF16)
            cache = (cache_k[:, j].reshape(bs, -1, d), cache_v[:, j].reshape(bs, -1, d))
            xs = _attention(qs, ks, vs, cache, xs, mod_s, g, da_lambda[j], sg, w_o,
                            batch=bs, lam_init=lam_init)
        w_up, w_dn = ffn_w_up[l].astype(BF16), ffn_w_down[l].astype(BF16)
        xp = _ffn(xp, mod_p, g, w_up, ffn_conv_w[l], w_dn, lp_)
        xs = _ffn(xs, mod_s, g, w_up, ffn_conv_w[l], w_dn, ls)

    return (xp.reshape(bp, lp_, d), xs.reshape(bs, ls, d), jnp.stack(new_ssm, axis=1),
            jnp.stack(new_k, axis=1), jnp.stack(new_v, axis=1))
```

```python
import functools
import math

import jax
import jax.numpy as jnp
from jax import lax
from jax.experimental import pallas as pl
from jax.experimental.pallas import tpu as pltpu

D_MODEL = 1024
DEPTH = 4
GRID_W = 64
N_MIXERS = 3
SSM_D_INNER = 2 * D_MODEL
SSM_HEAD_DIM = 64
SSM_HEADS = SSM_D_INNER // SSM_HEAD_DIM
SSM_GROUPS = 8
SSM_STATE = 128
SSM_CONV_DIM = SSM_D_INNER + 2 * SSM_GROUPS * SSM_STATE
HEADS_PER_GROUP = SSM_HEADS // SSM_GROUPS
GROUP_ROWS = HEADS_PER_GROUP * SSM_HEAD_DIM
DIFF_HEAD_DIM = 64
DIFF_HEADS = D_MODEL // (2 * DIFF_HEAD_DIM)
ROPE_THETA = 10000.0
ROT_PAIRS_PER_AXIS = DIFF_HEAD_DIM // 4
FFN_DIM = 2816
NORM_EPS = 1e-6

LANES = 128
HALO = 8
SCAN_CHUNK = 128
SCAN_STEP = 256
ROW_TILE = 256
COL_CHUNK = 256
FFN_DOWN_GROUP = 4
VMEM_LIMIT = 56 * 1024 * 1024

F32 = jnp.float32
BF16 = jnp.bfloat16


def _cparams(n_axes):
    return pltpu.CompilerParams(dimension_semantics=("arbitrary",) * n_axes,
                                vmem_limit_bytes=VMEM_LIMIT)


def _rms(x, g):
    ms = jnp.mean(x * x, axis=-1, keepdims=True)
    return x * lax.rsqrt(ms + NORM_EPS) * g


def _silu(x):
    return x * jax.nn.sigmoid(x)


def _mod_slices(mod, first):
    d = D_MODEL
    return (mod[:, first * d:(first + 1) * d], mod[:, (first + 1) * d:(first + 2) * d],
            mod[:, (first + 2) * d:(first + 3) * d])


def _ext_norm(xp_ref, x_ref, xn_ref, g, sc, sh, seq_len):
    tiles_per_seq = seq_len // x_ref.shape[0]
    k = pl.program_id(0) % tiles_per_seq
    keep = (jnp.where(k == 0, 0.0, 1.0), None, jnp.where(k == tiles_per_seq - 1, 0.0, 1.0))
    pieces = []
    for r, kp in zip((xp_ref, x_ref, xn_ref), keep):
        h = _rms(r[...], g) * (1.0 + sc) + sh
        pieces.append(h if kp is None else h * kp)
    return jnp.concatenate(pieces, axis=0).astype(BF16)


def _conv3(u_ext, w3, tile_rows):
    n_ext = tile_rows + 2 * HALO
    up = pltpu.roll(u_ext, 1, 0)[HALO:HALO + tile_rows]
    un = pltpu.roll(u_ext, n_ext - 1, 0)[HALO:HALO + tile_rows]
    um = u_ext[HALO:HALO + tile_rows]
    return up * w3[0:1] + um * w3[1:2] + un * w3[2:3]


def _halo_specs(n_rows, tile_rows, width):
    per = tile_rows // HALO
    last = n_rows // HALO - 1
    return [
        pl.BlockSpec((HALO, width), lambda i: (jnp.maximum(i * per - 1, 0), 0)),
        pl.BlockSpec((tile_rows, width), lambda i: (i, 0)),
        pl.BlockSpec((HALO, width), lambda i: (jnp.minimum((i + 1) * per, last), 0)),
    ]


def _mod_spec(n_mod, tile_rows, seq_len):
    if n_mod == 1:
        return pl.BlockSpec((1, 1, 6 * D_MODEL), lambda i: (0, 0, 0))
    return pl.BlockSpec((1, 1, 6 * D_MODEL), lambda i: ((i * tile_rows) // seq_len, 0, 0))


def _const_spec(shape):
    nd = len(shape)
    return pl.BlockSpec(shape, lambda *_: (0,) * nd)


def _mod_kernel(cond_ref, w_ref, b_ref, o_ref):
    a = _silu(cond_ref[...]).astype(BF16)
    o_ref[0] = jnp.dot(a, w_ref[0].astype(BF16), preferred_element_type=F32) + b_ref[0]


def _modulation(cond, w_mod, b_mod):
    n_cols = 6 * D_MODEL
    blk = n_cols // 4
    return pl.pallas_call(
        _mod_kernel,
        grid=(DEPTH, n_cols // blk),
        in_specs=[_const_spec(cond.shape),
                  pl.BlockSpec((1, D_MODEL, blk), lambda l, j: (l, 0, j)),
                  pl.BlockSpec((1, 1, blk), lambda l, j: (l, 0, j))],
        out_specs=pl.BlockSpec((1, cond.shape[0], blk), lambda l, j: (l, 0, j)),
        out_shape=jax.ShapeDtypeStruct((DEPTH, cond.shape[0], n_cols), F32),
        compiler_params=_cparams(2),
        name="modulation",
    )(cond, w_mod, b_mod.reshape(DEPTH, 1, n_cols))


def _ffn_kernel(xp_ref, x_ref, xn_ref, mod_ref, g_ref, wup_ref, cw_ref, wdn_ref, o_ref,
                hext_ref, act_ref, *, seq_len):
    t = x_ref.shape[0]
    sh, sc, gt = _mod_slices(mod_ref[0], 3)
    hext_ref[...] = _ext_norm(xp_ref, x_ref, xn_ref, g_ref[2:3], sc, sh, seq_len)
    n_chunks = FFN_DIM // COL_CHUNK
    acc = None
    for j0 in range(0, n_chunks, FFN_DOWN_GROUP):
        n_j = min(FFN_DOWN_GROUP, n_chunks - j0)
        for jj in range(n_j):
            cg = (j0 + jj) * COL_CHUNK
            cv = FFN_DIM + cg
            h = hext_ref[...]
            ug = jnp.dot(h, wup_ref[:, cg:cg + COL_CHUNK], preferred_element_type=F32)
            uv = jnp.dot(h, wup_ref[:, cv:cv + COL_CHUNK], preferred_element_type=F32)
            gate = _conv3(ug, cw_ref[:, cg:cg + COL_CHUNK], t)
            val = _conv3(uv, cw_ref[:, cv:cv + COL_CHUNK], t)
            act_ref[:, jj * COL_CHUNK:(jj + 1) * COL_CHUNK] = (_silu(gate) * val).astype(BF16)
        k0, kn = j0 * COL_CHUNK, n_j * COL_CHUNK
        part = jnp.dot(act_ref[:, :kn], wdn_ref[k0:k0 + kn, :], preferred_element_type=F32)
        acc = part if acc is None else acc + part
    o_ref[...] = x_ref[...] + gt * _rms(acc, g_ref[3:4])


def _ffn(x, mod, g, w_up, conv_w, w_down, seq_len):
    n = x.shape[0]
    t = ROW_TILE
    return pl.pallas_call(
        functools.partial(_ffn_kernel, seq_len=seq_len),
        grid=(n // t,),
        in_specs=_halo_specs(n, t, D_MODEL) + [
            _mod_spec(mod.shape[0], t, seq_len), _const_spec(g.shape), _const_spec(w_up.shape),
            _const_spec(conv_w.shape), _const_spec(w_down.shape)],
        out_specs=pl.BlockSpec((t, D_MODEL), lambda i: (i, 0)),
        out_shape=jax.ShapeDtypeStruct((n, D_MODEL), F32),
        scratch_shapes=[pltpu.VMEM((t + 2 * HALO, D_MODEL), BF16),
                        pltpu.VMEM((t, FFN_DOWN_GROUP * COL_CHUNK), BF16)],
        compiler_params=_cparams(1),
        name="conv_ffn",
    )(x, x, x, mod, g, w_up, conv_w, w_down)


def _sconv_kernel(xp_ref, x_ref, xn_ref, mod_ref, g_ref, win_ref, cw_ref, wout_ref, o_ref,
                  hext_ref, act_ref, *, seq_len):
    t = x_ref.shape[0]
    d = D_MODEL
    sh, sc, gt = _mod_slices(mod_ref[0], 0)
    hext_ref[...] = _ext_norm(xp_ref, x_ref, xn_ref, g_ref[0:1], sc, sh, seq_len)
    for j in range(d // COL_CHUNK):
        c0 = j * COL_CHUNK
        bg = jnp.dot(hext_ref[HALO:HALO + t, :], win_ref[:, c0:c0 + COL_CHUNK],
                     preferred_element_type=F32)
        h = hext_ref[...]
        cg = jnp.dot(h, win_ref[:, d + c0:d + c0 + COL_CHUNK], preferred_element_type=F32)
        u = jnp.dot(h, win_ref[:, 2 * d + c0:2 * d + c0 + COL_CHUNK], preferred_element_type=F32)
        conv = _conv3(cg * u, cw_ref[:, c0:c0 + COL_CHUNK], t)
        act_ref[:, c0:c0 + COL_CHUNK] = (bg * conv).astype(BF16)
    m = jnp.dot(act_ref[...], wout_ref[...], preferred_element_type=F32)
    o_ref[...] = x_ref[...] + gt * _rms(m, g_ref[1:2])


def _sconv(x, mod, g, w_in, conv_w, w_out, seq_len):
    n = x.shape[0]
    t = ROW_TILE
    return pl.pallas_call(
        functools.partial(_sconv_kernel, seq_len=seq_len),
        grid=(n // t,),
        in_specs=_halo_specs(n, t, D_MODEL) + [
            _mod_spec(mod.shape[0], t, seq_len), _const_spec(g.shape), _const_spec(w_in.shape),
            _const_spec(conv_w.shape), _const_spec(w_out.shape)],
        out_specs=pl.BlockSpec((t, D_MODEL), lambda i: (i, 0)),
        out_shape=jax.ShapeDtypeStruct((n, D_MODEL), F32),
        scratch_shapes=[pltpu.VMEM((t + 2 * HALO, D_MODEL), BF16), pltpu.VMEM((t, D_MODEL), BF16)],
        compiler_params=_cparams(1),
        name="short_conv_mixer",
    )(x, x, x, mod, g, w_in, conv_w, w_out)


def _ssd_in_kernel(xp_ref, x_ref, xn_ref, mod_ref, g_ref, win_ref, cw_ref, cb_ref, dtb_ref,
                   z_ref, xT_ref, bm_ref, cm_ref, dt_ref, hext_ref, *, seq_len):
    t = x_ref.shape[0]
    di = SSM_D_INNER
    gn = SSM_GROUPS * SSM_STATE
    sh, sc, _ = _mod_slices(mod_ref[0], 0)
    hext_ref[...] = _ext_norm(xp_ref, x_ref, xn_ref, g_ref[0:1], sc, sh, seq_len)

    for j in range(di // COL_CHUNK):
        c0 = j * COL_CHUNK
        z_ref[:, c0:c0 + COL_CHUNK] = jnp.dot(hext_ref[HALO:HALO + t, :], win_ref[:, c0:c0 + COL_CHUNK],
                                              preferred_element_type=F32)

    for j in range(SSM_CONV_DIM // COL_CHUNK):
        c0 = j * COL_CHUNK
        u = jnp.dot(hext_ref[...], win_ref[:, di + c0:di + c0 + COL_CHUNK], preferred_element_type=F32)
        act = _silu(_conv3(u, cw_ref[:, c0:c0 + COL_CHUNK], t) + cb_ref[:, c0:c0 + COL_CHUNK])
        if c0 < di:
            xT_ref[c0:c0 + COL_CHUNK, :] = act.T
        elif c0 < di + gn:
            bm_ref[:, c0 - di:c0 - di + COL_CHUNK] = act.astype(BF16)
        else:
            cm_ref[:, c0 - di - gn:c0 - di - gn + COL_CHUNK] = act.astype(BF16)

    raw = jnp.dot(hext_ref[HALO:HALO + t, :], win_ref[:, di + SSM_CONV_DIM:di + SSM_CONV_DIM + LANES],
                  preferred_element_type=F32) + dtb_ref[...]
    dt_ref[...] = jnp.maximum(raw, 0.0) + jnp.log1p(jnp.exp(-jnp.abs(raw)))


def _ssd_in(x, mod, g, w_in, conv_w, conv_b, dt_bias, seq_len):
    n = x.shape[0]
    t = ROW_TILE
    gn = SSM_GROUPS * SSM_STATE
    return pl.pallas_call(
        functools.partial(_ssd_in_kernel, seq_len=seq_len),
        grid=(n // t,),
        in_specs=_halo_specs(n, t, D_MODEL) + [
            _mod_spec(mod.shape[0], t, seq_len), _const_spec(g.shape), _const_spec(w_in.shape),
            _const_spec(conv_w.shape), _const_spec(conv_b.shape), _const_spec(dt_bias.shape)],
        out_specs=[pl.BlockSpec((t, SSM_D_INNER), lambda i: (i, 0)),
                   pl.BlockSpec((SSM_D_INNER, t), lambda i: (0, i)),
                   pl.BlockSpec((t, gn), lambda i: (i, 0)),
                   pl.BlockSpec((t, gn), lambda i: (i, 0)),
                   pl.BlockSpec((t, LANES), lambda i: (i, 0))],
        out_shape=[jax.ShapeDtypeStruct((n, SSM_D_INNER), F32),
                   jax.ShapeDtypeStruct((SSM_D_INNER, n), F32),
                   jax.ShapeDtypeStruct((n, gn), BF16),
                   jax.ShapeDtypeStruct((n, gn), BF16),
                   jax.ShapeDtypeStruct((n, LANES), F32)],
        scratch_shapes=[pltpu.VMEM((t + 2 * HALO, D_MODEL), BF16)],
        compiler_params=_cparams(1),
        name="ssd_in_proj",
    )(x, x, x, mod, g, w_in, conv_w, conv_b, dt_bias)


def _split3(v):
    hi = v.astype(BF16)
    r1 = v - hi.astype(F32)
    mid = r1.astype(BF16)
    lo = (r1 - mid.astype(F32)).astype(BF16)
    return hi, mid, lo


_CONTRACT_LAST = (((1,), (1,)), ((), ()))


def _scan_chunk(xT_ref, bm_ref, cm_ref, dt_ref, alog_ref, st_ref, yT_ref, off, reverse):
    q = SCAN_CHUNK
    tok = slice(off, off + q)
    dt = dt_ref[tok, :]
    da = dt * (-jnp.exp(alog_ref[...]))
    r_i = lax.broadcasted_iota(jnp.int32, (q, q), 0)
    c_i = lax.broadcasted_iota(jnp.int32, (q, q), 1)
    mask = (c_i >= r_i) if reverse else (c_i <= r_i)
    tri = jnp.where(mask, 1.0, 0.0).astype(BF16)
    acs = sum(jnp.dot(tri, part, preferred_element_type=F32) for part in _split3(da))
    acsT = acs.T
    dtT = dt.T
    srcT = acsT - jnp.log(dtT)
    last = 0 if reverse else q - 1
    totT = acsT[:, last:last + 1]
    wT = jnp.exp(totT - acsT) * dtT
    eaT = jnp.exp(acsT)
    etot = jnp.broadcast_to(jnp.exp(totT), (LANES, SSM_STATE))
    head0 = SSM_HEADS if reverse else 0

    for g in range(SSM_GROUPS):
        ns = slice(g * SSM_STATE, (g + 1) * SSM_STATE)
        bm_g = bm_ref[tok, ns]
        cm_g = cm_ref[tok, ns]
        cb = lax.dot_general(cm_g, bm_g, _CONTRACT_LAST, preferred_element_type=F32)
        rows = slice(g * GROUP_ROWS, (g + 1) * GROUP_ROWS)
        st_g = st_ref[rows, :]
        y_off = lax.dot_general(st_g.astype(BF16), cm_g, _CONTRACT_LAST, preferred_element_type=F32)
        xT_g = xT_ref[rows, tok]
        xw, scale = [], []
        for r in range(HEADS_PER_GROUP):
            col = head0 + g * HEADS_PER_GROUP + r
            hr = slice(r * SSM_HEAD_DIM, (r + 1) * SSM_HEAD_DIM)
            m = jnp.where(mask, cb * jnp.exp(acs[:, col:col + 1] - srcT[col:col + 1, :]), 0.0)
            xh = xT_g[hr, :]
            yh = lax.dot_general(xh.astype(BF16), m.astype(BF16), _CONTRACT_LAST,
                                 preferred_element_type=F32)
            yh = yh + y_off[hr, :] * eaT[col:col + 1, :]
            yT_ref[g * GROUP_ROWS + r * SSM_HEAD_DIM:g * GROUP_ROWS + (r + 1) * SSM_HEAD_DIM, tok] = yh
            xw.append((xh * wT[col:col + 1, :]).astype(BF16))
            scale.append(jnp.broadcast_to(etot[col:col + 1, :], (SSM_HEAD_DIM, SSM_STATE)))
        d_state = jnp.dot(jnp.concatenate(xw, axis=0), bm_g, preferred_element_type=F32)
        st_ref[rows, :] = st_g * jnp.concatenate(scale, axis=0) + d_state


def _ssd_scan_kernel(*refs, reverse, has_h0, finish, emit_state, n_steps):
    refs = list(refs)
    xT_ref, bm_ref, cm_ref, dt_ref, alog_ref = refs[:5]
    pos = 5
    h0_ref = None
    if has_h0:
        h0_ref = refs[pos]
        pos += 1
    if finish:
        (z_ref, ybT_ref, x_ref, mod_ref, g_ref, dexp_ref, ng_ref, wout_ref) = refs[pos:pos + 8]
        pos += 8
    y_out_ref = refs[pos]
    pos += 1
    state_out_ref = None
    if emit_state:
        state_out_ref = refs[pos]
        pos += 1
    st_ref = refs[pos]
    yT_ref = refs[pos + 1] if finish else y_out_ref

    c = pl.program_id(1)

    @pl.when(c == 0)
    def _():
        if has_h0:
            st_ref[...] = h0_ref[0]
        else:
            st_ref[...] = jnp.zeros_like(st_ref)

    offsets = [k * SCAN_CHUNK for k in range(SCAN_STEP // SCAN_CHUNK)]
    for off in (reversed(offsets) if reverse else offsets):
        _scan_chunk(xT_ref, bm_ref, cm_ref, dt_ref, alog_ref, st_ref, yT_ref, off, reverse)

    if emit_state:
        @pl.when(c == n_steps - 1)
        def _():
            state_out_ref[0] = st_ref[...]

    if finish:
        yT = yT_ref[...] + ybT_ref[...] + dexp_ref[...] * xT_ref[...]
        y = yT.T * _silu(z_ref[...])
        y = _rms(y, ng_ref[...])
        out = jnp.dot(y.astype(BF16), wout_ref[...], preferred_element_type=F32)
        _, _, gt = _mod_slices(mod_ref[0], 0)
        y_out_ref[...] = x_ref[...] + gt * _rms(out, g_ref[1:2])


def _ssd_scan(xT, bm, cm, dt, a_log, h0, fin, *, batch, reverse, emit_state):
    n = xT.shape[1]
    q = SCAN_STEP
    nc = n // batch // q
    gn = SSM_GROUPS * SSM_STATE
    finish = fin is not None

    def tok(b, c):
        return b * nc + ((nc - 1 - c) if reverse else c)

    def col_blk(b, c):
        return (0, tok(b, c))

    def row_blk(b, c):
        return (tok(b, c), 0)

    in_specs = [pl.BlockSpec((SSM_D_INNER, q), col_blk),
                pl.BlockSpec((q, gn), row_blk),
                pl.BlockSpec((q, gn), row_blk),
                pl.BlockSpec((q, LANES), row_blk),
                _const_spec(a_log.shape)]
    args = [xT, bm, cm, dt, a_log]
    if h0 is not None:
        in_specs.append(pl.BlockSpec((1, SSM_D_INNER, SSM_STATE), lambda b, c: (b, 0, 0)))
        args.append(h0)
    if finish:
        z, ybT, x, mod, g, dexp, ng, wout = fin
        n_mod = mod.shape[0]
        in_specs += [pl.BlockSpec((q, SSM_D_INNER), row_blk),
                     pl.BlockSpec((SSM_D_INNER, q), col_blk),
                     pl.BlockSpec((q, D_MODEL), row_blk),
                     pl.BlockSpec((1, 1, 6 * D_MODEL),
                                  (lambda b, c: (b, 0, 0)) if n_mod > 1 else (lambda b, c: (0, 0, 0))),
                     _const_spec(g.shape), _const_spec(dexp.shape), _const_spec(ng.shape),
                     _const_spec(wout.shape)]
        args += [z, ybT, x, mod, g, dexp, ng, wout]
        out_specs = [pl.BlockSpec((q, D_MODEL), row_blk)]
        out_shape = [jax.ShapeDtypeStruct((n, D_MODEL), F32)]
    else:
        out_specs = [pl.BlockSpec((SSM_D_INNER, q), col_blk)]
        out_shape = [jax.ShapeDtypeStruct((SSM_D_INNER, n), F32)]
    if emit_state:
        out_specs.append(pl.BlockSpec((1, SSM_D_INNER, SSM_STATE), lambda b, c: (b, 0, 0)))
        out_shape.append(jax.ShapeDtypeStruct((batch, SSM_D_INNER, SSM_STATE), F32))
    scratch = [pltpu.VMEM((SSM_D_INNER, SSM_STATE), F32)]
    if finish:
        scratch.append(pltpu.VMEM((SSM_D_INNER, q), F32))
    outs = pl.pallas_call(
        functools.partial(_ssd_scan_kernel, reverse=reverse, has_h0=h0 is not None, finish=finish,
                          emit_state=emit_state, n_steps=nc),
        grid=(batch, nc),
        in_specs=in_specs, out_specs=out_specs, out_shape=out_shape,
        scratch_shapes=scratch,
        compiler_params=_cparams(2),
        name="ssd_scan_bwd" if reverse else "ssd_scan_fwd",
    )(*args)
    return outs if emit_state else (outs[0], None)


def _ssd_layer(x, mod, g, p, h0, batch, seq_len):
    z, xT, bm, cm, dt = _ssd_in(x, mod, g, p["w_in"], p["conv_w"], p["conv_b"], p["dt_bias"], seq_len)
    emit = h0 is None
    h0f = None if h0 is None else h0[:, 0]
    h0b = None if h0 is None else h0[:, 1]
    ybT, s_b = _ssd_scan(xT, bm, cm, dt, p["a_log"], h0b, None, batch=batch, reverse=True,
                         emit_state=emit)
    fin = (z, ybT, x, mod, g, p["d_exp"], p["norm_g"], p["w_out"])
    x_new, s_f = _ssd_scan(xT, bm, cm, dt, p["a_log"], h0f, fin, batch=batch, reverse=False,
                           emit_state=emit)
    return x_new, (s_f, s_b)


def _qkv_kernel(*refs, rotary):
    if rotary:
        x_ref, mod_ref, g_ref, w_ref, cos_ref, sin_ref, q_ref, k_ref, v_ref = refs
    else:
        x_ref, mod_ref, g_ref, w_ref, q_ref, k_ref, v_ref = refs
    d = D_MODEL
    sh, sc, _ = _mod_slices(mod_ref[0], 0)
    h = (_rms(x_ref[...], g_ref[0:1]) * (1.0 + sc) + sh).astype(BF16)
    if rotary:
        lane = lax.broadcasted_iota(jnp.int32, (x_ref.shape[0], LANES), 1)
        first_half = (lane & (DIFF_HEAD_DIM - 1)) < DIFF_HEAD_DIM // 2
        cos = cos_ref[...]
        sin = sin_ref[...]

    def rot(a):
        if not rotary:
            return a
        partner = jnp.where(first_half, pltpu.roll(a, LANES - DIFF_HEAD_DIM // 2, 1),
                            pltpu.roll(a, DIFF_HEAD_DIM // 2, 1))
        return a * cos + partner * sin

    scale = DIFF_HEAD_DIM ** -0.5
    for j in range(d // LANES):
        cs = slice(j * LANES, (j + 1) * LANES)
        qj = jnp.dot(h, w_ref[:, j * LANES:(j + 1) * LANES], preferred_element_type=F32)
        kj = jnp.dot(h, w_ref[:, d + j * LANES:d + (j + 1) * LANES], preferred_element_type=F32)
        q_ref[:, cs] = (rot(qj) * scale).astype(q_ref.dtype)
        k_ref[:, cs] = rot(kj).astype(k_ref.dtype)
    v_ref[...] = jnp.dot(h, w_ref[:, 2 * d:3 * d], preferred_element_type=F32).astype(v_ref.dtype)


def _qkv(x, mod, g, w_qkv, rot_tables, seq_len, kv_dtype):
    n = x.shape[0]
    t = ROW_TILE
    rotary = rot_tables is not None
    in_specs = [pl.BlockSpec((t, D_MODEL), lambda i: (i, 0)), _mod_spec(mod.shape[0], t, seq_len),
                _const_spec(g.shape), _const_spec(w_qkv.shape)]
    args = [x, mod, g, w_qkv]
    if rotary:
        per_seq = seq_len // t
        in_specs += [pl.BlockSpec((t, LANES), lambda i: (i % per_seq, 0))] * 2
        args += list(rot_tables)
    row_spec = pl.BlockSpec((t, D_MODEL), lambda i: (i, 0))
    return pl.pallas_call(
        functools.partial(_qkv_kernel, rotary=rotary),
        grid=(n // t,),
        in_specs=in_specs,
        out_specs=[row_spec, row_spec, row_spec],
        out_shape=[jax.ShapeDtypeStruct((n, D_MODEL), BF16),
                   jax.ShapeDtypeStruct((n, D_MODEL), kv_dtype),
                   jax.ShapeDtypeStruct((n, D_MODEL), kv_dtype)],
        compiler_params=_cparams(1),
        name="diff_qkv",
    )(*args)


def _attn_kernel(*refs, has_cache, lam_init):
    if has_cache:
        (q_ref, k_ref, v_ref, ck_ref, cv_ref, x_ref, mod_ref, g_ref, lp_ref, sg_ref, wo_ref,
         o_ref, oall_ref) = refs
    else:
        q_ref, k_ref, v_ref, x_ref, mod_ref, g_ref, lp_ref, sg_ref, wo_ref, o_ref, oall_ref = refs
    lp = lp_ref[...]
    lam = (jnp.exp(jnp.sum(lp[0:1] * lp[1:2], axis=-1, keepdims=True))
           - jnp.exp(jnp.sum(lp[2:3] * lp[3:4], axis=-1, keepdims=True)) + lam_init)
    dn = (((1,), (1,)), ((), ()))
    hd = DIFF_HEAD_DIM

    def probs(hh):
        hs = slice(hh * hd, (hh + 1) * hd)
        qh = q_ref[:, hs]
        s_new = lax.dot_general(qh, k_ref[:, hs].astype(BF16), dn, preferred_element_type=F32)
        m = jnp.max(s_new, axis=-1, keepdims=True)
        if has_cache:
            s_old = lax.dot_general(qh, ck_ref[0, :, hs].astype(BF16), dn, preferred_element_type=F32)
            m = jnp.maximum(m, jnp.max(s_old, axis=-1, keepdims=True))
            p_old = jnp.exp(s_old - m)
        p_new = jnp.exp(s_new - m)
        l = jnp.sum(p_new, axis=-1, keepdims=True)
        if has_cache:
            l = l + jnp.sum(p_old, axis=-1, keepdims=True)
            return p_old, p_new, 1.0 / l
        return None, p_new, 1.0 / l

    for hp in range(DIFF_HEADS):
        po0, pn0, r0 = probs(2 * hp)
        po1, pn1, r1 = probs(2 * hp + 1)
        r1 = r1 * lam
        vs = slice(hp * 2 * hd, (hp + 1) * 2 * hd)
        att = (pn0 * r0 - pn1 * r1).astype(BF16)
        o = jnp.dot(att, v_ref[:, vs].astype(BF16), preferred_element_type=F32)
        if has_cache:
            att_old = (po0 * r0 - po1 * r1).astype(BF16)
            o = o + jnp.dot(att_old, cv_ref[0, :, vs].astype(BF16), preferred_element_type=F32)
        oall_ref[:, vs] = (_rms(o, sg_ref[...]) * (1.0 - lam_init)).astype(BF16)

    m_out = jnp.dot(oall_ref[...], wo_ref[...], preferred_element_type=F32)
    _, _, gt = _mod_slices(mod_ref[0], 0)
    o_ref[...] = x_ref[...] + gt * _rms(m_out, g_ref[1:2])


def _attention(q, k, v, cache, x, mod, g, lam_p, subln_g, w_o, *, batch, lam_init):
    n = x.shape[0]
    seq = n // batch
    tq = ROW_TILE
    nq = seq // tq
    has_cache = cache is not None
    kv_spec = pl.BlockSpec((seq, D_MODEL), lambda b, t: (b, 0))
    in_specs = [pl.BlockSpec((tq, D_MODEL), lambda b, t: (b * nq + t, 0)), kv_spec, kv_spec]
    args = [q, k, v]
    if has_cache:
        past = cache[0].shape[1]
        c_spec = pl.BlockSpec((1, past, D_MODEL), lambda b, t: (b, 0, 0))
        in_specs += [c_spec, c_spec]
        args += list(cache)
    n_mod = mod.shape[0]
    in_specs += [pl.BlockSpec((tq, D_MODEL), lambda b, t: (b * nq + t, 0)),
                 pl.BlockSpec((1, 1, 6 * D_MODEL),
                              (lambda b, t: (b, 0, 0)) if n_mod > 1 else (lambda b, t: (0, 0, 0))),
                 _const_spec(g.shape), _const_spec(lam_p.shape), _const_spec(subln_g.shape),
                 _const_spec(w_o.shape)]
    args += [x, mod, g, lam_p, subln_g, w_o]
    return pl.pallas_call(
        functools.partial(_attn_kernel, has_cache=has_cache, lam_init=lam_init),
        grid=(batch, nq),
        in_specs=in_specs,
        out_specs=pl.BlockSpec((tq, D_MODEL), lambda b, t: (b * nq + t, 0)),
        out_shape=jax.ShapeDtypeStruct((n, D_MODEL), F32),
        scratch_shapes=[pltpu.VMEM((tq, D_MODEL), BF16)],
        compiler_params=_cparams(2),
        name="diff_attention",
    )(*args)


def _rotary_tables(n_tokens):
    rows = n_tokens // GRID_W
    row = jnp.repeat(jnp.arange(rows, dtype=F32), GRID_W)
    col = jnp.tile(jnp.arange(GRID_W, dtype=F32), rows)
    inv = ROPE_THETA ** (-jnp.arange(ROT_PAIRS_PER_AXIS, dtype=F32) / ROT_PAIRS_PER_AXIS)
    ang = jnp.concatenate([row[:, None] * inv, col[:, None] * inv], axis=-1)
    cos, sin = jnp.cos(ang), jnp.sin(ang)
    reps = LANES // DIFF_HEAD_DIM
    return (jnp.tile(jnp.concatenate([cos, cos], axis=-1), (1, reps)),
            jnp.tile(jnp.concatenate([-sin, sin], axis=-1), (1, reps)))


def kernel(x_prompt, x_sample, state_ssm, cache_k, cache_v, c, c_ctx, w_mod, b_mod, norm_g, ssd_w_in, ssd_conv_w, ssd_conv_b, ssd_dt_bias, ssd_a_log, ssd_d, ssd_norm_g, ssd_w_out, sc_w_in, sc_conv_w, sc_w_out, da_w_qkv, da_lambda, da_subln_g, da_w_out, ffn_w_up, ffn_conv_w, ffn_w_down):
    bp, lp_, d = x_prompt.shape
    bs, ls, _ = x_sample.shape
    xp = x_prompt.reshape(bp * lp_, d)
    xs = x_sample.reshape(bs * ls, d)

    n_cond = 1 + bs
    cond = jnp.concatenate([c_ctx[None], c, jnp.zeros((HALO - n_cond, d), F32)], axis=0)
    mod_all = _modulation(cond, w_mod, b_mod)

    rot = _rotary_tables(ls)
    new_ssm, new_k, new_v = [], [], []
    for l in range(DEPTH):
        j = l // N_MIXERS
        kind = l % N_MIXERS
        mod_p = mod_all[l, 0:1][:, None]
        mod_s = mod_all[l, 1:n_cond][:, None]
        g = norm_g[l]
        if kind == 0:
            pad = LANES - 2 * SSM_HEADS
            p = {
                "w_in": jnp.pad(ssd_w_in[j], ((0, 0), (0, pad))).astype(BF16),
                "conv_w": ssd_conv_w[j],
                "conv_b": ssd_conv_b[j][None],
                "dt_bias": jnp.pad(ssd_dt_bias[j].reshape(1, -1), ((0, 0), (0, pad))),
                "a_log": jnp.pad(ssd_a_log[j].reshape(1, -1), ((0, 0), (0, pad))),
                "d_exp": jnp.broadcast_to(jnp.repeat(ssd_d[j], SSM_HEAD_DIM)[:, None],
                                          (SSM_D_INNER, SCAN_STEP)),
                "norm_g": ssd_norm_g[j][None],
                "w_out": ssd_w_out[j].astype(BF16),
            }
            h0 = state_ssm[:, j].reshape(bs, 2, SSM_D_INNER, SSM_STATE)
            xp, (s_f, s_b) = _ssd_layer(xp, mod_p, g, p, None, bp, lp_)
            xs, _ = _ssd_layer(xs, mod_s, g, p, h0, bs, ls)
            new_ssm.append(jnp.stack([s_f, s_b], axis=1).reshape(
                bp, 2, SSM_HEADS, SSM_HEAD_DIM, SSM_STATE))
        elif kind == 1:
            w_in, w_out = sc_w_in[j].astype(BF16), sc_w_out[j].astype(BF16)
            xp = _sconv(xp, mod_p, g, w_in, sc_conv_w[j], w_out, lp_)
            xs = _sconv(xs, mod_s, g, w_in, sc_conv_w[j], w_out, ls)
        else:
            lam_init = 0.8 - 0.6 * math.exp(-0.3 * l)
            w_qkv, w_o = da_w_qkv[j].astype(BF16), da_w_out[j].astype(BF16)
            sg = da_subln_g[j][None]
            qp, kp, vp = _qkv(xp, mod_p, g, w_qkv, None, lp_, F32)
            xp = _attention(qp, kp, vp, None, xp, mod_p, g, da_lambda[j], sg, w_o,
                            batch=bp, lam_init=lam_init)
            new_k.append(kp.reshape(bp, lp_, 2 * DIFF_HEADS, DIFF_HEAD_DIM))
            new_v.append(vp.reshape(bp, lp_, DIFF_HEADS, 2 * DIFF_HEAD_DIM))
            qs, ks, vs = _qkv(xs, mod_s, g, w_qkv, rot, ls, BF16)
            cache = (cache_k[:, j].reshape(bs, -1, d), cache_v[:, j].reshape(bs, -1, d))
            xs = _attention(qs, ks, vs, cache, xs, mod_s, g, da_lambda[j], sg, w_o,
                            batch=bs, lam_init=lam_init)
        w_up, w_dn = ffn_w_up[l].astype(BF16), ffn_w_down[l].astype(BF16)
        xp = _ffn(xp, mod_p, g, w_up, ffn_conv_w[l], w_dn, lp_)
        xs = _ffn(xs, mod_s, g, w_up, ffn_conv_w[l], w_dn, ls)

    return (xp.reshape(bp, lp_, d), xs.reshape(bs, ls, d), jnp.stack(new_ssm, axis=1),
            jnp.stack(new_k, axis=1), jnp.stack(new_v, axis=1))
```

```python
import functools
import math

import jax
import jax.numpy as jnp
from jax import lax
from jax.experimental import pallas as pl
from jax.experimental.pallas import tpu as pltpu

D_MODEL = 1024
DEPTH = 4
GRID_W = 64
N_MIXERS = 3
SSM_D_INNER = 2 * D_MODEL
SSM_HEAD_DIM = 64
SSM_HEADS = SSM_D_INNER // SSM_HEAD_DIM
SSM_GROUPS = 8
SSM_STATE = 128
SSM_CONV_DIM = SSM_D_INNER + 2 * SSM_GROUPS * SSM_STATE
HEADS_PER_GROUP = SSM_HEADS // SSM_GROUPS
GROUP_ROWS = HEADS_PER_GROUP * SSM_HEAD_DIM
DIFF_HEAD_DIM = 64
DIFF_HEADS = D_MODEL // (2 * DIFF_HEAD_DIM)
ROPE_THETA = 10000.0
ROT_PAIRS_PER_AXIS = DIFF_HEAD_DIM // 4
FFN_DIM = 2816
NORM_EPS = 1e-6

LANES = 128
HALO = 8
SCAN_CHUNK = 128
SCAN_STEP = 256
ROW_TILE = 256
FFN_ROW_TILE = 512
COL_CHUNK = 256
FFN_DOWN_GROUP = 4
VMEM_LIMIT = 56 * 1024 * 1024

F32 = jnp.float32
BF16 = jnp.bfloat16


def _cparams(n_axes):
    return pltpu.CompilerParams(dimension_semantics=("arbitrary",) * n_axes,
                                vmem_limit_bytes=VMEM_LIMIT)


def _rms(x, g):
    ms = jnp.mean(x * x, axis=-1, keepdims=True)
    return x * lax.rsqrt(ms + NORM_EPS) * g


def _silu(x):
    return x * jax.nn.sigmoid(x)


def _mod_slices(mod, first):
    d = D_MODEL
    return (mod[:, first * d:(first + 1) * d], mod[:, (first + 1) * d:(first + 2) * d],
            mod[:, (first + 2) * d:(first + 3) * d])


def _ext_norm(xp_ref, x_ref, xn_ref, g, sc, sh, seq_len):
    tiles_per_seq = seq_len // x_ref.shape[0]
    k = pl.program_id(0) % tiles_per_seq
    keep = (jnp.where(k == 0, 0.0, 1.0), None, jnp.where(k == tiles_per_seq - 1, 0.0, 1.0))
    pieces = []
    for r, kp in zip((xp_ref, x_ref, xn_ref), keep):
        h = _rms(r[...], g) * (1.0 + sc) + sh
        pieces.append(h if kp is None else h * kp)
    return jnp.concatenate(pieces, axis=0).astype(BF16)


def _conv3(u_ext, w3, tile_rows):
    n_ext = tile_rows + 2 * HALO
    up = pltpu.roll(u_ext, 1, 0)[HALO:HALO + tile_rows]
    un = pltpu.roll(u_ext, n_ext - 1, 0)[HALO:HALO + tile_rows]
    um = u_ext[HALO:HALO + tile_rows]
    return up * w3[0:1] + um * w3[1:2] + un * w3[2:3]


def _halo_specs(n_rows, tile_rows, width):
    per = tile_rows // HALO
    last = n_rows // HALO - 1
    return [
        pl.BlockSpec((HALO, width), lambda i: (jnp.maximum(i * per - 1, 0), 0)),
        pl.BlockSpec((tile_rows, width), lambda i: (i, 0)),
        pl.BlockSpec((HALO, width), lambda i: (jnp.minimum((i + 1) * per, last), 0)),
    ]


def _mod_spec(n_mod, tile_rows, seq_len):
    if n_mod == 1:
        return pl.BlockSpec((1, 1, 6 * D_MODEL), lambda i: (0, 0, 0))
    return pl.BlockSpec((1, 1, 6 * D_MODEL), lambda i: ((i * tile_rows) // seq_len, 0, 0))


def _const_spec(shape):
    nd = len(shape)
    return pl.BlockSpec(shape, lambda *_: (0,) * nd)


def _weight_spec(shape):
    nd = len(shape)
    return pl.BlockSpec(shape, lambda *_: (0,) * nd, pipeline_mode=pl.Buffered(1))


def _mod_kernel(cond_ref, w_ref, b_ref, o_ref):
    a = _silu(cond_ref[...]).astype(BF16)
    o_ref[0] = jnp.dot(a, w_ref[0].astype(BF16), preferred_element_type=F32) + b_ref[0]


def _modulation(cond, w_mod, b_mod):
    n_cols = 6 * D_MODEL
    blk = n_cols // 4
    return pl.pallas_call(
        _mod_kernel,
        grid=(DEPTH, n_cols // blk),
        in_specs=[_const_spec(cond.shape),
                  pl.BlockSpec((1, D_MODEL, blk), lambda l, j: (l, 0, j)),
                  pl.BlockSpec((1, 1, blk), lambda l, j: (l, 0, j))],
        out_specs=pl.BlockSpec((1, cond.shape[0], blk), lambda l, j: (l, 0, j)),
        out_shape=jax.ShapeDtypeStruct((DEPTH, cond.shape[0], n_cols), F32),
        compiler_params=_cparams(2),
        name="modulation",
    )(cond, w_mod, b_mod.reshape(DEPTH, 1, n_cols))


def _ffn_kernel(xp_ref, x_ref, xn_ref, mod_ref, g_ref, wup_ref, cw_ref, wdn_ref, o_ref,
                hext_ref, act_ref, *, seq_len):
    t = x_ref.shape[0]
    sh, sc, gt = _mod_slices(mod_ref[0], 3)
    hext_ref[...] = _ext_norm(xp_ref, x_ref, xn_ref, g_ref[2:3], sc, sh, seq_len)
    n_chunks = FFN_DIM // COL_CHUNK
    acc = None
    for j0 in range(0, n_chunks, FFN_DOWN_GROUP):
        n_j = min(FFN_DOWN_GROUP, n_chunks - j0)
        for jj in range(n_j):
            cg = (j0 + jj) * COL_CHUNK
            cv = FFN_DIM + cg
            h = hext_ref[...]
            ug = jnp.dot(h, wup_ref[:, cg:cg + COL_CHUNK], preferred_element_type=F32)
            uv = jnp.dot(h, wup_ref[:, cv:cv + COL_CHUNK], preferred_element_type=F32)
            gate = _conv3(ug, cw_ref[:, cg:cg + COL_CHUNK], t)
            val = _conv3(uv, cw_ref[:, cv:cv + COL_CHUNK], t)
            act_ref[:, jj * COL_CHUNK:(jj + 1) * COL_CHUNK] = (_silu(gate) * val).astype(BF16)
        k0, kn = j0 * COL_CHUNK, n_j * COL_CHUNK
        part = jnp.dot(act_ref[:, :kn], wdn_ref[k0:k0 + kn, :], preferred_element_type=F32)
        acc = part if acc is None else acc + part
    o_ref[...] = x_ref[...] + gt * _rms(acc, g_ref[3:4])


def _ffn(x, mod, g, w_up, conv_w, w_down, seq_len):
    n = x.shape[0]
    t = min(FFN_ROW_TILE, seq_len)
    return pl.pallas_call(
        functools.partial(_ffn_kernel, seq_len=seq_len),
        grid=(n // t,),
        in_specs=_halo_specs(n, t, D_MODEL) + [
            _mod_spec(mod.shape[0], t, seq_len), _const_spec(g.shape), _weight_spec(w_up.shape),
            _const_spec(conv_w.shape), _weight_spec(w_down.shape)],
        out_specs=pl.BlockSpec((t, D_MODEL), lambda i: (i, 0)),
        out_shape=jax.ShapeDtypeStruct((n, D_MODEL), F32),
        scratch_shapes=[pltpu.VMEM((t + 2 * HALO, D_MODEL), BF16),
                        pltpu.VMEM((t, FFN_DOWN_GROUP * COL_CHUNK), BF16)],
        compiler_params=_cparams(1),
        name="conv_ffn",
    )(x, x, x, mod, g, w_up, conv_w, w_down)


def _sconv_kernel(xp_ref, x_ref, xn_ref, mod_ref, g_ref, win_ref, cw_ref, wout_ref, o_ref,
                  hext_ref, act_ref, *, seq_len):
    t = x_ref.shape[0]
    d = D_MODEL
    sh, sc, gt = _mod_slices(mod_ref[0], 0)
    hext_ref[...] = _ext_norm(xp_ref, x_ref, xn_ref, g_ref[0:1], sc, sh, seq_len)
    for j in range(d // COL_CHUNK):
        c0 = j * COL_CHUNK
        bg = jnp.dot(hext_ref[HALO:HALO + t, :], win_ref[:, c0:c0 + COL_CHUNK],
                     preferred_element_type=F32)
        h = hext_ref[...]
        cg = jnp.dot(h, win_ref[:, d + c0:d + c0 + COL_CHUNK], preferred_element_type=F32)
        u = jnp.dot(h, win_ref[:, 2 * d + c0:2 * d + c0 + COL_CHUNK], preferred_element_type=F32)
        conv = _conv3(cg * u, cw_ref[:, c0:c0 + COL_CHUNK], t)
        act_ref[:, c0:c0 + COL_CHUNK] = (bg * conv).astype(BF16)
    m = jnp.dot(act_ref[...], wout_ref[...], preferred_element_type=F32)
    o_ref[...] = x_ref[...] + gt * _rms(m, g_ref[1:2])


def _sconv(x, mod, g, w_in, conv_w, w_out, seq_len):
    n = x.shape[0]
    t = ROW_TILE
    return pl.pallas_call(
        functools.partial(_sconv_kernel, seq_len=seq_len),
        grid=(n // t,),
        in_specs=_halo_specs(n, t, D_MODEL) + [
            _mod_spec(mod.shape[0], t, seq_len), _const_spec(g.shape), _const_spec(w_in.shape),
            _const_spec(conv_w.shape), _const_spec(w_out.shape)],
        out_specs=pl.BlockSpec((t, D_MODEL), lambda i: (i, 0)),
        out_shape=jax.ShapeDtypeStruct((n, D_MODEL), F32),
        scratch_shapes=[pltpu.VMEM((t + 2 * HALO, D_MODEL), BF16), pltpu.VMEM((t, D_MODEL), BF16)],
        compiler_params=_cparams(1),
        name="short_conv_mixer",
    )(x, x, x, mod, g, w_in, conv_w, w_out)


def _ssd_in_kernel(xp_ref, x_ref, xn_ref, mod_ref, g_ref, win_ref, cw_ref, cb_ref, dtb_ref,
                   z_ref, xT_ref, bm_ref, cm_ref, dt_ref, hext_ref, *, seq_len):
    t = x_ref.shape[0]
    di = SSM_D_INNER
    gn = SSM_GROUPS * SSM_STATE
    sh, sc, _ = _mod_slices(mod_ref[0], 0)
    hext_ref[...] = _ext_norm(xp_ref, x_ref, xn_ref, g_ref[0:1], sc, sh, seq_len)

    for j in range(di // COL_CHUNK):
        c0 = j * COL_CHUNK
        z_ref[:, c0:c0 + COL_CHUNK] = jnp.dot(hext_ref[HALO:HALO + t, :], win_ref[:, c0:c0 + COL_CHUNK],
                                              preferred_element_type=F32)

    for j in range(SSM_CONV_DIM // COL_CHUNK):
        c0 = j * COL_CHUNK
        u = jnp.dot(hext_ref[...], win_ref[:, di + c0:di + c0 + COL_CHUNK], preferred_element_type=F32)
        act = _silu(_conv3(u, cw_ref[:, c0:c0 + COL_CHUNK], t) + cb_ref[:, c0:c0 + COL_CHUNK])
        if c0 < di:
            xT_ref[c0:c0 + COL_CHUNK, :] = act.T
        elif c0 < di + gn:
            bm_ref[:, c0 - di:c0 - di + COL_CHUNK] = act.astype(BF16)
        else:
            cm_ref[:, c0 - di - gn:c0 - di - gn + COL_CHUNK] = act.astype(BF16)

    raw = jnp.dot(hext_ref[HALO:HALO + t, :], win_ref[:, di + SSM_CONV_DIM:di + SSM_CONV_DIM + LANES],
                  preferred_element_type=F32) + dtb_ref[...]
    dt_ref[...] = jnp.maximum(raw, 0.0) + jnp.log1p(jnp.exp(-jnp.abs(raw)))


def _ssd_in(x, mod, g, w_in, conv_w, conv_b, dt_bias, seq_len):
    n = x.shape[0]
    t = ROW_TILE
    gn = SSM_GROUPS * SSM_STATE
    return pl.pallas_call(
        functools.partial(_ssd_in_kernel, seq_len=seq_len),
        grid=(n // t,),
        in_specs=_halo_specs(n, t, D_MODEL) + [
            _mod_spec(mod.shape[0], t, seq_len), _const_spec(g.shape), _const_spec(w_in.shape),
            _const_spec(conv_w.shape), _const_spec(conv_b.shape), _const_spec(dt_bias.shape)],
        out_specs=[pl.BlockSpec((t, SSM_D_INNER), lambda i: (i, 0)),
                   pl.BlockSpec((SSM_D_INNER, t), lambda i: (0, i)),
                   pl.BlockSpec((t, gn), lambda i: (i, 0)),
                   pl.BlockSpec((t, gn), lambda i: (i, 0)),
                   pl.BlockSpec((t, LANES), lambda i: (i, 0))],
        out_shape=[jax.ShapeDtypeStruct((n, SSM_D_INNER), F32),
                   jax.ShapeDtypeStruct((SSM_D_INNER, n), F32),
                   jax.ShapeDtypeStruct((n, gn), BF16),
                   jax.ShapeDtypeStruct((n, gn), BF16),
                   jax.ShapeDtypeStruct((n, LANES), F32)],
        scratch_shapes=[pltpu.VMEM((t + 2 * HALO, D_MODEL), BF16)],
        compiler_params=_cparams(1),
        name="ssd_in_proj",
    )(x, x, x, mod, g, w_in, conv_w, conv_b, dt_bias)


def _split3(v):
    hi = v.astype(BF16)
    r1 = v - hi.astype(F32)
    mid = r1.astype(BF16)
    lo = (r1 - mid.astype(F32)).astype(BF16)
    return hi, mid, lo


_CONTRACT_LAST = (((1,), (1,)), ((), ()))


def _scan_chunk(xT_ref, bm_ref, cm_ref, dt_ref, alog_ref, st_ref, yT_ref, off, reverse):
    q = SCAN_CHUNK
    tok = slice(off, off + q)
    dt = dt_ref[tok, :]
    da = dt * (-jnp.exp(alog_ref[...]))
    r_i = lax.broadcasted_iota(jnp.int32, (q, q), 0)
    c_i = lax.broadcasted_iota(jnp.int32, (q, q), 1)
    mask = (c_i >= r_i) if reverse else (c_i <= r_i)
    tri = jnp.where(mask, 1.0, 0.0).astype(BF16)
    acs = sum(jnp.dot(tri, part, preferred_element_type=F32) for part in _split3(da))
    acsT = acs.T
    dtT = dt.T
    srcT = acsT - jnp.log(dtT)
    last = 0 if reverse else q - 1
    totT = acsT[:, last:last + 1]
    wT = jnp.exp(totT - acsT) * dtT
    eaT = jnp.exp(acsT)
    etot = jnp.broadcast_to(jnp.exp(totT), (LANES, SSM_STATE))
    head0 = SSM_HEADS if reverse else 0

    for g in range(SSM_GROUPS):
        ns = slice(g * SSM_STATE, (g + 1) * SSM_STATE)
        bm_g = bm_ref[tok, ns]
        cm_g = cm_ref[tok, ns]
        cb = lax.dot_general(cm_g, bm_g, _CONTRACT_LAST, preferred_element_type=F32)
        rows = slice(g * GROUP_ROWS, (g + 1) * GROUP_ROWS)
        st_g = st_ref[rows, :]
        y_off = lax.dot_general(st_g.astype(BF16), cm_g, _CONTRACT_LAST, preferred_element_type=F32)
        xT_g = xT_ref[rows, tok]
        xw, scale = [], []
        for r in range(HEADS_PER_GROUP):
            col = head0 + g * HEADS_PER_GROUP + r
            hr = slice(r * SSM_HEAD_DIM, (r + 1) * SSM_HEAD_DIM)
            m = jnp.where(mask, cb * jnp.exp(acs[:, col:col + 1] - srcT[col:col + 1, :]), 0.0)
            xh = xT_g[hr, :]
            yh = lax.dot_general(xh.astype(BF16), m.astype(BF16), _CONTRACT_LAST,
                                 preferred_element_type=F32)
            yh = yh + y_off[hr, :] * eaT[col:col + 1, :]
            yT_ref[g * GROUP_ROWS + r * SSM_HEAD_DIM:g * GROUP_ROWS + (r + 1) * SSM_HEAD_DIM, tok] = yh
            xw.append((xh * wT[col:col + 1, :]).astype(BF16))
            scale.append(jnp.broadcast_to(etot[col:col + 1, :], (SSM_HEAD_DIM, SSM_STATE)))
        d_state = jnp.dot(jnp.concatenate(xw, axis=0), bm_g, preferred_element_type=F32)
        st_ref[rows, :] = st_g * jnp.concatenate(scale, axis=0) + d_state


def _ssd_scan_kernel(*refs, reverse, has_h0, finish, emit_state, state_aliased, n_steps):
    refs = list(refs)
    xT_ref, bm_ref, cm_ref, dt_ref, alog_ref = refs[:5]
    pos = 5
    h0_ref = None
    if has_h0:
        h0_ref = refs[pos]
        pos += 1
    if finish:
        (z_ref, ybT_ref, x_ref, mod_ref, g_ref, dexp_ref, ng_ref, wout_ref) = refs[pos:pos + 8]
        pos += 8
    if state_aliased:
        pos += 1
    y_out_ref = refs[pos]
    pos += 1
    state_out_ref = None
    if emit_state:
        state_out_ref = refs[pos]
        pos += 1
    st_ref = refs[pos]
    yT_ref = refs[pos + 1] if finish else y_out_ref

    c = pl.program_id(1)

    @pl.when(c == 0)
    def _():
        if has_h0:
            st_ref[...] = h0_ref[0, 0]
        else:
            st_ref[...] = jnp.zeros_like(st_ref)

    offsets = [k * SCAN_CHUNK for k in range(SCAN_STEP // SCAN_CHUNK)]
    for off in (reversed(offsets) if reverse else offsets):
        _scan_chunk(xT_ref, bm_ref, cm_ref, dt_ref, alog_ref, st_ref, yT_ref, off, reverse)

    if emit_state:
        @pl.when(c == n_steps - 1)
        def _():
            state_out_ref[0, 0, 0] = st_ref[...]

    if finish:
        yT = yT_ref[...] + ybT_ref[...] + dexp_ref[...] * xT_ref[...]
        y = yT.T * _silu(z_ref[...])
        y = _rms(y, ng_ref[...])
        out = jnp.dot(y.astype(BF16), wout_ref[...], preferred_element_type=F32)
        _, _, gt = _mod_slices(mod_ref[0], 0)
        y_out_ref[...] = x_ref[...] + gt * _rms(out, g_ref[1:2])


def _ssd_scan(xT, bm, cm, dt, a_log, h0, fin, state, *, batch, reverse):
    n = xT.shape[1]
    direction = 1 if reverse else 0
    emit_state = state is not None
    q = SCAN_STEP
    nc = n // batch // q
    gn = SSM_GROUPS * SSM_STATE
    finish = fin is not None

    def tok(b, c):
        return b * nc + ((nc - 1 - c) if reverse else c)

    def col_blk(b, c):
        return (0, tok(b, c))

    def row_blk(b, c):
        return (tok(b, c), 0)

    in_specs = [pl.BlockSpec((SSM_D_INNER, q), col_blk),
                pl.BlockSpec((q, gn), row_blk),
                pl.BlockSpec((q, gn), row_blk),
                pl.BlockSpec((q, LANES), row_blk),
                _const_spec(a_log.shape)]
    args = [xT, bm, cm, dt, a_log]
    if h0 is not None:
        in_specs.append(pl.BlockSpec((1, 1, SSM_D_INNER, SSM_STATE), lambda b, c: (b, direction, 0, 0)))
        args.append(h0)
    if finish:
        z, ybT, x, mod, g, dexp, ng, wout = fin
        n_mod = mod.shape[0]
        in_specs += [pl.BlockSpec((q, SSM_D_INNER), row_blk),
                     pl.BlockSpec((SSM_D_INNER, q), col_blk),
                     pl.BlockSpec((q, D_MODEL), row_blk),
                     pl.BlockSpec((1, 1, 6 * D_MODEL),
                                  (lambda b, c: (b, 0, 0)) if n_mod > 1 else (lambda b, c: (0, 0, 0))),
                     _const_spec(g.shape), _const_spec(dexp.shape), _const_spec(ng.shape),
                     _const_spec(wout.shape)]
        args += [z, ybT, x, mod, g, dexp, ng, wout]
        out_specs = [pl.BlockSpec((q, D_MODEL), row_blk)]
        out_shape = [jax.ShapeDtypeStruct((n, D_MODEL), F32)]
    else:
        out_specs = [pl.BlockSpec((SSM_D_INNER, q), col_blk)]
        out_shape = [jax.ShapeDtypeStruct((SSM_D_INNER, n), F32)]
    aliases = {}
    if emit_state:
        buf, n_layers, layer = state
        out_specs.append(pl.BlockSpec((1, 1, 1, SSM_D_INNER, SSM_STATE),
                                      lambda b, c: (b, layer, direction, 0, 0)))
        out_shape.append(jax.ShapeDtypeStruct((batch, n_layers, 2, SSM_D_INNER, SSM_STATE), F32))
        if buf is not None:
            in_specs.append(pl.BlockSpec(memory_space=pl.ANY))
            args.append(buf)
            aliases = {len(args) - 1: 1}
    scratch = [pltpu.VMEM((SSM_D_INNER, SSM_STATE), F32)]
    if finish:
        scratch.append(pltpu.VMEM((SSM_D_INNER, q), F32))
    outs = pl.pallas_call(
        functools.partial(_ssd_scan_kernel, reverse=reverse, has_h0=h0 is not None, finish=finish,
                          emit_state=emit_state, state_aliased=bool(aliases), n_steps=nc),
        grid=(batch, nc),
        in_specs=in_specs, out_specs=out_specs, out_shape=out_shape,
        scratch_shapes=scratch,
        input_output_aliases=aliases,
        compiler_params=_cparams(2),
        name="ssd_scan_bwd" if reverse else "ssd_scan_fwd",
    )(*args)
    return outs if emit_state else (outs[0], None)


def _ssd_layer(x, mod, g, p, h0, state, batch, seq_len):
    z, xT, bm, cm, dt = _ssd_in(x, mod, g, p["w_in"], p["conv_w"], p["conv_b"], p["dt_bias"], seq_len)
    ybT, buf = _ssd_scan(xT, bm, cm, dt, p["a_log"], h0, None, state, batch=batch, reverse=True)
    if state is not None:
        state = (buf,) + tuple(state[1:])
    fin = (z, ybT, x, mod, g, p["d_exp"], p["norm_g"], p["w_out"])
    x_new, buf = _ssd_scan(xT, bm, cm, dt, p["a_log"], h0, fin, state, batch=batch, reverse=False)
    return x_new, buf


def _qkv_kernel(*refs, rotary):
    if rotary:
        x_ref, mod_ref, g_ref, w_ref, cos_ref, sin_ref, q_ref, k_ref, v_ref = refs
    else:
        x_ref, mod_ref, g_ref, w_ref, q_ref, k_ref, v_ref = refs
    d = D_MODEL
    sh, sc, _ = _mod_slices(mod_ref[0], 0)
    h = (_rms(x_ref[...], g_ref[0:1]) * (1.0 + sc) + sh).astype(BF16)
    if rotary:
        lane = lax.broadcasted_iota(jnp.int32, (x_ref.shape[0], LANES), 1)
        first_half = (lane & (DIFF_HEAD_DIM - 1)) < DIFF_HEAD_DIM // 2
        cos = cos_ref[...]
        sin = sin_ref[...]

    def rot(a):
        if not rotary:
            return a
        partner = jnp.where(first_half, pltpu.roll(a, LANES - DIFF_HEAD_DIM // 2, 1),
                            pltpu.roll(a, DIFF_HEAD_DIM // 2, 1))
        return a * cos + partner * sin

    scale = DIFF_HEAD_DIM ** -0.5 * math.log2(math.e)
    for j in range(d // LANES):
        cs = slice(j * LANES, (j + 1) * LANES)
        qj = jnp.dot(h, w_ref[:, j * LANES:(j + 1) * LANES], preferred_element_type=F32)
        kj = jnp.dot(h, w_ref[:, d + j * LANES:d + (j + 1) * LANES], preferred_element_type=F32)
        q_ref[:, cs] = (rot(qj) * scale).astype(q_ref.dtype)
        k_ref[:, cs] = rot(kj).astype(k_ref.dtype)
    v_ref[...] = jnp.dot(h, w_ref[:, 2 * d:3 * d], preferred_element_type=F32).astype(v_ref.dtype)


def _qkv(x, mod, g, w_qkv, rot_tables, seq_len, kv_dtype):
    n = x.shape[0]
    t = ROW_TILE
    rotary = rot_tables is not None
    in_specs = [pl.BlockSpec((t, D_MODEL), lambda i: (i, 0)), _mod_spec(mod.shape[0], t, seq_len),
                _const_spec(g.shape), _const_spec(w_qkv.shape)]
    args = [x, mod, g, w_qkv]
    if rotary:
        per_seq = seq_len // t
        in_specs += [pl.BlockSpec((t, LANES), lambda i: (i % per_seq, 0))] * 2
        args += list(rot_tables)
    row_spec = pl.BlockSpec((t, D_MODEL), lambda i: (i, 0))
    return pl.pallas_call(
        functools.partial(_qkv_kernel, rotary=rotary),
        grid=(n // t,),
        in_specs=in_specs,
        out_specs=[row_spec, row_spec, row_spec],
        out_shape=[jax.ShapeDtypeStruct((n, D_MODEL), BF16),
                   jax.ShapeDtypeStruct((n, D_MODEL), kv_dtype),
                   jax.ShapeDtypeStruct((n, D_MODEL), kv_dtype)],
        compiler_params=_cparams(1),
        name="diff_qkv",
    )(*args)


def _attn_kernel(*refs, has_cache, lam_init):
    if has_cache:
        (q_ref, k_ref, v_ref, ck_ref, cv_ref, x_ref, mod_ref, g_ref, lp_ref, sg_ref, wo_ref,
         o_ref, oall_ref) = refs
    else:
        q_ref, k_ref, v_ref, x_ref, mod_ref, g_ref, lp_ref, sg_ref, wo_ref, o_ref, oall_ref = refs
    lp = lp_ref[...]
    lam = (jnp.exp(jnp.sum(lp[0:1] * lp[1:2], axis=-1, keepdims=True))
           - jnp.exp(jnp.sum(lp[2:3] * lp[3:4], axis=-1, keepdims=True)) + lam_init)
    dn = (((1,), (1,)), ((), ()))
    hd = DIFF_HEAD_DIM

    def probs(hh):
        hs = slice(hh * hd, (hh + 1) * hd)
        qh = q_ref[:, hs]
        s_new = lax.dot_general(qh, k_ref[:, hs].astype(BF16), dn, preferred_element_type=F32)
        m = jnp.max(s_new, axis=-1, keepdims=True)
        if has_cache:
            s_old = lax.dot_general(qh, ck_ref[0, :, hs].astype(BF16), dn, preferred_element_type=F32)
            m = jnp.maximum(m, jnp.max(s_old, axis=-1, keepdims=True))
            p_old = jnp.exp2(s_old - m)
        p_new = jnp.exp2(s_new - m)
        l = jnp.sum(p_new, axis=-1, keepdims=True)
        if has_cache:
            l = l + jnp.sum(p_old, axis=-1, keepdims=True)
            return p_old, p_new, l
        return None, p_new, l

    for hp in range(DIFF_HEADS):
        po0, pn0, l0 = probs(2 * hp)
        po1, pn1, l1 = probs(2 * hp + 1)
        ratio = lam * l0 / l1
        vs = slice(hp * 2 * hd, (hp + 1) * 2 * hd)
        att = (pn0 - pn1 * ratio).astype(BF16)
        o = jnp.dot(att, v_ref[:, vs].astype(BF16), preferred_element_type=F32)
        if has_cache:
            att_old = (po0 - po1 * ratio).astype(BF16)
            o = o + jnp.dot(att_old, cv_ref[0, :, vs].astype(BF16), preferred_element_type=F32)
        o = o * (1.0 / l0)
        oall_ref[:, vs] = (_rms(o, sg_ref[...]) * (1.0 - lam_init)).astype(BF16)

    m_out = jnp.dot(oall_ref[...], wo_ref[...], preferred_element_type=F32)
    _, _, gt = _mod_slices(mod_ref[0], 0)
    o_ref[...] = x_ref[...] + gt * _rms(m_out, g_ref[1:2])


def _attention(q, k, v, cache, x, mod, g, lam_p, subln_g, w_o, *, batch, lam_init):
    n = x.shape[0]
    seq = n // batch
    tq = ROW_TILE
    nq = seq // tq
    has_cache = cache is not None
    kv_spec = pl.BlockSpec((seq, D_MODEL), lambda b, t: (b, 0))
    in_specs = [pl.BlockSpec((tq, D_MODEL), lambda b, t: (b * nq + t, 0)), kv_spec, kv_spec]
    args = [q, k, v]
    if has_cache:
        past = cache[0].shape[1]
        c_spec = pl.BlockSpec((1, past, D_MODEL), lambda b, t: (b, 0, 0))
        in_specs += [c_spec, c_spec]
        args += list(cache)
    n_mod = mod.shape[0]
    in_specs += [pl.BlockSpec((tq, D_MODEL), lambda b, t: (b * nq + t, 0)),
                 pl.BlockSpec((1, 1, 6 * D_MODEL),
                              (lambda b, t: (b, 0, 0)) if n_mod > 1 else (lambda b, t: (0, 0, 0))),
                 _const_spec(g.shape), _const_spec(lam_p.shape), _const_spec(subln_g.shape),
                 _const_spec(w_o.shape)]
    args += [x, mod, g, lam_p, subln_g, w_o]
    return pl.pallas_call(
        functools.partial(_attn_kernel, has_cache=has_cache, lam_init=lam_init),
        grid=(batch, nq),
        in_specs=in_specs,
        out_specs=pl.BlockSpec((tq, D_MODEL), lambda b, t: (b * nq + t, 0)),
        out_shape=jax.ShapeDtypeStruct((n, D_MODEL), F32),
        scratch_shapes=[pltpu.VMEM((tq, D_MODEL), BF16)],
        compiler_params=_cparams(2),
        name="diff_attention",
    )(*args)


def _rotary_tables(n_tokens):
    rows = n_tokens // GRID_W
    row = jnp.repeat(jnp.arange(rows, dtype=F32), GRID_W)
    col = jnp.tile(jnp.arange(GRID_W, dtype=F32), rows)
    inv = ROPE_THETA ** (-jnp.arange(ROT_PAIRS_PER_AXIS, dtype=F32) / ROT_PAIRS_PER_AXIS)
    ang = jnp.concatenate([row[:, None] * inv, col[:, None] * inv], axis=-1)
    cos, sin = jnp.cos(ang), jnp.sin(ang)
    reps = LANES // DIFF_HEAD_DIM
    return (jnp.tile(jnp.concatenate([cos, cos], axis=-1), (1, reps)),
            jnp.tile(jnp.concatenate([-sin, sin], axis=-1), (1, reps)))


def kernel(x_prompt, x_sample, state_ssm, cache_k, cache_v, c, c_ctx, w_mod, b_mod, norm_g, ssd_w_in, ssd_conv_w, ssd_conv_b, ssd_dt_bias, ssd_a_log, ssd_d, ssd_norm_g, ssd_w_out, sc_w_in, sc_conv_w, sc_w_out, da_w_qkv, da_lambda, da_subln_g, da_w_out, ffn_w_up, ffn_conv_w, ffn_w_down):
    bp, lp_, d = x_prompt.shape
    bs, ls, _ = x_sample.shape
    xp = x_prompt.reshape(bp * lp_, d)
    xs = x_sample.reshape(bs * ls, d)

    n_cond = 1 + bs
    cond = jnp.concatenate([c_ctx[None], c, jnp.zeros((HALO - n_cond, d), F32)], axis=0)
    mod_all = _modulation(cond, w_mod, b_mod)

    rot = _rotary_tables(ls)
    n_ssd_layers = (DEPTH + 2) // N_MIXERS
    ssm_buf = None
    new_k, new_v = [], []
    for l in range(DEPTH):
        j = l // N_MIXERS
        kind = l % N_MIXERS
        mod_p = mod_all[l, 0:1][:, None]
        mod_s = mod_all[l, 1:n_cond][:, None]
        g = norm_g[l]
        if kind == 0:
            pad = LANES - 2 * SSM_HEADS
            p = {
                "w_in": jnp.pad(ssd_w_in[j], ((0, 0), (0, pad))).astype(BF16),
                "conv_w": ssd_conv_w[j],
                "conv_b": ssd_conv_b[j][None],
                "dt_bias": jnp.pad(ssd_dt_bias[j].reshape(1, -1), ((0, 0), (0, pad))),
                "a_log": jnp.pad(ssd_a_log[j].reshape(1, -1), ((0, 0), (0, pad))),
                "d_exp": jnp.broadcast_to(jnp.repeat(ssd_d[j], SSM_HEAD_DIM)[:, None],
                                          (SSM_D_INNER, SCAN_STEP)),
                "norm_g": ssd_norm_g[j][None],
                "w_out": ssd_w_out[j].astype(BF16),
            }
            h0 = state_ssm[:, j].reshape(bs, 2, SSM_D_INNER, SSM_STATE)
            xp, ssm_buf = _ssd_layer(xp, mod_p, g, p, None, (ssm_buf, n_ssd_layers, j), bp, lp_)
            xs, _ = _ssd_layer(xs, mod_s, g, p, h0, None, bs, ls)
        elif kind == 1:
            w_in, w_out = sc_w_in[j].astype(BF16), sc_w_out[j].astype(BF16)
            xp = _sconv(xp, mod_p, g, w_in, sc_conv_w[j], w_out, lp_)
            xs = _sconv(xs, mod_s, g, w_in, sc_conv_w[j], w_out, ls)
        else:
            lam_init = 0.8 - 0.6 * math.exp(-0.3 * l)
            w_qkv, w_o = da_w_qkv[j].astype(BF16), da_w_out[j].astype(BF16)
            sg = da_subln_g[j][None]
            qp, kp, vp = _qkv(xp, mod_p, g, w_qkv, None, lp_, F32)
            xp = _attention(qp, kp, vp, None, xp, mod_p, g, da_lambda[j], sg, w_o,
                            batch=bp, lam_init=lam_init)
            new_k.append(kp.reshape(bp, lp_, 2 * DIFF_HEADS, DIFF_HEAD_DIM))
            new_v.append(vp.reshape(bp, lp_, DIFF_HEADS, 2 * DIFF_HEAD_DIM))
            qs, ks, vs = _qkv(xs, mod_s, g, w_qkv, rot, ls, BF16)
            cache = (cache_k[:, j].reshape(bs, -1, d), cache_v[:, j].reshape(bs, -1, d))
            xs = _attention(qs, ks, vs, cache, xs, mod_s, g, da_lambda[j], sg, w_o,
                            batch=bs, lam_init=lam_init)
        w_up, w_dn = ffn_w_up[l].astype(BF16), ffn_w_down[l].astype(BF16)
        xp = _ffn(xp, mod_p, g, w_up, ffn_conv_w[l], w_dn, lp_)
        xs = _ffn(xs, mod_s, g, w_up, ffn_conv_w[l], w_dn, ls)

    new_ssm = ssm_buf.reshape(bp, n_ssd_layers, 2, SSM_HEADS, SSM_HEAD_DIM, SSM_STATE)
    return (xp.reshape(bp, lp_, d), xs.reshape(bs, ls, d), new_ssm,
            jnp.stack(new_k, axis=1), jnp.stack(new_v, axis=1))
```

```python
import functools
import math

import jax
import jax.numpy as jnp
from jax import lax
from jax.experimental import pallas as pl
from jax.experimental.pallas import tpu as pltpu

D_MODEL = 1024
DEPTH = 4
GRID_W = 64
N_MIXERS = 3
SSM_D_INNER = 2 * D_MODEL
SSM_HEAD_DIM = 64
SSM_HEADS = SSM_D_INNER // SSM_HEAD_DIM
SSM_GROUPS = 8
SSM_STATE = 128
SSM_CONV_DIM = SSM_D_INNER + 2 * SSM_GROUPS * SSM_STATE
HEADS_PER_GROUP = SSM_HEADS // SSM_GROUPS
GROUP_ROWS = HEADS_PER_GROUP * SSM_HEAD_DIM
DIFF_HEAD_DIM = 64
DIFF_HEADS = D_MODEL // (2 * DIFF_HEAD_DIM)
ROPE_THETA = 10000.0
ROT_PAIRS_PER_AXIS = DIFF_HEAD_DIM // 4
FFN_DIM = 2816
NORM_EPS = 1e-6

LANES = 128
HALO = 8
SCAN_CHUNK = 128
SCAN_STEP = 256
ROW_TILE = 256
FFN_ROW_TILE = 512
COL_CHUNK = 256
FFN_DOWN_GROUP = 4
VMEM_LIMIT = 56 * 1024 * 1024

F32 = jnp.float32
BF16 = jnp.bfloat16


def _cparams(n_axes):
    return pltpu.CompilerParams(dimension_semantics=("arbitrary",) * n_axes,
                                vmem_limit_bytes=VMEM_LIMIT)


def _rms(x, g):
    ms = jnp.mean(x * x, axis=-1, keepdims=True)
    return x * lax.rsqrt(ms + NORM_EPS) * g


def _silu(x):
    return x * jax.nn.sigmoid(x)


def _mod_slices(mod, first):
    d = D_MODEL
    return (mod[:, first * d:(first + 1) * d], mod[:, (first + 1) * d:(first + 2) * d],
            mod[:, (first + 2) * d:(first + 3) * d])


def _ext_norm(xp_ref, x_ref, xn_ref, g, sc, sh, seq_len):
    tiles_per_seq = seq_len // x_ref.shape[0]
    k = pl.program_id(0) % tiles_per_seq
    keep = (jnp.where(k == 0, 0.0, 1.0), None, jnp.where(k == tiles_per_seq - 1, 0.0, 1.0))
    pieces = []
    for r, kp in zip((xp_ref, x_ref, xn_ref), keep):
        h = _rms(r[...], g) * (1.0 + sc) + sh
        pieces.append(h if kp is None else h * kp)
    return jnp.concatenate(pieces, axis=0).astype(BF16)


def _conv3(u_ext, w3, tile_rows):
    n_ext = tile_rows + 2 * HALO
    up = pltpu.roll(u_ext, 1, 0)[HALO:HALO + tile_rows]
    un = pltpu.roll(u_ext, n_ext - 1, 0)[HALO:HALO + tile_rows]
    um = u_ext[HALO:HALO + tile_rows]
    return up * w3[0:1] + um * w3[1:2] + un * w3[2:3]


def _halo_specs(n_rows, tile_rows, width):
    per = tile_rows // HALO
    last = n_rows // HALO - 1
    return [
        pl.BlockSpec((HALO, width), lambda i: (jnp.maximum(i * per - 1, 0), 0)),
        pl.BlockSpec((tile_rows, width), lambda i: (i, 0)),
        pl.BlockSpec((HALO, width), lambda i: (jnp.minimum((i + 1) * per, last), 0)),
    ]


def _mod_spec(n_mod, tile_rows, seq_len):
    if n_mod == 1:
        return pl.BlockSpec((1, 1, 6 * D_MODEL), lambda i: (0, 0, 0))
    return pl.BlockSpec((1, 1, 6 * D_MODEL), lambda i: ((i * tile_rows) // seq_len, 0, 0))


def _const_spec(shape):
    nd = len(shape)
    return pl.BlockSpec(shape, lambda *_: (0,) * nd)


def _layer_spec(stacked, layer):
    shape = stacked.shape[1:]
    idx = (layer,) + (0,) * len(shape)
    return pl.BlockSpec((None,) + shape, lambda *_: idx, pipeline_mode=pl.Buffered(1))


def _mod_kernel(cond_ref, w_ref, b_ref, o_ref):
    a = _silu(cond_ref[...]).astype(BF16)
    o_ref[0] = jnp.dot(a, w_ref[0].astype(BF16), preferred_element_type=F32) + b_ref[0]


def _modulation(cond, w_mod, b_mod):
    n_cols = 6 * D_MODEL
    blk = n_cols // 4
    return pl.pallas_call(
        _mod_kernel,
        grid=(DEPTH, n_cols // blk),
        in_specs=[_const_spec(cond.shape),
                  pl.BlockSpec((1, D_MODEL, blk), lambda l, j: (l, 0, j)),
                  pl.BlockSpec((1, 1, blk), lambda l, j: (l, 0, j))],
        out_specs=pl.BlockSpec((1, cond.shape[0], blk), lambda l, j: (l, 0, j)),
        out_shape=jax.ShapeDtypeStruct((DEPTH, cond.shape[0], n_cols), F32),
        compiler_params=_cparams(2),
        name="modulation",
    )(cond, w_mod, b_mod.reshape(DEPTH, 1, n_cols))


def _ffn_kernel(xp_ref, x_ref, xn_ref, mod_ref, g_ref, wup_ref, cw_ref, wdn_ref, o_ref,
                hext_ref, act_ref, *, seq_len):
    t = x_ref.shape[0]
    sh, sc, gt = _mod_slices(mod_ref[0], 3)
    hext_ref[...] = _ext_norm(xp_ref, x_ref, xn_ref, g_ref[2:3], sc, sh, seq_len)
    n_chunks = FFN_DIM // COL_CHUNK
    acc = None
    for j0 in range(0, n_chunks, FFN_DOWN_GROUP):
        n_j = min(FFN_DOWN_GROUP, n_chunks - j0)
        for jj in range(n_j):
            cg = (j0 + jj) * COL_CHUNK
            cv = FFN_DIM + cg
            h = hext_ref[...]
            ug = jnp.dot(h, wup_ref[:, cg:cg + COL_CHUNK], preferred_element_type=F32)
            uv = jnp.dot(h, wup_ref[:, cv:cv + COL_CHUNK], preferred_element_type=F32)
            gate = _conv3(ug, cw_ref[:, cg:cg + COL_CHUNK], t)
            val = _conv3(uv, cw_ref[:, cv:cv + COL_CHUNK], t)
            act_ref[:, jj * COL_CHUNK:(jj + 1) * COL_CHUNK] = (_silu(gate) * val).astype(BF16)
        k0, kn = j0 * COL_CHUNK, n_j * COL_CHUNK
        part = jnp.dot(act_ref[:, :kn], wdn_ref[k0:k0 + kn, :], preferred_element_type=F32)
        acc = part if acc is None else acc + part
    o_ref[...] = x_ref[...] + gt * _rms(acc, g_ref[3:4])


def _ffn(x, mod, g, layer, w_up, conv_w, w_down, seq_len):
    n = x.shape[0]
    t = min(FFN_ROW_TILE, seq_len)
    return pl.pallas_call(
        functools.partial(_ffn_kernel, seq_len=seq_len),
        grid=(n // t,),
        in_specs=_halo_specs(n, t, D_MODEL) + [
            _mod_spec(mod.shape[0], t, seq_len), _const_spec(g.shape), _layer_spec(w_up, layer),
            _layer_spec(conv_w, layer), _layer_spec(w_down, layer)],
        out_specs=pl.BlockSpec((t, D_MODEL), lambda i: (i, 0)),
        out_shape=jax.ShapeDtypeStruct((n, D_MODEL), F32),
        scratch_shapes=[pltpu.VMEM((t + 2 * HALO, D_MODEL), BF16),
                        pltpu.VMEM((t, FFN_DOWN_GROUP * COL_CHUNK), BF16)],
        compiler_params=_cparams(1),
        name="conv_ffn",
    )(x, x, x, mod, g, w_up, conv_w, w_down)


def _sconv_kernel(xp_ref, x_ref, xn_ref, mod_ref, g_ref, win_ref, cw_ref, wout_ref, o_ref,
                  hext_ref, act_ref, *, seq_len):
    t = x_ref.shape[0]
    d = D_MODEL
    sh, sc, gt = _mod_slices(mod_ref[0], 0)
    hext_ref[...] = _ext_norm(xp_ref, x_ref, xn_ref, g_ref[0:1], sc, sh, seq_len)
    for j in range(d // COL_CHUNK):
        c0 = j * COL_CHUNK
        bg = jnp.dot(hext_ref[HALO:HALO + t, :], win_ref[:, c0:c0 + COL_CHUNK],
                     preferred_element_type=F32)
        h = hext_ref[...]
        cg = jnp.dot(h, win_ref[:, d + c0:d + c0 + COL_CHUNK], preferred_element_type=F32)
        u = jnp.dot(h, win_ref[:, 2 * d + c0:2 * d + c0 + COL_CHUNK], preferred_element_type=F32)
        conv = _conv3(cg * u, cw_ref[:, c0:c0 + COL_CHUNK], t)
        act_ref[:, c0:c0 + COL_CHUNK] = (bg * conv).astype(BF16)
    m = jnp.dot(act_ref[...], wout_ref[...], preferred_element_type=F32)
    o_ref[...] = x_ref[...] + gt * _rms(m, g_ref[1:2])


def _sconv(x, mod, g, w_in, conv_w, w_out, seq_len):
    n = x.shape[0]
    t = ROW_TILE
    return pl.pallas_call(
        functools.partial(_sconv_kernel, seq_len=seq_len),
        grid=(n // t,),
        in_specs=_halo_specs(n, t, D_MODEL) + [
            _mod_spec(mod.shape[0], t, seq_len), _const_spec(g.shape), _const_spec(w_in.shape),
            _const_spec(conv_w.shape), _const_spec(w_out.shape)],
        out_specs=pl.BlockSpec((t, D_MODEL), lambda i: (i, 0)),
        out_shape=jax.ShapeDtypeStruct((n, D_MODEL), F32),
        scratch_shapes=[pltpu.VMEM((t + 2 * HALO, D_MODEL), BF16), pltpu.VMEM((t, D_MODEL), BF16)],
        compiler_params=_cparams(1),
        name="short_conv_mixer",
    )(x, x, x, mod, g, w_in, conv_w, w_out)


def _ssd_in_kernel(xp_ref, x_ref, xn_ref, mod_ref, g_ref, win_ref, wdt_ref, cw_ref, cb_ref, dtb_ref,
                   z_ref, xT_ref, bm_ref, cm_ref, dt_ref, hext_ref, *, seq_len):
    t = x_ref.shape[0]
    di = SSM_D_INNER
    gn = SSM_GROUPS * SSM_STATE
    sh, sc, _ = _mod_slices(mod_ref[0], 0)
    hext_ref[...] = _ext_norm(xp_ref, x_ref, xn_ref, g_ref[0:1], sc, sh, seq_len)

    for j in range(di // COL_CHUNK):
        c0 = j * COL_CHUNK
        z_ref[:, c0:c0 + COL_CHUNK] = jnp.dot(hext_ref[HALO:HALO + t, :], win_ref[:, c0:c0 + COL_CHUNK],
                                              preferred_element_type=F32)

    for j in range(SSM_CONV_DIM // COL_CHUNK):
        c0 = j * COL_CHUNK
        u = jnp.dot(hext_ref[...], win_ref[:, di + c0:di + c0 + COL_CHUNK], preferred_element_type=F32)
        act = _silu(_conv3(u, cw_ref[:, c0:c0 + COL_CHUNK], t) + cb_ref[:, c0:c0 + COL_CHUNK])
        if c0 < di:
            xT_ref[c0:c0 + COL_CHUNK, :] = act.T
        elif c0 < di + gn:
            bm_ref[:, c0 - di:c0 - di + COL_CHUNK] = act.astype(BF16)
        else:
            cm_ref[:, c0 - di - gn:c0 - di - gn + COL_CHUNK] = act.astype(BF16)

    raw = jnp.dot(hext_ref[HALO:HALO + t, :], wdt_ref[...], preferred_element_type=F32) + dtb_ref[...]
    dt_ref[...] = jnp.maximum(raw, 0.0) + jnp.log1p(jnp.exp(-jnp.abs(raw)))


def _ssd_in(x, mod, g, p, layer, seq_len):
    n = x.shape[0]
    t = ROW_TILE
    gn = SSM_GROUPS * SSM_STATE
    weights = [p[k] for k in ("w_in", "w_dt", "conv_w", "conv_b", "dt_bias")]
    return pl.pallas_call(
        functools.partial(_ssd_in_kernel, seq_len=seq_len),
        grid=(n // t,),
        in_specs=_halo_specs(n, t, D_MODEL) + [
            _mod_spec(mod.shape[0], t, seq_len), _const_spec(g.shape)]
        + [_layer_spec(w, layer) for w in weights],
        out_specs=[pl.BlockSpec((t, SSM_D_INNER), lambda i: (i, 0)),
                   pl.BlockSpec((SSM_D_INNER, t), lambda i: (0, i)),
                   pl.BlockSpec((t, gn), lambda i: (i, 0)),
                   pl.BlockSpec((t, gn), lambda i: (i, 0)),
                   pl.BlockSpec((t, LANES), lambda i: (i, 0))],
        out_shape=[jax.ShapeDtypeStruct((n, SSM_D_INNER), F32),
                   jax.ShapeDtypeStruct((SSM_D_INNER, n), F32),
                   jax.ShapeDtypeStruct((n, gn), BF16),
                   jax.ShapeDtypeStruct((n, gn), BF16),
                   jax.ShapeDtypeStruct((n, LANES), F32)],
        scratch_shapes=[pltpu.VMEM((t + 2 * HALO, D_MODEL), BF16)],
        compiler_params=_cparams(1),
        name="ssd_in_proj",
    )(x, x, x, mod, g, *weights)


def _split3(v):
    hi = v.astype(BF16)
    r1 = v - hi.astype(F32)
    mid = r1.astype(BF16)
    lo = (r1 - mid.astype(F32)).astype(BF16)
    return hi, mid, lo


_CONTRACT_LAST = (((1,), (1,)), ((), ()))


def _chunk_decays(dt_ref, alog_ref, off, reverse):
    q = SCAN_CHUNK
    dt = dt_ref[off:off + q, :]
    da = dt * (-jnp.exp(alog_ref[...]))
    r_i = lax.broadcasted_iota(jnp.int32, (q, q), 0)
    c_i = lax.broadcasted_iota(jnp.int32, (q, q), 1)
    mask = (c_i >= r_i) if reverse else (c_i <= r_i)
    tri = jnp.where(mask, 1.0, 0.0).astype(BF16)
    acs = sum(jnp.dot(tri, part, preferred_element_type=F32) for part in _split3(da))
    acsT = acs.T
    dtT = dt.T
    srcT = acsT - jnp.log(dtT)
    last = 0 if reverse else q - 1
    totT = acsT[:, last:last + 1]
    wT = jnp.exp(totT - acsT) * dtT
    eaT = jnp.exp(acsT)
    etot = jnp.broadcast_to(jnp.exp(totT), (LANES, SSM_STATE))
    return mask, acs, srcT, wT, eaT, etot


def _scan_group(decays, xT_ref, bm_ref, cm_ref, yT_ref, st_g, off, g, reverse):
    mask, acs, srcT, wT, eaT, etot = decays
    tok = slice(off, off + SCAN_CHUNK)
    ns = slice(g * SSM_STATE, (g + 1) * SSM_STATE)
    bm_g = bm_ref[tok, ns]
    cm_g = cm_ref[tok, ns]
    cb = lax.dot_general(cm_g, bm_g, _CONTRACT_LAST, preferred_element_type=F32)
    y_off = lax.dot_general(st_g.astype(BF16), cm_g, _CONTRACT_LAST, preferred_element_type=F32)
    xT_g = xT_ref[g * GROUP_ROWS:(g + 1) * GROUP_ROWS, tok]
    head0 = (SSM_HEADS if reverse else 0) + g * HEADS_PER_GROUP
    xw, scale = [], []
    for r in range(HEADS_PER_GROUP):
        col = head0 + r
        hr = slice(r * SSM_HEAD_DIM, (r + 1) * SSM_HEAD_DIM)
        m = jnp.where(mask, cb * jnp.exp(acs[:, col:col + 1] - srcT[col:col + 1, :]), 0.0)
        xh = xT_g[hr, :]
        yh = lax.dot_general(xh.astype(BF16), m.astype(BF16), _CONTRACT_LAST,
                             preferred_element_type=F32)
        yh = yh + y_off[hr, :] * eaT[col:col + 1, :]
        yT_ref[g * GROUP_ROWS + r * SSM_HEAD_DIM:g * GROUP_ROWS + (r + 1) * SSM_HEAD_DIM, tok] = yh
        xw.append((xh * wT[col:col + 1, :]).astype(BF16))
        scale.append(jnp.broadcast_to(etot[col:col + 1, :], (SSM_HEAD_DIM, SSM_STATE)))
    d_state = jnp.dot(jnp.concatenate(xw, axis=0), bm_g, preferred_element_type=F32)
    return st_g * jnp.concatenate(scale, axis=0) + d_state


def _ssd_scan_kernel(*refs, reverse, has_h0, finish, emit_state, state_aliased, n_steps):
    refs = list(refs)
    xT_ref, bm_ref, cm_ref, dt_ref, alog_ref = refs[:5]
    pos = 5
    h0_ref = None
    if has_h0:
        h0_ref = refs[pos]
        pos += 1
    if finish:
        (z_ref, ybT_ref, x_ref, mod_ref, g_ref, dexp_ref, ng_ref, wout_ref) = refs[pos:pos + 8]
        pos += 8
    if state_aliased:
        pos += 1
    y_out_ref = refs[pos]
    pos += 1
    state_out_ref = None
    if emit_state:
        state_out_ref = refs[pos]
        pos += 1
    st_ref = refs[pos]
    yT_ref = refs[pos + 1] if finish else y_out_ref

    c = pl.program_id(1)

    @pl.when(c == 0)
    def _():
        if has_h0:
            st_ref[...] = h0_ref[0, 0, 0]
        else:
            st_ref[...] = jnp.zeros_like(st_ref)

    offsets = [k * SCAN_CHUNK for k in range(SCAN_STEP // SCAN_CHUNK)]
    if reverse:
        offsets.reverse()
    decays = [_chunk_decays(dt_ref, alog_ref, off, reverse) for off in offsets]
    for g in range(SSM_GROUPS):
        rows = slice(g * GROUP_ROWS, (g + 1) * GROUP_ROWS)
        st_g = st_ref[rows, :]
        for off, dec in zip(offsets, decays):
            st_g = _scan_group(dec, xT_ref, bm_ref, cm_ref, yT_ref, st_g, off, g, reverse)
        st_ref[rows, :] = st_g

    if emit_state:
        @pl.when(c == n_steps - 1)
        def _():
            state_out_ref[0, 0, 0] = st_ref[...]

    if finish:
        yT = yT_ref[...] + ybT_ref[...] + dexp_ref[...] * xT_ref[...]
        y = yT.T * _silu(z_ref[...])
        y = _rms(y, ng_ref[...])
        out = jnp.dot(y.astype(BF16), wout_ref[...], preferred_element_type=F32)
        _, _, gt = _mod_slices(mod_ref[0], 0)
        y_out_ref[...] = x_ref[...] + gt * _rms(out, g_ref[1:2])


def _ssd_scan(xT, bm, cm, dt, p, layer, h0, fin, state, *, batch, reverse):
    n = xT.shape[1]
    direction = 1 if reverse else 0
    emit_state = state is not None
    q = SCAN_STEP
    nc = n // batch // q
    gn = SSM_GROUPS * SSM_STATE
    finish = fin is not None

    def tok(b, c):
        return b * nc + ((nc - 1 - c) if reverse else c)

    def col_blk(b, c):
        return (0, tok(b, c))

    def row_blk(b, c):
        return (tok(b, c), 0)

    in_specs = [pl.BlockSpec((SSM_D_INNER, q), col_blk),
                pl.BlockSpec((q, gn), row_blk),
                pl.BlockSpec((q, gn), row_blk),
                pl.BlockSpec((q, LANES), row_blk),
                _layer_spec(p["a_log"], layer)]
    args = [xT, bm, cm, dt, p["a_log"]]
    if h0 is not None:
        in_specs.append(pl.BlockSpec((1, 1, 1, SSM_D_INNER, SSM_STATE),
                                     lambda b, c: (b, layer, direction, 0, 0)))
        args.append(h0)
    if finish:
        z, ybT, x, mod, g = fin
        n_mod = mod.shape[0]
        in_specs += [pl.BlockSpec((q, SSM_D_INNER), row_blk),
                     pl.BlockSpec((SSM_D_INNER, q), col_blk),
                     pl.BlockSpec((q, D_MODEL), row_blk),
                     pl.BlockSpec((1, 1, 6 * D_MODEL),
                                  (lambda b, c: (b, 0, 0)) if n_mod > 1 else (lambda b, c: (0, 0, 0))),
                     _const_spec(g.shape), _layer_spec(p["d_exp"], layer),
                     _layer_spec(p["norm_g"], layer), _layer_spec(p["w_out"], layer)]
        args += [z, ybT, x, mod, g, p["d_exp"], p["norm_g"], p["w_out"]]
        out_specs = [pl.BlockSpec((q, D_MODEL), row_blk)]
        out_shape = [jax.ShapeDtypeStruct((n, D_MODEL), F32)]
    else:
        out_specs = [pl.BlockSpec((SSM_D_INNER, q), col_blk)]
        out_shape = [jax.ShapeDtypeStruct((SSM_D_INNER, n), F32)]
    aliases = {}
    if emit_state:
        buf, n_layers = state
        out_specs.append(pl.BlockSpec((1, 1, 1, SSM_D_INNER, SSM_STATE),
                                      lambda b, c: (b, layer, direction, 0, 0)))
        out_shape.append(jax.ShapeDtypeStruct((batch, n_layers, 2, SSM_D_INNER, SSM_STATE), F32))
        if buf is not None:
            in_specs.append(pl.BlockSpec(memory_space=pl.ANY))
            args.append(buf)
            aliases = {len(args) - 1: 1}
    scratch = [pltpu.VMEM((SSM_D_INNER, SSM_STATE), F32)]
    if finish:
        scratch.append(pltpu.VMEM((SSM_D_INNER, q), F32))
    outs = pl.pallas_call(
        functools.partial(_ssd_scan_kernel, reverse=reverse, has_h0=h0 is not None, finish=finish,
                          emit_state=emit_state, state_aliased=bool(aliases), n_steps=nc),
        grid=(batch, nc),
        in_specs=in_specs, out_specs=out_specs, out_shape=out_shape,
        scratch_shapes=scratch,
        input_output_aliases=aliases,
        compiler_params=_cparams(2),
        name="ssd_scan_bwd" if reverse else "ssd_scan_fwd",
    )(*args)
    return outs if emit_state else (outs[0], None)


def _ssd_layer(x, mod, g, p, layer, h0, state, batch, seq_len):
    z, xT, bm, cm, dt = _ssd_in(x, mod, g, p, layer, seq_len)
    ybT, buf = _ssd_scan(xT, bm, cm, dt, p, layer, h0, None, state, batch=batch, reverse=True)
    if state is not None:
        state = (buf, state[1])
    x_new, buf = _ssd_scan(xT, bm, cm, dt, p, layer, h0, (z, ybT, x, mod, g), state,
                           batch=batch, reverse=False)
    return x_new, buf


def _qkv_kernel(*refs, rotary):
    if rotary:
        x_ref, mod_ref, g_ref, w_ref, cos_ref, sin_ref, q_ref, k_ref, v_ref = refs
    else:
        x_ref, mod_ref, g_ref, w_ref, q_ref, k_ref, v_ref = refs
    d = D_MODEL
    sh, sc, _ = _mod_slices(mod_ref[0], 0)
    h = (_rms(x_ref[...], g_ref[0:1]) * (1.0 + sc) + sh).astype(BF16)
    if rotary:
        lane = lax.broadcasted_iota(jnp.int32, (x_ref.shape[0], LANES), 1)
        first_half = (lane & (DIFF_HEAD_DIM - 1)) < DIFF_HEAD_DIM // 2
        cos = cos_ref[...]
        sin = sin_ref[...]

    def rot(a):
        if not rotary:
            return a
        partner = jnp.where(first_half, pltpu.roll(a, LANES - DIFF_HEAD_DIM // 2, 1),
                            pltpu.roll(a, DIFF_HEAD_DIM // 2, 1))
        return a * cos + partner * sin

    scale = DIFF_HEAD_DIM ** -0.5 * math.log2(math.e)
    for j in range(d // LANES):
        cs = slice(j * LANES, (j + 1) * LANES)
        qj = jnp.dot(h, w_ref[:, j * LANES:(j + 1) * LANES], preferred_element_type=F32)
        kj = jnp.dot(h, w_ref[:, d + j * LANES:d + (j + 1) * LANES], preferred_element_type=F32)
        q_ref[:, cs] = (rot(qj) * scale).astype(q_ref.dtype)
        k_ref[:, cs] = rot(kj).astype(k_ref.dtype)
    v_ref[...] = jnp.dot(h, w_ref[:, 2 * d:3 * d], preferred_element_type=F32).astype(v_ref.dtype)


def _qkv(x, mod, g, w_qkv, rot_tables, seq_len, kv_dtype):
    n = x.shape[0]
    t = ROW_TILE
    rotary = rot_tables is not None
    in_specs = [pl.BlockSpec((t, D_MODEL), lambda i: (i, 0)), _mod_spec(mod.shape[0], t, seq_len),
                _const_spec(g.shape), _const_spec(w_qkv.shape)]
    args = [x, mod, g, w_qkv]
    if rotary:
        per_seq = seq_len // t
        in_specs += [pl.BlockSpec((t, LANES), lambda i: (i % per_seq, 0))] * 2
        args += list(rot_tables)
    row_spec = pl.BlockSpec((t, D_MODEL), lambda i: (i, 0))
    return pl.pallas_call(
        functools.partial(_qkv_kernel, rotary=rotary),
        grid=(n // t,),
        in_specs=in_specs,
        out_specs=[row_spec, row_spec, row_spec],
        out_shape=[jax.ShapeDtypeStruct((n, D_MODEL), BF16),
                   jax.ShapeDtypeStruct((n, D_MODEL), kv_dtype),
                   jax.ShapeDtypeStruct((n, D_MODEL), kv_dtype)],
        compiler_params=_cparams(1),
        name="diff_qkv",
    )(*args)


def _attn_kernel(*refs, has_cache, lam_init):
    if has_cache:
        (q_ref, k_ref, v_ref, ck_ref, cv_ref, x_ref, mod_ref, g_ref, lp_ref, sg_ref, wo_ref,
         o_ref, oall_ref) = refs
    else:
        q_ref, k_ref, v_ref, x_ref, mod_ref, g_ref, lp_ref, sg_ref, wo_ref, o_ref, oall_ref = refs
    lp = lp_ref[...]
    lam = (jnp.exp(jnp.sum(lp[0:1] * lp[1:2], axis=-1, keepdims=True))
           - jnp.exp(jnp.sum(lp[2:3] * lp[3:4], axis=-1, keepdims=True)) + lam_init)
    dn = (((1,), (1,)), ((), ()))
    hd = DIFF_HEAD_DIM

    def scores(hh):
        hs = slice(hh * hd, (hh + 1) * hd)
        qh = q_ref[:, hs]
        s_new = lax.dot_general(qh, k_ref[:, hs].astype(BF16), dn, preferred_element_type=F32)
        s_old = None
        if has_cache:
            s_old = lax.dot_general(qh, ck_ref[0, :, hs].astype(BF16), dn, preferred_element_type=F32)
        return s_old, s_new

    def probs(s_old, s_new):
        m = jnp.max(s_new, axis=-1, keepdims=True)
        if has_cache:
            m = jnp.maximum(m, jnp.max(s_old, axis=-1, keepdims=True))
            p_old = jnp.exp2(s_old - m)
        p_new = jnp.exp2(s_new - m)
        l = jnp.sum(p_new, axis=-1, keepdims=True)
        if has_cache:
            l = l + jnp.sum(p_old, axis=-1, keepdims=True)
            return p_old, p_new, l
        return None, p_new, l

    def diff_weights(pair_scores):
        po0, pn0, l0 = probs(*pair_scores[0])
        po1, pn1, l1 = probs(*pair_scores[1])
        ratio = lam * l0 / l1
        att_old = (po0 - po1 * ratio).astype(BF16) if has_cache else None
        return att_old, (pn0 - pn1 * ratio).astype(BF16), l0

    def weighted_values(hp, att_old, att, l0):
        vs = slice(hp * 2 * hd, (hp + 1) * 2 * hd)
        o = jnp.dot(att, v_ref[:, vs].astype(BF16), preferred_element_type=F32)
        if has_cache:
            o = o + jnp.dot(att_old, cv_ref[0, :, vs].astype(BF16), preferred_element_type=F32)
        o = o * (1.0 / l0)
        oall_ref[:, vs] = (_rms(o, sg_ref[...]) * (1.0 - lam_init)).astype(BF16)

    sc, att = {}, {}
    for step in range(DIFF_HEADS + 2):
        if step < DIFF_HEADS:
            sc[step] = (scores(2 * step), scores(2 * step + 1))
        if step >= 2:
            weighted_values(step - 2, *att.pop(step - 2))
        if 1 <= step <= DIFF_HEADS:
            att[step - 1] = diff_weights(sc.pop(step - 1))

    m_out = jnp.dot(oall_ref[...], wo_ref[...], preferred_element_type=F32)
    _, _, gt = _mod_slices(mod_ref[0], 0)
    o_ref[...] = x_ref[...] + gt * _rms(m_out, g_ref[1:2])


def _attention(q, k, v, cache, x, mod, g, lam_p, subln_g, w_o, *, batch, lam_init):
    n = x.shape[0]
    seq = n // batch
    tq = ROW_TILE
    nq = seq // tq
    has_cache = cache is not None
    kv_spec = pl.BlockSpec((seq, D_MODEL), lambda b, t: (b, 0))
    in_specs = [pl.BlockSpec((tq, D_MODEL), lambda b, t: (b * nq + t, 0)), kv_spec, kv_spec]
    args = [q, k, v]
    if has_cache:
        past = cache[0].shape[1]
        c_spec = pl.BlockSpec((1, past, D_MODEL), lambda b, t: (b, 0, 0))
        in_specs += [c_spec, c_spec]
        args += list(cache)
    n_mod = mod.shape[0]
    in_specs += [pl.BlockSpec((tq, D_MODEL), lambda b, t: (b * nq + t, 0)),
                 pl.BlockSpec((1, 1, 6 * D_MODEL),
                              (lambda b, t: (b, 0, 0)) if n_mod > 1 else (lambda b, t: (0, 0, 0))),
                 _const_spec(g.shape), _const_spec(lam_p.shape), _const_spec(subln_g.shape),
                 _const_spec(w_o.shape)]
    args += [x, mod, g, lam_p, subln_g, w_o]
    return pl.pallas_call(
        functools.partial(_attn_kernel, has_cache=has_cache, lam_init=lam_init),
        grid=(batch, nq),
        in_specs=in_specs,
        out_specs=pl.BlockSpec((tq, D_MODEL), lambda b, t: (b * nq + t, 0)),
        out_shape=jax.ShapeDtypeStruct((n, D_MODEL), F32),
        scratch_shapes=[pltpu.VMEM((tq, D_MODEL), BF16)],
        compiler_params=_cparams(2),
        name="diff_attention",
    )(*args)


def _rotary_tables(n_tokens):
    rows = n_tokens // GRID_W
    row = jnp.repeat(jnp.arange(rows, dtype=F32), GRID_W)
    col = jnp.tile(jnp.arange(GRID_W, dtype=F32), rows)
    inv = ROPE_THETA ** (-jnp.arange(ROT_PAIRS_PER_AXIS, dtype=F32) / ROT_PAIRS_PER_AXIS)
    ang = jnp.concatenate([row[:, None] * inv, col[:, None] * inv], axis=-1)
    cos, sin = jnp.cos(ang), jnp.sin(ang)
    reps = LANES // DIFF_HEAD_DIM
    return (jnp.tile(jnp.concatenate([cos, cos], axis=-1), (1, reps)),
            jnp.tile(jnp.concatenate([-sin, sin], axis=-1), (1, reps)))


def kernel(x_prompt, x_sample, state_ssm, cache_k, cache_v, c, c_ctx, w_mod, b_mod, norm_g, ssd_w_in, ssd_conv_w, ssd_conv_b, ssd_dt_bias, ssd_a_log, ssd_d, ssd_norm_g, ssd_w_out, sc_w_in, sc_conv_w, sc_w_out, da_w_qkv, da_lambda, da_subln_g, da_w_out, ffn_w_up, ffn_conv_w, ffn_w_down):
    bp, lp_, d = x_prompt.shape
    bs, ls, _ = x_sample.shape
    xp = x_prompt.reshape(bp * lp_, d)
    xs = x_sample.reshape(bs * ls, d)

    n_cond = 1 + bs
    cond = jnp.concatenate([c_ctx[None], c, jnp.zeros((HALO - n_cond, d), F32)], axis=0)
    mod_all = _modulation(cond, w_mod, b_mod)

    rot = _rotary_tables(ls)
    n_ssd_layers = (DEPTH + 2) // N_MIXERS
    pad = LANES - 2 * SSM_HEADS
    xbc_end = SSM_D_INNER + SSM_CONV_DIM
    ssd = {
        "w_in": ssd_w_in.astype(BF16),
        "w_dt": jnp.pad(ssd_w_in[:, :, xbc_end:], ((0, 0), (0, 0), (0, pad))).astype(BF16),
        "conv_w": ssd_conv_w,
        "conv_b": ssd_conv_b[:, None],
        "dt_bias": jnp.pad(ssd_dt_bias.reshape(n_ssd_layers, 1, -1), ((0, 0), (0, 0), (0, pad))),
        "a_log": jnp.pad(ssd_a_log.reshape(n_ssd_layers, 1, -1), ((0, 0), (0, 0), (0, pad))),
        "d_exp": jnp.broadcast_to(jnp.repeat(ssd_d, SSM_HEAD_DIM, axis=1)[:, :, None],
                                  (n_ssd_layers, SSM_D_INNER, SCAN_STEP)),
        "norm_g": ssd_norm_g[:, None],
        "w_out": ssd_w_out.astype(BF16),
    }
    h0 = state_ssm.reshape(bs, n_ssd_layers, 2, SSM_D_INNER, SSM_STATE)
    w_up_all, w_dn_all = ffn_w_up.astype(BF16), ffn_w_down.astype(BF16)
    ssm_buf = None
    new_k, new_v = [], []
    for l in range(DEPTH):
        j = l // N_MIXERS
        kind = l % N_MIXERS
        mod_p = mod_all[l, 0:1][:, None]
        mod_s = mod_all[l, 1:n_cond][:, None]
        g = norm_g[l]
        if kind == 0:
            xp, ssm_buf = _ssd_layer(xp, mod_p, g, ssd, j, None, (ssm_buf, n_ssd_layers), bp, lp_)
            xs, _ = _ssd_layer(xs, mod_s, g, ssd, j, h0, None, bs, ls)
        elif kind == 1:
            w_in, w_out = sc_w_in[j].astype(BF16), sc_w_out[j].astype(BF16)
            xp = _sconv(xp, mod_p, g, w_in, sc_conv_w[j], w_out, lp_)
            xs = _sconv(xs, mod_s, g, w_in, sc_conv_w[j], w_out, ls)
        else:
            lam_init = 0.8 - 0.6 * math.exp(-0.3 * l)
            w_qkv, w_o = da_w_qkv[j].astype(BF16), da_w_out[j].astype(BF16)
            sg = da_subln_g[j][None]
            qp, kp, vp = _qkv(xp, mod_p, g, w_qkv, None, lp_, F32)
            xp = _attention(qp, kp, vp, None, xp, mod_p, g, da_lambda[j], sg, w_o,
                            batch=bp, lam_init=lam_init)
            new_k.append(kp.reshape(bp, lp_, 2 * DIFF_HEADS, DIFF_HEAD_DIM))
            new_v.append(vp.reshape(bp, lp_, DIFF_HEADS, 2 * DIFF_HEAD_DIM))
            qs, ks, vs = _qkv(xs, mod_s, g, w_qkv, rot, ls, BF16)
            cache = (cache_k[:, j].reshape(bs, -1, d), cache_v[:, j].reshape(bs, -1, d))
            xs = _attention(qs, ks, vs, cache, xs, mod_s, g, da_lambda[j], sg, w_o,
                            batch=bs, lam_init=lam_init)
        xp = _ffn(xp, mod_p, g, l, w_up_all, ffn_conv_w, w_dn_all, lp_)
        xs = _ffn(xs, mod_s, g, l, w_up_all, ffn_conv_w, w_dn_all, ls)

    new_ssm = ssm_buf.reshape(bp, n_ssd_layers, 2, SSM_HEADS, SSM_HEAD_DIM, SSM_STATE)
    return (xp.reshape(bp, lp_, d), xs.reshape(bs, ls, d), new_ssm,
            jnp.stack(new_k, axis=1), jnp.stack(new_v, axis=1))
```

```python
import functools
import math

import jax
import jax.numpy as jnp
from jax import lax
from jax.experimental import pallas as pl
from jax.experimental.pallas import tpu as pltpu

D_MODEL = 1024
DEPTH = 4
GRID_W = 64
N_MIXERS = 3
SSM_D_INNER = 2 * D_MODEL
SSM_HEAD_DIM = 64
SSM_HEADS = SSM_D_INNER // SSM_HEAD_DIM
SSM_GROUPS = 8
SSM_STATE = 128
SSM_CONV_DIM = SSM_D_INNER + 2 * SSM_GROUPS * SSM_STATE
HEADS_PER_GROUP = SSM_HEADS // SSM_GROUPS
GROUP_ROWS = HEADS_PER_GROUP * SSM_HEAD_DIM
DIFF_HEAD_DIM = 64
DIFF_HEADS = D_MODEL // (2 * DIFF_HEAD_DIM)
ROPE_THETA = 10000.0
ROT_PAIRS_PER_AXIS = DIFF_HEAD_DIM // 4
FFN_DIM = 2816
NORM_EPS = 1e-6

LANES = 128
HALO = 8
SCAN_CHUNK = 128
SCAN_STEP = 256
ROW_TILE = 512
ATTN_Q_TILE = 256
COL_CHUNK = 256
FFN_DOWN_GROUP = 4
VMEM_LIMIT = 56 * 1024 * 1024

F32 = jnp.float32
BF16 = jnp.bfloat16


def _cparams(n_axes):
    return pltpu.CompilerParams(dimension_semantics=("arbitrary",) * n_axes,
                                vmem_limit_bytes=VMEM_LIMIT)


def _rms(x, g):
    ms = jnp.mean(x * x, axis=-1, keepdims=True)
    return x * lax.rsqrt(ms + NORM_EPS) * g


def _silu(x):
    return x * jax.nn.sigmoid(x)


def _mod_slices(mod, first):
    d = D_MODEL
    return (mod[:, first * d:(first + 1) * d], mod[:, (first + 1) * d:(first + 2) * d],
            mod[:, (first + 2) * d:(first + 3) * d])


def _ext_rows(tile_rows, seq_len):
    return tile_rows if seq_len <= tile_rows else tile_rows + 2 * HALO


def _main_rows(tile_rows, seq_len):
    return slice(0, tile_rows) if seq_len <= tile_rows else slice(HALO, HALO + tile_rows)


def _ext_norm(xp_ref, x_ref, xn_ref, g, sc, sh, seq_len):
    t = x_ref.shape[0]
    if seq_len <= t:
        return (_rms(x_ref[...], g) * (1.0 + sc) + sh).astype(BF16)
    tiles_per_seq = seq_len // t
    k = pl.program_id(0) % tiles_per_seq
    keep = (jnp.where(k == 0, 0.0, 1.0), None, jnp.where(k == tiles_per_seq - 1, 0.0, 1.0))
    pieces = []
    for r, kp in zip((xp_ref, x_ref, xn_ref), keep):
        h = _rms(r[...], g) * (1.0 + sc) + sh
        pieces.append(h if kp is None else h * kp)
    return jnp.concatenate(pieces, axis=0).astype(BF16)


def _conv3(u, w3, tile_rows, seq_len):
    n = u.shape[0]
    up = pltpu.roll(u, 1, 0)
    un = pltpu.roll(u, n - 1, 0)
    if n == tile_rows:
        pos = lax.broadcasted_iota(jnp.int32, u.shape, 0) & (seq_len - 1)
        up = jnp.where(pos != 0, up, 0.0)
        un = jnp.where(pos != seq_len - 1, un, 0.0)
        return up * w3[0:1] + u * w3[1:2] + un * w3[2:3]
    rows = slice(HALO, HALO + tile_rows)
    return up[rows] * w3[0:1] + u[rows] * w3[1:2] + un[rows] * w3[2:3]


def _halo_specs(n_rows, tile_rows, width):
    per = tile_rows // HALO
    last = n_rows // HALO - 1
    return [
        pl.BlockSpec((HALO, width), lambda i: (jnp.maximum(i * per - 1, 0), 0)),
        pl.BlockSpec((tile_rows, width), lambda i: (i, 0)),
        pl.BlockSpec((HALO, width), lambda i: (jnp.minimum((i + 1) * per, last), 0)),
    ]


def _mod_spec(n_mod, tile_rows, seq_len):
    if n_mod == 1:
        return pl.BlockSpec((1, 1, 6 * D_MODEL), lambda i: (0, 0, 0))
    return pl.BlockSpec((1, 1, 6 * D_MODEL), lambda i: ((i * tile_rows) // seq_len, 0, 0))


def _const_spec(shape):
    nd = len(shape)
    return pl.BlockSpec(shape, lambda *_: (0,) * nd)


def _layer_spec(stacked, layer):
    shape = stacked.shape[1:]
    idx = (layer,) + (0,) * len(shape)
    return pl.BlockSpec((None,) + shape, lambda *_: idx, pipeline_mode=pl.Buffered(1))


def _mod_kernel(cond_ref, w_ref, b_ref, o_ref):
    a = _silu(cond_ref[...]).astype(BF16)
    o_ref[0] = jnp.dot(a, w_ref[0].astype(BF16), preferred_element_type=F32) + b_ref[0]


def _modulation(cond, w_mod, b_mod):
    n_cols = 6 * D_MODEL
    blk = n_cols // 4
    return pl.pallas_call(
        _mod_kernel,
        grid=(DEPTH, n_cols // blk),
        in_specs=[_const_spec(cond.shape),
                  pl.BlockSpec((1, D_MODEL, blk), lambda l, j: (l, 0, j)),
                  pl.BlockSpec((1, 1, blk), lambda l, j: (l, 0, j))],
        out_specs=pl.BlockSpec((1, cond.shape[0], blk), lambda l, j: (l, 0, j)),
        out_shape=jax.ShapeDtypeStruct((DEPTH, cond.shape[0], n_cols), F32),
        compiler_params=_cparams(2),
        name="modulation",
    )(cond, w_mod, b_mod.reshape(DEPTH, 1, n_cols))


def _ffn_kernel(xp_ref, x_ref, xn_ref, mod_ref, g_ref, wup_ref, cw_ref, wdn_ref, o_ref,
                hext_ref, act_ref, *, seq_len):
    t = x_ref.shape[0]
    sh, sc, gt = _mod_slices(mod_ref[0], 3)
    hext_ref[...] = _ext_norm(xp_ref, x_ref, xn_ref, g_ref[2:3], sc, sh, seq_len)
    n_chunks = FFN_DIM // COL_CHUNK
    acc = None
    for j0 in range(0, n_chunks, FFN_DOWN_GROUP):
        n_j = min(FFN_DOWN_GROUP, n_chunks - j0)
        for jj in range(n_j):
            cg = (j0 + jj) * COL_CHUNK
            cv = FFN_DIM + cg
            h = hext_ref[...]
            ug = jnp.dot(h, wup_ref[:, cg:cg + COL_CHUNK], preferred_element_type=F32)
            uv = jnp.dot(h, wup_ref[:, cv:cv + COL_CHUNK], preferred_element_type=F32)
            gate = _conv3(ug, cw_ref[:, cg:cg + COL_CHUNK], t, seq_len)
            val = _conv3(uv, cw_ref[:, cv:cv + COL_CHUNK], t, seq_len)
            act_ref[:, jj * COL_CHUNK:(jj + 1) * COL_CHUNK] = (_silu(gate) * val).astype(BF16)
        k0, kn = j0 * COL_CHUNK, n_j * COL_CHUNK
        part = jnp.dot(act_ref[:, :kn], wdn_ref[k0:k0 + kn, :], preferred_element_type=F32)
        acc = part if acc is None else acc + part
    o_ref[...] = x_ref[...] + gt * _rms(acc, g_ref[3:4])


def _ffn(x, mod, g, layer, w_up, conv_w, w_down, seq_len):
    n = x.shape[0]
    t = ROW_TILE
    return pl.pallas_call(
        functools.partial(_ffn_kernel, seq_len=seq_len),
        grid=(n // t,),
        in_specs=_halo_specs(n, t, D_MODEL) + [
            _mod_spec(mod.shape[0], t, seq_len), _const_spec(g.shape), _layer_spec(w_up, layer),
            _layer_spec(conv_w, layer), _layer_spec(w_down, layer)],
        out_specs=pl.BlockSpec((t, D_MODEL), lambda i: (i, 0)),
        out_shape=jax.ShapeDtypeStruct((n, D_MODEL), F32),
        scratch_shapes=[pltpu.VMEM((_ext_rows(t, seq_len), D_MODEL), BF16),
                        pltpu.VMEM((t, FFN_DOWN_GROUP * COL_CHUNK), BF16)],
        compiler_params=_cparams(1),
        name="conv_ffn",
    )(x, x, x, mod, g, w_up, conv_w, w_down)


def _sconv_kernel(xp_ref, x_ref, xn_ref, mod_ref, g_ref, win_ref, cw_ref, wout_ref, o_ref,
                  hext_ref, act_ref, *, seq_len):
    t = x_ref.shape[0]
    d = D_MODEL
    sh, sc, gt = _mod_slices(mod_ref[0], 0)
    hext_ref[...] = _ext_norm(xp_ref, x_ref, xn_ref, g_ref[0:1], sc, sh, seq_len)
    for j in range(d // COL_CHUNK):
        c0 = j * COL_CHUNK
        bg = jnp.dot(hext_ref[_main_rows(t, seq_len), :], win_ref[:, c0:c0 + COL_CHUNK],
                     preferred_element_type=F32)
        h = hext_ref[...]
        cg = jnp.dot(h, win_ref[:, d + c0:d + c0 + COL_CHUNK], preferred_element_type=F32)
        u = jnp.dot(h, win_ref[:, 2 * d + c0:2 * d + c0 + COL_CHUNK], preferred_element_type=F32)
        conv = _conv3(cg * u, cw_ref[:, c0:c0 + COL_CHUNK], t, seq_len)
        act_ref[:, c0:c0 + COL_CHUNK] = (bg * conv).astype(BF16)
    m = jnp.dot(act_ref[...], wout_ref[...], preferred_element_type=F32)
    o_ref[...] = x_ref[...] + gt * _rms(m, g_ref[1:2])


def _sconv(x, mod, g, w_in, conv_w, w_out, seq_len):
    n = x.shape[0]
    t = ROW_TILE
    return pl.pallas_call(
        functools.partial(_sconv_kernel, seq_len=seq_len),
        grid=(n // t,),
        in_specs=_halo_specs(n, t, D_MODEL) + [
            _mod_spec(mod.shape[0], t, seq_len), _const_spec(g.shape), _const_spec(w_in.shape),
            _const_spec(conv_w.shape), _const_spec(w_out.shape)],
        out_specs=pl.BlockSpec((t, D_MODEL), lambda i: (i, 0)),
        out_shape=jax.ShapeDtypeStruct((n, D_MODEL), F32),
        scratch_shapes=[pltpu.VMEM((_ext_rows(t, seq_len), D_MODEL), BF16),
                        pltpu.VMEM((t, D_MODEL), BF16)],
        compiler_params=_cparams(1),
        name="short_conv_mixer",
    )(x, x, x, mod, g, w_in, conv_w, w_out)


def _ssd_in_kernel(xp_ref, x_ref, xn_ref, mod_ref, g_ref, win_ref, wdt_ref, cw_ref, cb_ref, dtb_ref,
                   z_ref, xT_ref, bm_ref, cm_ref, dt_ref, hext_ref, *, seq_len):
    t = x_ref.shape[0]
    di = SSM_D_INNER
    gn = SSM_GROUPS * SSM_STATE
    sh, sc, _ = _mod_slices(mod_ref[0], 0)
    hext_ref[...] = _ext_norm(xp_ref, x_ref, xn_ref, g_ref[0:1], sc, sh, seq_len)

    for j in range(di // COL_CHUNK):
        c0 = j * COL_CHUNK
        z_ref[:, c0:c0 + COL_CHUNK] = jnp.dot(hext_ref[_main_rows(t, seq_len), :],
                                              win_ref[:, c0:c0 + COL_CHUNK], preferred_element_type=F32)

    for j in range(SSM_CONV_DIM // COL_CHUNK):
        c0 = j * COL_CHUNK
        u = jnp.dot(hext_ref[...], win_ref[:, di + c0:di + c0 + COL_CHUNK], preferred_element_type=F32)
        act = _silu(_conv3(u, cw_ref[:, c0:c0 + COL_CHUNK], t, seq_len) + cb_ref[:, c0:c0 + COL_CHUNK])
        if c0 < di:
            xT_ref[c0:c0 + COL_CHUNK, :] = act.T
        elif c0 < di + gn:
            bm_ref[:, c0 - di:c0 - di + COL_CHUNK] = act.astype(BF16)
        else:
            cm_ref[:, c0 - di - gn:c0 - di - gn + COL_CHUNK] = act.astype(BF16)

    raw = jnp.dot(hext_ref[_main_rows(t, seq_len), :], wdt_ref[...],
                  preferred_element_type=F32) + dtb_ref[...]
    dt_ref[...] = jnp.maximum(raw, 0.0) + jnp.log1p(jnp.exp(-jnp.abs(raw)))


def _ssd_in(x, mod, g, p, layer, seq_len):
    n = x.shape[0]
    t = SCAN_STEP
    gn = SSM_GROUPS * SSM_STATE
    weights = [p[k] for k in ("w_in", "w_dt", "conv_w", "conv_b", "dt_bias")]
    return pl.pallas_call(
        functools.partial(_ssd_in_kernel, seq_len=seq_len),
        grid=(n // t,),
        in_specs=_halo_specs(n, t, D_MODEL) + [
            _mod_spec(mod.shape[0], t, seq_len), _const_spec(g.shape)]
        + [_layer_spec(w, layer) for w in weights],
        out_specs=[pl.BlockSpec((t, SSM_D_INNER), lambda i: (i, 0)),
                   pl.BlockSpec((None, SSM_D_INNER, t), lambda i: (i, 0, 0)),
                   pl.BlockSpec((t, gn), lambda i: (i, 0)),
                   pl.BlockSpec((t, gn), lambda i: (i, 0)),
                   pl.BlockSpec((t, LANES), lambda i: (i, 0))],
        out_shape=[jax.ShapeDtypeStruct((n, SSM_D_INNER), F32),
                   jax.ShapeDtypeStruct((n // t, SSM_D_INNER, t), F32),
                   jax.ShapeDtypeStruct((n, gn), BF16),
                   jax.ShapeDtypeStruct((n, gn), BF16),
                   jax.ShapeDtypeStruct((n, LANES), F32)],
        scratch_shapes=[pltpu.VMEM((_ext_rows(t, seq_len), D_MODEL), BF16)],
        compiler_params=_cparams(1),
        name="ssd_in_proj",
    )(x, x, x, mod, g, *weights)


def _split3(v):
    hi = v.astype(BF16)
    r1 = v - hi.astype(F32)
    mid = r1.astype(BF16)
    lo = (r1 - mid.astype(F32)).astype(BF16)
    return hi, mid, lo


_CONTRACT_LAST = (((1,), (1,)), ((), ()))


def _chunk_decays(dt_ref, alog_ref, off, reverse):
    q = SCAN_CHUNK
    dt = dt_ref[off:off + q, :]
    da = dt * (-jnp.exp(alog_ref[...]))
    r_i = lax.broadcasted_iota(jnp.int32, (q, q), 0)
    c_i = lax.broadcasted_iota(jnp.int32, (q, q), 1)
    mask = (c_i >= r_i) if reverse else (c_i <= r_i)
    tri = jnp.where(mask, 1.0, 0.0).astype(BF16)
    acs = sum(jnp.dot(tri, part, preferred_element_type=F32) for part in _split3(da))
    acsT = acs.T
    dtT = dt.T
    srcT = acsT - jnp.log(dtT)
    last = 0 if reverse else q - 1
    totT = acsT[:, last:last + 1]
    wT = jnp.exp(totT - acsT) * dtT
    eaT = jnp.exp(acsT)
    etot = jnp.broadcast_to(jnp.exp(totT), (LANES, SSM_STATE))
    return mask, acs, srcT, wT, eaT, etot


def _scan_group(decays, xT_ref, bm_ref, cm_ref, yT_ref, st_g, off, g, reverse):
    mask, acs, srcT, wT, eaT, etot = decays
    tok = slice(off, off + SCAN_CHUNK)
    ns = slice(g * SSM_STATE, (g + 1) * SSM_STATE)
    bm_g = bm_ref[tok, ns]
    cm_g = cm_ref[tok, ns]
    cb = lax.dot_general(cm_g, bm_g, _CONTRACT_LAST, preferred_element_type=F32)
    y_off = lax.dot_general(st_g.astype(BF16), cm_g, _CONTRACT_LAST, preferred_element_type=F32)
    xT_g = xT_ref[g * GROUP_ROWS:(g + 1) * GROUP_ROWS, tok]
    head0 = (SSM_HEADS if reverse else 0) + g * HEADS_PER_GROUP
    xw, scale = [], []
    for r in range(HEADS_PER_GROUP):
        col = head0 + r
        hr = slice(r * SSM_HEAD_DIM, (r + 1) * SSM_HEAD_DIM)
        m = jnp.where(mask, cb * jnp.exp(acs[:, col:col + 1] - srcT[col:col + 1, :]), 0.0)
        xh = xT_g[hr, :]
        yh = lax.dot_general(xh.astype(BF16), m.astype(BF16), _CONTRACT_LAST,
                             preferred_element_type=F32)
        yh = yh + y_off[hr, :] * eaT[col:col + 1, :]
        yT_ref[g * GROUP_ROWS + r * SSM_HEAD_DIM:g * GROUP_ROWS + (r + 1) * SSM_HEAD_DIM, tok] = yh
        xw.append((xh * wT[col:col + 1, :]).astype(BF16))
        scale.append(jnp.broadcast_to(etot[col:col + 1, :], (SSM_HEAD_DIM, SSM_STATE)))
    d_state = jnp.dot(jnp.concatenate(xw, axis=0), bm_g, preferred_element_type=F32)
    return st_g * jnp.concatenate(scale, axis=0) + d_state


def _ssd_scan_kernel(*refs, reverse, has_h0, finish, emit_state, state_aliased, n_steps):
    refs = list(refs)
    xT_ref, bm_ref, cm_ref, dt_ref, alog_ref = refs[:5]
    pos = 5
    h0_ref = None
    if has_h0:
        h0_ref = refs[pos]
        pos += 1
    if finish:
        (z_ref, ybT_ref, x_ref, mod_ref, g_ref, dexp_ref, ng_ref, wout_ref) = refs[pos:pos + 8]
        pos += 8
    if state_aliased:
        pos += 1
    y_out_ref = refs[pos]
    pos += 1
    state_out_ref = None
    if emit_state:
        state_out_ref = refs[pos]
        pos += 1
    st_ref = refs[pos]
    yT_ref = refs[pos + 1] if finish else y_out_ref

    c = pl.program_id(1)

    @pl.when(c == 0)
    def _():
        if has_h0:
            st_ref[...] = h0_ref[0, 0, 0]
        else:
            st_ref[...] = jnp.zeros_like(st_ref)

    offsets = [k * SCAN_CHUNK for k in range(SCAN_STEP // SCAN_CHUNK)]
    if reverse:
        offsets.reverse()
    decays = [_chunk_decays(dt_ref, alog_ref, off, reverse) for off in offsets]
    for g in range(SSM_GROUPS):
        rows = slice(g * GROUP_ROWS, (g + 1) * GROUP_ROWS)
        st_g = st_ref[rows, :]
        for off, dec in zip(offsets, decays):
            st_g = _scan_group(dec, xT_ref, bm_ref, cm_ref, yT_ref, st_g, off, g, reverse)
        st_ref[rows, :] = st_g

    if emit_state:
        @pl.when(c == n_steps - 1)
        def _():
            state_out_ref[0, 0, 0] = st_ref[...]

    if finish:
        yT = yT_ref[...] + ybT_ref[...] + dexp_ref[...] * xT_ref[...]
        y = yT.T * _silu(z_ref[...])
        y = _rms(y, ng_ref[...])
        out = jnp.dot(y.astype(BF16), wout_ref[...], preferred_element_type=F32)
        _, _, gt = _mod_slices(mod_ref[0], 0)
        y_out_ref[...] = x_ref[...] + gt * _rms(out, g_ref[1:2])


def _ssd_scan(xT, bm, cm, dt, p, layer, h0, fin, state, *, batch, reverse):
    q = SCAN_STEP
    n = xT.shape[0] * q
    direction = 1 if reverse else 0
    emit_state = state is not None
    nc = n // batch // q
    gn = SSM_GROUPS * SSM_STATE
    finish = fin is not None

    def tok(b, c):
        return b * nc + ((nc - 1 - c) if reverse else c)

    def row_blk(b, c):
        return (tok(b, c), 0)

    t_spec = pl.BlockSpec((None, SSM_D_INNER, q), lambda b, c: (tok(b, c), 0, 0))
    in_specs = [t_spec,
                pl.BlockSpec((q, gn), row_blk),
                pl.BlockSpec((q, gn), row_blk),
                pl.BlockSpec((q, LANES), row_blk),
                _layer_spec(p["a_log"], layer)]
    args = [xT, bm, cm, dt, p["a_log"]]
    if h0 is not None:
        in_specs.append(pl.BlockSpec((1, 1, 1, SSM_D_INNER, SSM_STATE),
                                     lambda b, c: (b, layer, direction, 0, 0)))
        args.append(h0)
    if finish:
        z, ybT, x, mod, g = fin
        n_mod = mod.shape[0]
        in_specs += [pl.BlockSpec((q, SSM_D_INNER), row_blk),
                     t_spec,
                     pl.BlockSpec((q, D_MODEL), row_blk),
                     pl.BlockSpec((1, 1, 6 * D_MODEL),
                                  (lambda b, c: (b, 0, 0)) if n_mod > 1 else (lambda b, c: (0, 0, 0))),
                     _const_spec(g.shape), _layer_spec(p["d_exp"], layer),
                     _layer_spec(p["norm_g"], layer), _layer_spec(p["w_out"], layer)]
        args += [z, ybT, x, mod, g, p["d_exp"], p["norm_g"], p["w_out"]]
        out_specs = [pl.BlockSpec((q, D_MODEL), row_blk)]
        out_shape = [jax.ShapeDtypeStruct((n, D_MODEL), F32)]
    else:
        out_specs = [t_spec]
        out_shape = [jax.ShapeDtypeStruct((n // q, SSM_D_INNER, q), F32)]
    aliases = {}
    if emit_state:
        buf, n_layers = state
        out_specs.append(pl.BlockSpec((1, 1, 1, SSM_D_INNER, SSM_STATE),
                                      lambda b, c: (b, layer, direction, 0, 0)))
        out_shape.append(jax.ShapeDtypeStruct((batch, n_layers, 2, SSM_D_INNER, SSM_STATE), F32))
        if buf is not None:
            in_specs.append(pl.BlockSpec(memory_space=pl.ANY))
            args.append(buf)
            aliases = {len(args) - 1: 1}
    scratch = [pltpu.VMEM((SSM_D_INNER, SSM_STATE), F32)]
    if finish:
        scratch.append(pltpu.VMEM((SSM_D_INNER, q), F32))
    outs = pl.pallas_call(
        functools.partial(_ssd_scan_kernel, reverse=reverse, has_h0=h0 is not None, finish=finish,
                          emit_state=emit_state, state_aliased=bool(aliases), n_steps=nc),
        grid=(batch, nc),
        in_specs=in_specs, out_specs=out_specs, out_shape=out_shape,
        scratch_shapes=scratch,
        input_output_aliases=aliases,
        compiler_params=_cparams(2),
        name="ssd_scan_bwd" if reverse else "ssd_scan_fwd",
    )(*args)
    return outs if emit_state else (outs[0], None)


def _ssd_layer(x, mod, g, p, layer, h0, state, batch, seq_len):
    z, xT, bm, cm, dt = _ssd_in(x, mod, g, p, layer, seq_len)
    ybT, buf = _ssd_scan(xT, bm, cm, dt, p, layer, h0, None, state, batch=batch, reverse=True)
    if state is not None:
        state = (buf, state[1])
    x_new, buf = _ssd_scan(xT, bm, cm, dt, p, layer, h0, (z, ybT, x, mod, g), state,
                           batch=batch, reverse=False)
    return x_new, buf


def _qkv_kernel(*refs, rotary):
    kh_ref = None
    if rotary:
        x_ref, mod_ref, g_ref, w_ref, cos_ref, sin_ref, q_ref, k_ref, v_ref = refs
    else:
        x_ref, mod_ref, g_ref, w_ref, q_ref, k_ref, v_ref, kh_ref = refs
    d = D_MODEL
    sh, sc, _ = _mod_slices(mod_ref[0], 0)
    h = (_rms(x_ref[...], g_ref[0:1]) * (1.0 + sc) + sh).astype(BF16)
    if rotary:
        lane = lax.broadcasted_iota(jnp.int32, (x_ref.shape[0], LANES), 1)
        first_half = (lane & (DIFF_HEAD_DIM - 1)) < DIFF_HEAD_DIM // 2
        cos = cos_ref[...]
        sin = sin_ref[...]

    def rot(a):
        if not rotary:
            return a
        partner = jnp.where(first_half, pltpu.roll(a, LANES - DIFF_HEAD_DIM // 2, 1),
                            pltpu.roll(a, DIFF_HEAD_DIM // 2, 1))
        return a * cos + partner * sin

    scale = DIFF_HEAD_DIM ** -0.5 * math.log2(math.e)
    for j in range(d // LANES):
        cs = slice(j * LANES, (j + 1) * LANES)
        qj = jnp.dot(h, w_ref[:, j * LANES:(j + 1) * LANES], preferred_element_type=F32)
        kj = jnp.dot(h, w_ref[:, d + j * LANES:d + (j + 1) * LANES], preferred_element_type=F32)
        q_ref[:, cs] = (rot(qj) * scale).astype(q_ref.dtype)
        kj = rot(kj)
        k_ref[:, cs] = kj.astype(k_ref.dtype)
        if kh_ref is not None:
            for i in range(LANES // DIFF_HEAD_DIM):
                kh_ref[:, j * (LANES // DIFF_HEAD_DIM) + i, :] = (
                    kj[:, i * DIFF_HEAD_DIM:(i + 1) * DIFF_HEAD_DIM])
    v_ref[...] = jnp.dot(h, w_ref[:, 2 * d:3 * d], preferred_element_type=F32).astype(v_ref.dtype)


def _qkv(x, mod, g, w_qkv, rot_tables, seq_len):
    n = x.shape[0]
    t = ROW_TILE
    rotary = rot_tables is not None
    in_specs = [pl.BlockSpec((t, D_MODEL), lambda i: (i, 0)), _mod_spec(mod.shape[0], t, seq_len),
                _const_spec(g.shape), _const_spec(w_qkv.shape)]
    args = [x, mod, g, w_qkv]
    if rotary:
        per_seq = seq_len // t
        in_specs += [pl.BlockSpec((t, LANES), lambda i: (i % per_seq, 0))] * 2
        args += list(rot_tables)
    row_spec = pl.BlockSpec((t, D_MODEL), lambda i: (i, 0))
    out_specs = [row_spec, row_spec, row_spec]
    out_shape = [jax.ShapeDtypeStruct((n, D_MODEL), BF16),
                 jax.ShapeDtypeStruct((n, D_MODEL), BF16),
                 jax.ShapeDtypeStruct((n, D_MODEL), BF16 if rotary else F32)]
    if not rotary:
        out_specs.append(pl.BlockSpec((t, 2 * DIFF_HEADS, DIFF_HEAD_DIM), lambda i: (i, 0, 0)))
        out_shape.append(jax.ShapeDtypeStruct((n, 2 * DIFF_HEADS, DIFF_HEAD_DIM), F32))
    return pl.pallas_call(
        functools.partial(_qkv_kernel, rotary=rotary),
        grid=(n // t,),
        in_specs=in_specs,
        out_specs=out_specs,
        out_shape=out_shape,
        compiler_params=_cparams(1),
        name="diff_qkv",
    )(*args)


def _attn_kernel(*refs, has_cache, lam_init):
    if has_cache:
        (q_ref, k_ref, v_ref, ck_ref, cv_ref, x_ref, mod_ref, g_ref, lp_ref, sg_ref, wo_ref,
         o_ref, oall_ref) = refs
    else:
        q_ref, k_ref, v_ref, x_ref, mod_ref, g_ref, lp_ref, sg_ref, wo_ref, o_ref, oall_ref = refs
    lp = lp_ref[...]
    lam = (jnp.exp(jnp.sum(lp[0:1] * lp[1:2], axis=-1, keepdims=True))
           - jnp.exp(jnp.sum(lp[2:3] * lp[3:4], axis=-1, keepdims=True)) + lam_init)
    dn = (((1,), (1,)), ((), ()))
    hd = DIFF_HEAD_DIM

    def scores(hh):
        hs = slice(hh * hd, (hh + 1) * hd)
        qh = q_ref[:, hs]
        s_new = lax.dot_general(qh, k_ref[:, hs].astype(BF16), dn, preferred_element_type=F32)
        s_old = None
        if has_cache:
            s_old = lax.dot_general(qh, ck_ref[0, :, hs].astype(BF16), dn, preferred_element_type=F32)
        return s_old, s_new

    def probs(s_old, s_new):
        m = jnp.max(s_new, axis=-1, keepdims=True)
        if has_cache:
            m = jnp.maximum(m, jnp.max(s_old, axis=-1, keepdims=True))
            p_old = jnp.exp2(s_old - m)
        p_new = jnp.exp2(s_new - m)
        l = jnp.sum(p_new, axis=-1, keepdims=True)
        if has_cache:
            l = l + jnp.sum(p_old, axis=-1, keepdims=True)
            return p_old, p_new, l
        return None, p_new, l

    def diff_weights(pair_scores):
        po0, pn0, l0 = probs(*pair_scores[0])
        po1, pn1, l1 = probs(*pair_scores[1])
        ratio = lam * l0 / l1
        att_old = (po0 - po1 * ratio).astype(BF16) if has_cache else None
        return att_old, (pn0 - pn1 * ratio).astype(BF16), l0

    def weighted_values(hp, att_old, att, l0):
        vs = slice(hp * 2 * hd, (hp + 1) * 2 * hd)
        o = jnp.dot(att, v_ref[:, vs].astype(BF16), preferred_element_type=F32)
        if has_cache:
            o = o + jnp.dot(att_old, cv_ref[0, :, vs].astype(BF16), preferred_element_type=F32)
        o = o * (1.0 / l0)
        oall_ref[:, vs] = (_rms(o, sg_ref[...]) * (1.0 - lam_init)).astype(BF16)

    sc, att = {}, {}
    for step in range(DIFF_HEADS + 2):
        if step < DIFF_HEADS:
            sc[step] = (scores(2 * step), scores(2 * step + 1))
        if step >= 2:
            weighted_values(step - 2, *att.pop(step - 2))
        if 1 <= step <= DIFF_HEADS:
            att[step - 1] = diff_weights(sc.pop(step - 1))

    m_out = jnp.dot(oall_ref[...], wo_ref[...], preferred_element_type=F32)
    _, _, gt = _mod_slices(mod_ref[0], 0)
    o_ref[...] = x_ref[...] + gt * _rms(m_out, g_ref[1:2])


def _attention(q, k, v, cache, x, mod, g, lam_p, subln_g, w_o, *, batch, lam_init):
    n = x.shape[0]
    seq = n // batch
    tq = ATTN_Q_TILE
    nq = seq // tq
    has_cache = cache is not None
    kv_spec = pl.BlockSpec((seq, D_MODEL), lambda b, t: (b, 0))
    in_specs = [pl.BlockSpec((tq, D_MODEL), lambda b, t: (b * nq + t, 0)), kv_spec, kv_spec]
    args = [q, k, v]
    if has_cache:
        past = cache[0].shape[1]
        c_spec = pl.BlockSpec((1, past, D_MODEL), lambda b, t: (b, 0, 0))
        in_specs += [c_spec, c_spec]
        args += list(cache)
    n_mod = mod.shape[0]
    in_specs += [pl.BlockSpec((tq, D_MODEL), lambda b, t: (b * nq + t, 0)),
                 pl.BlockSpec((1, 1, 6 * D_MODEL),
                              (lambda b, t: (b, 0, 0)) if n_mod > 1 else (lambda b, t: (0, 0, 0))),
                 _const_spec(g.shape), _const_spec(lam_p.shape), _const_spec(subln_g.shape),
                 _const_spec(w_o.shape)]
    args += [x, mod, g, lam_p, subln_g, w_o]
    return pl.pallas_call(
        functools.partial(_attn_kernel, has_cache=has_cache, lam_init=lam_init),
        grid=(batch, nq),
        in_specs=in_specs,
        out_specs=pl.BlockSpec((tq, D_MODEL), lambda b, t: (b * nq + t, 0)),
        out_shape=jax.ShapeDtypeStruct((n, D_MODEL), F32),
        scratch_shapes=[pltpu.VMEM((tq, D_MODEL), BF16)],
        compiler_params=_cparams(2),
        name="diff_attention",
    )(*args)


def _rotary_tables(n_tokens):
    rows = n_tokens // GRID_W
    row = jnp.repeat(jnp.arange(rows, dtype=F32), GRID_W)
    col = jnp.tile(jnp.arange(GRID_W, dtype=F32), rows)
    inv = ROPE_THETA ** (-jnp.arange(ROT_PAIRS_PER_AXIS, dtype=F32) / ROT_PAIRS_PER_AXIS)
    ang = jnp.concatenate([row[:, None] * inv, col[:, None] * inv], axis=-1)
    cos, sin = jnp.cos(ang), jnp.sin(ang)
    reps = LANES // DIFF_HEAD_DIM
    return (jnp.tile(jnp.concatenate([cos, cos], axis=-1), (1, reps)),
            jnp.tile(jnp.concatenate([-sin, sin], axis=-1), (1, reps)))


def kernel(x_prompt, x_sample, state_ssm, cache_k, cache_v, c, c_ctx, w_mod, b_mod, norm_g, ssd_w_in, ssd_conv_w, ssd_conv_b, ssd_dt_bias, ssd_a_log, ssd_d, ssd_norm_g, ssd_w_out, sc_w_in, sc_conv_w, sc_w_out, da_w_qkv, da_lambda, da_subln_g, da_w_out, ffn_w_up, ffn_conv_w, ffn_w_down):
    bp, lp_, d = x_prompt.shape
    bs, ls, _ = x_sample.shape
    xp = x_prompt.reshape(bp * lp_, d)
    xs = x_sample.reshape(bs * ls, d)

    n_cond = 1 + bs
    cond = jnp.concatenate([c_ctx[None], c, jnp.zeros((HALO - n_cond, d), F32)], axis=0)
    mod_all = _modulation(cond, w_mod, b_mod)

    rot = _rotary_tables(ls)
    n_ssd_layers = (DEPTH + 2) // N_MIXERS
    pad = LANES - 2 * SSM_HEADS
    xbc_end = SSM_D_INNER + SSM_CONV_DIM
    ssd = {
        "w_in": ssd_w_in.astype(BF16),
        "w_dt": jnp.pad(ssd_w_in[:, :, xbc_end:], ((0, 0), (0, 0), (0, pad))).astype(BF16),
        "conv_w": ssd_conv_w,
        "conv_b": ssd_conv_b[:, None],
        "dt_bias": jnp.pad(ssd_dt_bias.reshape(n_ssd_layers, 1, -1), ((0, 0), (0, 0), (0, pad))),
        "a_log": jnp.pad(ssd_a_log.reshape(n_ssd_layers, 1, -1), ((0, 0), (0, 0), (0, pad))),
        "d_exp": jnp.broadcast_to(jnp.repeat(ssd_d, SSM_HEAD_DIM, axis=1)[:, :, None],
                                  (n_ssd_layers, SSM_D_INNER, SCAN_STEP)),
        "norm_g": ssd_norm_g[:, None],
        "w_out": ssd_w_out.astype(BF16),
    }
    h0 = state_ssm.reshape(bs, n_ssd_layers, 2, SSM_D_INNER, SSM_STATE)
    w_up_all, w_dn_all = ffn_w_up.astype(BF16), ffn_w_down.astype(BF16)
    ssm_buf = None
    new_k, new_v = [], []
    for l in range(DEPTH):
        j = l // N_MIXERS
        kind = l % N_MIXERS
        mod_p = mod_all[l, 0:1][:, None]
        mod_s = mod_all[l, 1:n_cond][:, None]
        g = norm_g[l]
        if kind == 0:
            xp, ssm_buf = _ssd_layer(xp, mod_p, g, ssd, j, None, (ssm_buf, n_ssd_layers), bp, lp_)
            xs, _ = _ssd_layer(xs, mod_s, g, ssd, j, h0, None, bs, ls)
        elif kind == 1:
            w_in, w_out = sc_w_in[j].astype(BF16), sc_w_out[j].astype(BF16)
            xp = _sconv(xp, mod_p, g, w_in, sc_conv_w[j], w_out, lp_)
            xs = _sconv(xs, mod_s, g, w_in, sc_conv_w[j], w_out, ls)
        else:
            lam_init = 0.8 - 0.6 * math.exp(-0.3 * l)
            w_qkv, w_o = da_w_qkv[j].astype(BF16), da_w_out[j].astype(BF16)
            sg = da_subln_g[j][None]
            qp, kp, vp, kp_heads = _qkv(xp, mod_p, g, w_qkv, None, lp_)
            xp = _attention(qp, kp, vp, None, xp, mod_p, g, da_lambda[j], sg, w_o,
                            batch=bp, lam_init=lam_init)
            new_k.append(kp_heads.reshape(bp, lp_, 2 * DIFF_HEADS, DIFF_HEAD_DIM))
            new_v.append(vp.reshape(bp, lp_, DIFF_HEADS, 2 * DIFF_HEAD_DIM))
            qs, ks, vs = _qkv(xs, mod_s, g, w_qkv, rot, ls)
            cache = (cache_k[:, j].reshape(bs, -1, d), cache_v[:, j].reshape(bs, -1, d))
            xs = _attention(qs, ks, vs, cache, xs, mod_s, g, da_lambda[j], sg, w_o,
                            batch=bs, lam_init=lam_init)
        xp = _ffn(xp, mod_p, g, l, w_up_all, ffn_conv_w, w_dn_all, lp_)
        xs = _ffn(xs, mod_s, g, l, w_up_all, ffn_conv_w, w_dn_all, ls)

    new_ssm = ssm_buf.reshape(bp, n_ssd_layers, 2, SSM_HEADS, SSM_HEAD_DIM, SSM_STATE)
    return (xp.reshape(bp, lp_, d), xs.reshape(bs, ls, d), new_ssm,
            jnp.stack(new_k, axis=1), jnp.stack(new_v, axis=1))
```

```python
import functools
import math

import jax
import jax.numpy as jnp
from jax import lax
from jax.experimental import pallas as pl
from jax.experimental.pallas import tpu as pltpu

D_MODEL = 1024
DEPTH = 4
GRID_W = 64
N_MIXERS = 3
SSM_D_INNER = 2 * D_MODEL
SSM_HEAD_DIM = 64
SSM_HEADS = SSM_D_INNER // SSM_HEAD_DIM
SSM_GROUPS = 8
SSM_STATE = 128
SSM_CONV_DIM = SSM_D_INNER + 2 * SSM_GROUPS * SSM_STATE
HEADS_PER_GROUP = SSM_HEADS // SSM_GROUPS
GROUP_ROWS = HEADS_PER_GROUP * SSM_HEAD_DIM
DIFF_HEAD_DIM = 64
DIFF_HEADS = D_MODEL // (2 * DIFF_HEAD_DIM)
ROPE_THETA = 10000.0
ROT_PAIRS_PER_AXIS = DIFF_HEAD_DIM // 4
FFN_DIM = 2816
NORM_EPS = 1e-6

LANES = 128
HALO = 8
SCAN_CHUNK = 128
SCAN_STEP = 256
ROW_TILE = 512
ATTN_Q_TILE = 256
COL_CHUNK = 256
FFN_DOWN_GROUP = 4
VMEM_LIMIT = 56 * 1024 * 1024

F32 = jnp.float32
BF16 = jnp.bfloat16


def _cparams(n_axes):
    return pltpu.CompilerParams(dimension_semantics=("arbitrary",) * n_axes,
                                vmem_limit_bytes=VMEM_LIMIT)


def _rms(x, g):
    ms = jnp.mean(x * x, axis=-1, keepdims=True)
    return x * lax.rsqrt(ms + NORM_EPS) * g


def _silu(x):
    return x * jax.nn.sigmoid(x)


def _mod_slices(mod, first):
    d = D_MODEL
    return (mod[:, first * d:(first + 1) * d], mod[:, (first + 1) * d:(first + 2) * d],
            mod[:, (first + 2) * d:(first + 3) * d])


def _ext_rows(tile_rows, seq_len):
    return tile_rows if seq_len <= tile_rows else tile_rows + 2 * HALO


def _main_rows(tile_rows, seq_len):
    return slice(0, tile_rows) if seq_len <= tile_rows else slice(HALO, HALO + tile_rows)


def _ext_norm(xp_ref, x_ref, xn_ref, g, sc, sh, seq_len):
    t = x_ref.shape[0]
    if seq_len <= t:
        return (_rms(x_ref[...], g) * (1.0 + sc) + sh).astype(BF16)
    tiles_per_seq = seq_len // t
    k = pl.program_id(0) % tiles_per_seq
    keep = (jnp.where(k == 0, 0.0, 1.0), None, jnp.where(k == tiles_per_seq - 1, 0.0, 1.0))
    pieces = []
    for r, kp in zip((xp_ref, x_ref, xn_ref), keep):
        h = _rms(r[...], g) * (1.0 + sc) + sh
        pieces.append(h if kp is None else h * kp)
    return jnp.concatenate(pieces, axis=0).astype(BF16)


def _conv3(u, w3, tile_rows, seq_len):
    n = u.shape[0]
    up = pltpu.roll(u, 1, 0)
    un = pltpu.roll(u, n - 1, 0)
    if n == tile_rows:
        pos = lax.broadcasted_iota(jnp.int32, u.shape, 0) & (seq_len - 1)
        up = jnp.where(pos != 0, up, 0.0)
        un = jnp.where(pos != seq_len - 1, un, 0.0)
        return up * w3[0:1] + u * w3[1:2] + un * w3[2:3]
    rows = slice(HALO, HALO + tile_rows)
    return up[rows] * w3[0:1] + u[rows] * w3[1:2] + un[rows] * w3[2:3]


def _halo_specs(n_rows, tile_rows, width):
    per = tile_rows // HALO
    last = n_rows // HALO - 1
    return [
        pl.BlockSpec((HALO, width), lambda i: (jnp.maximum(i * per - 1, 0), 0)),
        pl.BlockSpec((tile_rows, width), lambda i: (i, 0)),
        pl.BlockSpec((HALO, width), lambda i: (jnp.minimum((i + 1) * per, last), 0)),
    ]


def _mod_spec(n_mod, tile_rows, seq_len):
    if n_mod == 1:
        return pl.BlockSpec((1, 1, 6 * D_MODEL), lambda i: (0, 0, 0))
    return pl.BlockSpec((1, 1, 6 * D_MODEL), lambda i: ((i * tile_rows) // seq_len, 0, 0))


def _const_spec(shape):
    nd = len(shape)
    return pl.BlockSpec(shape, lambda *_: (0,) * nd)


def _layer_spec(stacked, layer):
    shape = stacked.shape[1:]
    idx = (layer,) + (0,) * len(shape)
    return pl.BlockSpec((None,) + shape, lambda *_: idx, pipeline_mode=pl.Buffered(1))


def _mod_kernel(cond_ref, w_ref, b_ref, o_ref):
    a = _silu(cond_ref[...]).astype(BF16)
    o_ref[0] = jnp.dot(a, w_ref[0].astype(BF16), preferred_element_type=F32) + b_ref[0]


def _modulation(cond, w_mod, b_mod):
    n_cols = 6 * D_MODEL
    blk = n_cols // 4
    return pl.pallas_call(
        _mod_kernel,
        grid=(DEPTH, n_cols // blk),
        in_specs=[_const_spec(cond.shape),
                  pl.BlockSpec((1, D_MODEL, blk), lambda l, j: (l, 0, j)),
                  pl.BlockSpec((1, 1, blk), lambda l, j: (l, 0, j))],
        out_specs=pl.BlockSpec((1, cond.shape[0], blk), lambda l, j: (l, 0, j)),
        out_shape=jax.ShapeDtypeStruct((DEPTH, cond.shape[0], n_cols), F32),
        compiler_params=_cparams(2),
        name="modulation",
    )(cond, w_mod, b_mod.reshape(DEPTH, 1, n_cols))


def _ffn_kernel(xp_ref, x_ref, xn_ref, mod_ref, g_ref, wup_ref, cw_ref, wdn_ref, o_ref,
                hext_ref, act_ref, *, seq_len):
    t = x_ref.shape[0]
    sh, sc, gt = _mod_slices(mod_ref[0], 3)
    hext_ref[...] = _ext_norm(xp_ref, x_ref, xn_ref, g_ref[2:3], sc, sh, seq_len)
    n_chunks = FFN_DIM // COL_CHUNK
    acc = None
    for j0 in range(0, n_chunks, FFN_DOWN_GROUP):
        n_j = min(FFN_DOWN_GROUP, n_chunks - j0)
        for jj in range(n_j):
            cg = (j0 + jj) * COL_CHUNK
            cv = FFN_DIM + cg
            h = hext_ref[...]
            ug = jnp.dot(h, wup_ref[:, cg:cg + COL_CHUNK], preferred_element_type=F32)
            uv = jnp.dot(h, wup_ref[:, cv:cv + COL_CHUNK], preferred_element_type=F32)
            gate = _conv3(ug, cw_ref[:, cg:cg + COL_CHUNK], t, seq_len)
            val = _conv3(uv, cw_ref[:, cv:cv + COL_CHUNK], t, seq_len)
            act_ref[:, jj * COL_CHUNK:(jj + 1) * COL_CHUNK] = (_silu(gate) * val).astype(BF16)
        k0, kn = j0 * COL_CHUNK, n_j * COL_CHUNK
        part = jnp.dot(act_ref[:, :kn], wdn_ref[k0:k0 + kn, :], preferred_element_type=F32)
        acc = part if acc is None else acc + part
    o_ref[...] = x_ref[...] + gt * _rms(acc, g_ref[3:4])


def _ffn(x, mod, g, layer, w_up, conv_w, w_down, seq_len):
    n = x.shape[0]
    t = ROW_TILE
    return pl.pallas_call(
        functools.partial(_ffn_kernel, seq_len=seq_len),
        grid=(n // t,),
        in_specs=_halo_specs(n, t, D_MODEL) + [
            _mod_spec(mod.shape[0], t, seq_len), _const_spec(g.shape), _layer_spec(w_up, layer),
            _layer_spec(conv_w, layer), _layer_spec(w_down, layer)],
        out_specs=pl.BlockSpec((t, D_MODEL), lambda i: (i, 0)),
        out_shape=jax.ShapeDtypeStruct((n, D_MODEL), F32),
        scratch_shapes=[pltpu.VMEM((_ext_rows(t, seq_len), D_MODEL), BF16),
                        pltpu.VMEM((t, FFN_DOWN_GROUP * COL_CHUNK), BF16)],
        compiler_params=_cparams(1),
        name="conv_ffn",
    )(x, x, x, mod, g, w_up, conv_w, w_down)


def _sconv_kernel(xp_ref, x_ref, xn_ref, mod_ref, g_ref, win_ref, cw_ref, wout_ref, o_ref,
                  hext_ref, act_ref, *, seq_len):
    t = x_ref.shape[0]
    d = D_MODEL
    sh, sc, gt = _mod_slices(mod_ref[0], 0)
    hext_ref[...] = _ext_norm(xp_ref, x_ref, xn_ref, g_ref[0:1], sc, sh, seq_len)
    for j in range(d // COL_CHUNK):
        c0 = j * COL_CHUNK
        bg = jnp.dot(hext_ref[_main_rows(t, seq_len), :], win_ref[:, c0:c0 + COL_CHUNK],
                     preferred_element_type=F32)
        h = hext_ref[...]
        cg = jnp.dot(h, win_ref[:, d + c0:d + c0 + COL_CHUNK], preferred_element_type=F32)
        u = jnp.dot(h, win_ref[:, 2 * d + c0:2 * d + c0 + COL_CHUNK], preferred_element_type=F32)
        conv = _conv3(cg * u, cw_ref[:, c0:c0 + COL_CHUNK], t, seq_len)
        act_ref[:, c0:c0 + COL_CHUNK] = (bg * conv).astype(BF16)
    m = jnp.dot(act_ref[...], wout_ref[...], preferred_element_type=F32)
    o_ref[...] = x_ref[...] + gt * _rms(m, g_ref[1:2])


def _sconv(x, mod, g, w_in, conv_w, w_out, seq_len):
    n = x.shape[0]
    t = ROW_TILE
    return pl.pallas_call(
        functools.partial(_sconv_kernel, seq_len=seq_len),
        grid=(n // t,),
        in_specs=_halo_specs(n, t, D_MODEL) + [
            _mod_spec(mod.shape[0], t, seq_len), _const_spec(g.shape), _const_spec(w_in.shape),
            _const_spec(conv_w.shape), _const_spec(w_out.shape)],
        out_specs=pl.BlockSpec((t, D_MODEL), lambda i: (i, 0)),
        out_shape=jax.ShapeDtypeStruct((n, D_MODEL), F32),
        scratch_shapes=[pltpu.VMEM((_ext_rows(t, seq_len), D_MODEL), BF16),
                        pltpu.VMEM((t, D_MODEL), BF16)],
        compiler_params=_cparams(1),
        name="short_conv_mixer",
    )(x, x, x, mod, g, w_in, conv_w, w_out)


def _ssd_in_kernel(xp_ref, x_ref, xn_ref, mod_ref, g_ref, win_ref, wdt_ref, cw_ref, cb_ref, dtb_ref,
                   z_ref, xT_ref, bm_ref, cmT_ref, dt_ref, hext_ref, *, seq_len):
    t = x_ref.shape[0]
    di = SSM_D_INNER
    gn = SSM_GROUPS * SSM_STATE
    sh, sc, _ = _mod_slices(mod_ref[0], 0)
    hext_ref[...] = _ext_norm(xp_ref, x_ref, xn_ref, g_ref[0:1], sc, sh, seq_len)

    for j in range(di // COL_CHUNK):
        c0 = j * COL_CHUNK
        z_ref[:, c0:c0 + COL_CHUNK] = jnp.dot(hext_ref[_main_rows(t, seq_len), :],
                                              win_ref[:, c0:c0 + COL_CHUNK], preferred_element_type=F32)

    for j in range(SSM_CONV_DIM // COL_CHUNK):
        c0 = j * COL_CHUNK
        u = jnp.dot(hext_ref[...], win_ref[:, di + c0:di + c0 + COL_CHUNK], preferred_element_type=F32)
        act = _silu(_conv3(u, cw_ref[:, c0:c0 + COL_CHUNK], t, seq_len) + cb_ref[:, c0:c0 + COL_CHUNK])
        if c0 < di:
            xT_ref[c0:c0 + COL_CHUNK, :] = act.T
        elif c0 < di + gn:
            bm_ref[:, c0 - di:c0 - di + COL_CHUNK] = act.astype(BF16)
        else:
            cmT_ref[c0 - di - gn:c0 - di - gn + COL_CHUNK, :] = act.T.astype(BF16)

    raw = jnp.dot(hext_ref[_main_rows(t, seq_len), :], wdt_ref[...],
                  preferred_element_type=F32) + dtb_ref[...]
    dt_ref[...] = jnp.maximum(raw, 0.0) + jnp.log1p(jnp.exp(-jnp.abs(raw)))


def _ssd_in(x, mod, g, p, layer, seq_len):
    n = x.shape[0]
    t = SCAN_STEP
    gn = SSM_GROUPS * SSM_STATE
    weights = [p[k] for k in ("w_in", "w_dt", "conv_w", "conv_b", "dt_bias")]
    return pl.pallas_call(
        functools.partial(_ssd_in_kernel, seq_len=seq_len),
        grid=(n // t,),
        in_specs=_halo_specs(n, t, D_MODEL) + [
            _mod_spec(mod.shape[0], t, seq_len), _const_spec(g.shape)]
        + [_layer_spec(w, layer) for w in weights],
        out_specs=[pl.BlockSpec((t, SSM_D_INNER), lambda i: (i, 0)),
                   pl.BlockSpec((None, SSM_D_INNER, t), lambda i: (i, 0, 0)),
                   pl.BlockSpec((t, gn), lambda i: (i, 0)),
                   pl.BlockSpec((None, gn, t), lambda i: (i, 0, 0)),
                   pl.BlockSpec((t, LANES), lambda i: (i, 0))],
        out_shape=[jax.ShapeDtypeStruct((n, SSM_D_INNER), F32),
                   jax.ShapeDtypeStruct((n // t, SSM_D_INNER, t), F32),
                   jax.ShapeDtypeStruct((n, gn), BF16),
                   jax.ShapeDtypeStruct((n // t, gn, t), BF16),
                   jax.ShapeDtypeStruct((n, LANES), F32)],
        scratch_shapes=[pltpu.VMEM((_ext_rows(t, seq_len), D_MODEL), BF16)],
        compiler_params=_cparams(1),
        name="ssd_in_proj",
    )(x, x, x, mod, g, *weights)


def _split3(v):
    hi = v.astype(BF16)
    r1 = v - hi.astype(F32)
    mid = r1.astype(BF16)
    lo = (r1 - mid.astype(F32)).astype(BF16)
    return hi, mid, lo


def _chunk_decays(dt_ref, alog_ref, off, reverse):
    q = SCAN_CHUNK
    dt = dt_ref[off:off + q, :]
    da = dt * (-jnp.exp(alog_ref[...]))
    r_i = lax.broadcasted_iota(jnp.int32, (q, q), 0)
    c_i = lax.broadcasted_iota(jnp.int32, (q, q), 1)
    tri = jnp.where((c_i >= r_i) if reverse else (c_i <= r_i), 1.0, 0.0).astype(BF16)
    acs = sum(jnp.dot(tri, part, preferred_element_type=F32) for part in _split3(da))
    mask = (r_i >= c_i) if reverse else (r_i <= c_i)
    acsT = acs.T
    dtT = dt.T
    src = acs - jnp.log(dt)
    last = 0 if reverse else q - 1
    totT = acsT[:, last:last + 1]
    wT = jnp.exp(totT - acsT) * dtT
    eaT = jnp.exp(acsT)
    etot = jnp.broadcast_to(jnp.exp(totT), (LANES, SSM_STATE))
    return mask, acsT, src, wT, eaT, etot


def _scan_group(decays, xT_ref, bm_ref, cmT_ref, yT_ref, st_g, off, g, reverse):
    mask, acsT, src, wT, eaT, etot = decays
    tok = slice(off, off + SCAN_CHUNK)
    ns = slice(g * SSM_STATE, (g + 1) * SSM_STATE)
    bm_g = bm_ref[tok, ns]
    cmT_g = cmT_ref[ns, tok]
    cbT = jnp.dot(bm_g, cmT_g, preferred_element_type=F32)
    y_off = jnp.dot(st_g.astype(BF16), cmT_g, preferred_element_type=F32)
    xT_g = xT_ref[g * GROUP_ROWS:(g + 1) * GROUP_ROWS, tok]
    head0 = (SSM_HEADS if reverse else 0) + g * HEADS_PER_GROUP
    xw, scale = [], []
    for r in range(HEADS_PER_GROUP):
        col = head0 + r
        hr = slice(r * SSM_HEAD_DIM, (r + 1) * SSM_HEAD_DIM)
        mT = jnp.where(mask, cbT * jnp.exp(acsT[col:col + 1, :] - src[:, col:col + 1]), 0.0)
        xh = xT_g[hr, :]
        yh = jnp.dot(xh.astype(BF16), mT.astype(BF16), preferred_element_type=F32)
        yh = yh + y_off[hr, :] * eaT[col:col + 1, :]
        yT_ref[g * GROUP_ROWS + r * SSM_HEAD_DIM:g * GROUP_ROWS + (r + 1) * SSM_HEAD_DIM, tok] = yh
        xw.append((xh * wT[col:col + 1, :]).astype(BF16))
        scale.append(jnp.broadcast_to(etot[col:col + 1, :], (SSM_HEAD_DIM, SSM_STATE)))
    d_state = jnp.dot(jnp.concatenate(xw, axis=0), bm_g, preferred_element_type=F32)
    return st_g * jnp.concatenate(scale, axis=0) + d_state


def _ssd_scan_kernel(*refs, reverse, has_h0, finish, emit_state, state_aliased, n_steps):
    refs = list(refs)
    xT_ref, bm_ref, cmT_ref, dt_ref, alog_ref = refs[:5]
    pos = 5
    h0_ref = None
    if has_h0:
        h0_ref = refs[pos]
        pos += 1
    if finish:
        (z_ref, ybT_ref, x_ref, mod_ref, g_ref, dexp_ref, ng_ref, wout_ref) = refs[pos:pos + 8]
        pos += 8
    if state_aliased:
        pos += 1
    y_out_ref = refs[pos]
    pos += 1
    state_out_ref = None
    if emit_state:
        state_out_ref = refs[pos]
        pos += 1
    st_ref = refs[pos]
    yT_ref = refs[pos + 1] if finish else y_out_ref

    c = pl.program_id(1)

    @pl.when(c == 0)
    def _():
        if has_h0:
            st_ref[...] = h0_ref[0, 0, 0]
        else:
            st_ref[...] = jnp.zeros_like(st_ref)

    offsets = [k * SCAN_CHUNK for k in range(SCAN_STEP // SCAN_CHUNK)]
    if reverse:
        offsets.reverse()
    decays = [_chunk_decays(dt_ref, alog_ref, off, reverse) for off in offsets]
    for g in range(SSM_GROUPS):
        rows = slice(g * GROUP_ROWS, (g + 1) * GROUP_ROWS)
        st_g = st_ref[rows, :]
        for off, dec in zip(offsets, decays):
            st_g = _scan_group(dec, xT_ref, bm_ref, cmT_ref, yT_ref, st_g, off, g, reverse)
        st_ref[rows, :] = st_g

    if emit_state:
        @pl.when(c == n_steps - 1)
        def _():
            state_out_ref[0, 0, 0] = st_ref[...]

    if finish:
        yT = yT_ref[...] + ybT_ref[...] + dexp_ref[...] * xT_ref[...]
        y = yT.T * _silu(z_ref[...])
        y = _rms(y, ng_ref[...])
        out = jnp.dot(y.astype(BF16), wout_ref[...], preferred_element_type=F32)
        _, _, gt = _mod_slices(mod_ref[0], 0)
        y_out_ref[...] = x_ref[...] + gt * _rms(out, g_ref[1:2])


def _ssd_scan(xT, bm, cmT, dt, p, layer, h0, fin, state, *, batch, reverse):
    q = SCAN_STEP
    n = xT.shape[0] * q
    direction = 1 if reverse else 0
    emit_state = state is not None
    nc = n // batch // q
    gn = SSM_GROUPS * SSM_STATE
    finish = fin is not None

    def tok(b, c):
        return b * nc + ((nc - 1 - c) if reverse else c)

    def row_blk(b, c):
        return (tok(b, c), 0)

    t_spec = pl.BlockSpec((None, SSM_D_INNER, q), lambda b, c: (tok(b, c), 0, 0))
    in_specs = [t_spec,
                pl.BlockSpec((q, gn), row_blk),
                pl.BlockSpec((None, gn, q), lambda b, c: (tok(b, c), 0, 0)),
                pl.BlockSpec((q, LANES), row_blk),
                _layer_spec(p["a_log"], layer)]
    args = [xT, bm, cmT, dt, p["a_log"]]
    if h0 is not None:
        in_specs.append(pl.BlockSpec((1, 1, 1, SSM_D_INNER, SSM_STATE),
                                     lambda b, c: (b, layer, direction, 0, 0)))
        args.append(h0)
    if finish:
        z, ybT, x, mod, g = fin
        n_mod = mod.shape[0]
        in_specs += [pl.BlockSpec((q, SSM_D_INNER), row_blk),
                     t_spec,
                     pl.BlockSpec((q, D_MODEL), row_blk),
                     pl.BlockSpec((1, 1, 6 * D_MODEL),
                                  (lambda b, c: (b, 0, 0)) if n_mod > 1 else (lambda b, c: (0, 0, 0))),
                     _const_spec(g.shape), _layer_spec(p["d_exp"], layer),
                     _layer_spec(p["norm_g"], layer), _layer_spec(p["w_out"], layer)]
        args += [z, ybT, x, mod, g, p["d_exp"], p["norm_g"], p["w_out"]]
        out_specs = [pl.BlockSpec((q, D_MODEL), row_blk)]
        out_shape = [jax.ShapeDtypeStruct((n, D_MODEL), F32)]
    else:
        out_specs = [t_spec]
        out_shape = [jax.ShapeDtypeStruct((n // q, SSM_D_INNER, q), F32)]
    aliases = {}
    if emit_state:
        buf, n_layers = state
        out_specs.append(pl.BlockSpec((1, 1, 1, SSM_D_INNER, SSM_STATE),
                                      lambda b, c: (b, layer, direction, 0, 0)))
        out_shape.append(jax.ShapeDtypeStruct((batch, n_layers, 2, SSM_D_INNER, SSM_STATE), F32))
        if buf is not None:
            in_specs.append(pl.BlockSpec(memory_space=pl.ANY))
            args.append(buf)
            aliases = {len(args) - 1: 1}
    scratch = [pltpu.VMEM((SSM_D_INNER, SSM_STATE), F32)]
    if finish:
        scratch.append(pltpu.VMEM((SSM_D_INNER, q), F32))
    outs = pl.pallas_call(
        functools.partial(_ssd_scan_kernel, reverse=reverse, has_h0=h0 is not None, finish=finish,
                          emit_state=emit_state, state_aliased=bool(aliases), n_steps=nc),
        grid=(batch, nc),
        in_specs=in_specs, out_specs=out_specs, out_shape=out_shape,
        scratch_shapes=scratch,
        input_output_aliases=aliases,
        compiler_params=_cparams(2),
        name="ssd_scan_bwd" if reverse else "ssd_scan_fwd",
    )(*args)
    return outs if emit_state else (outs[0], None)


def _ssd_layer(x, mod, g, p, layer, h0, state, batch, seq_len):
    z, xT, bm, cmT, dt = _ssd_in(x, mod, g, p, layer, seq_len)
    ybT, buf = _ssd_scan(xT, bm, cmT, dt, p, layer, h0, None, state, batch=batch, reverse=True)
    if state is not None:
        state = (buf, state[1])
    x_new, buf = _ssd_scan(xT, bm, cmT, dt, p, layer, h0, (z, ybT, x, mod, g), state,
                           batch=batch, reverse=False)
    return x_new, buf


def _qkv_kernel(*refs, rotary):
    kh_ref = None
    if rotary:
        x_ref, mod_ref, g_ref, w_ref, cos_ref, sin_ref, q_ref, k_ref, v_ref = refs
    else:
        x_ref, mod_ref, g_ref, w_ref, q_ref, k_ref, v_ref, kh_ref = refs
    d = D_MODEL
    sh, sc, _ = _mod_slices(mod_ref[0], 0)
    h = (_rms(x_ref[...], g_ref[0:1]) * (1.0 + sc) + sh).astype(BF16)
    if rotary:
        lane = lax.broadcasted_iota(jnp.int32, (x_ref.shape[0], LANES), 1)
        first_half = (lane & (DIFF_HEAD_DIM - 1)) < DIFF_HEAD_DIM // 2
        cos = cos_ref[...]
        sin = sin_ref[...]

    def rot(a):
        if not rotary:
            return a
        partner = jnp.where(first_half, pltpu.roll(a, LANES - DIFF_HEAD_DIM // 2, 1),
                            pltpu.roll(a, DIFF_HEAD_DIM // 2, 1))
        return a * cos + partner * sin

    scale = DIFF_HEAD_DIM ** -0.5 * math.log2(math.e)
    for j in range(d // LANES):
        cs = slice(j * LANES, (j + 1) * LANES)
        qj = jnp.dot(h, w_ref[:, j * LANES:(j + 1) * LANES], preferred_element_type=F32)
        kj = jnp.dot(h, w_ref[:, d + j * LANES:d + (j + 1) * LANES], preferred_element_type=F32)
        q_ref[:, cs] = (rot(qj) * scale).astype(q_ref.dtype)
        kj = rot(kj)
        k_ref[cs, :] = kj.T.astype(k_ref.dtype)
        if kh_ref is not None:
            for i in range(LANES // DIFF_HEAD_DIM):
                kh_ref[:, j * (LANES // DIFF_HEAD_DIM) + i, :] = (
                    kj[:, i * DIFF_HEAD_DIM:(i + 1) * DIFF_HEAD_DIM])
    v_ref[...] = jnp.dot(h, w_ref[:, 2 * d:3 * d], preferred_element_type=F32).astype(v_ref.dtype)


def _qkv(x, mod, g, w_qkv, rot_tables, seq_len):
    n = x.shape[0]
    t = ROW_TILE
    rotary = rot_tables is not None
    in_specs = [pl.BlockSpec((t, D_MODEL), lambda i: (i, 0)), _mod_spec(mod.shape[0], t, seq_len),
                _const_spec(g.shape), _const_spec(w_qkv.shape)]
    args = [x, mod, g, w_qkv]
    if rotary:
        per_seq = seq_len // t
        in_specs += [pl.BlockSpec((t, LANES), lambda i: (i % per_seq, 0))] * 2
        args += list(rot_tables)
    row_spec = pl.BlockSpec((t, D_MODEL), lambda i: (i, 0))
    blk = max(seq_len, t)
    kT_spec = pl.BlockSpec((None, D_MODEL, t), lambda i: ((i * t) // blk, 0, ((i * t) % blk) // t))
    out_specs = [row_spec, kT_spec, row_spec]
    out_shape = [jax.ShapeDtypeStruct((n, D_MODEL), BF16),
                 jax.ShapeDtypeStruct((n // blk, D_MODEL, blk), BF16),
                 jax.ShapeDtypeStruct((n, D_MODEL), BF16 if rotary else F32)]
    if not rotary:
        out_specs.append(pl.BlockSpec((t, 2 * DIFF_HEADS, DIFF_HEAD_DIM), lambda i: (i, 0, 0)))
        out_shape.append(jax.ShapeDtypeStruct((n, 2 * DIFF_HEADS, DIFF_HEAD_DIM), F32))
    return pl.pallas_call(
        functools.partial(_qkv_kernel, rotary=rotary),
        grid=(n // t,),
        in_specs=in_specs,
        out_specs=out_specs,
        out_shape=out_shape,
        compiler_params=_cparams(1),
        name="diff_qkv",
    )(*args)


def _attn_kernel(*refs, has_cache, lam_init):
    if has_cache:
        (q_ref, kT_ref, v_ref, ckT_ref, cv_ref, x_ref, mod_ref, g_ref, lp_ref, sg_ref, wo_ref,
         o_ref, oall_ref) = refs
    else:
        q_ref, kT_ref, v_ref, x_ref, mod_ref, g_ref, lp_ref, sg_ref, wo_ref, o_ref, oall_ref = refs
    lp = lp_ref[...]
    lam = (jnp.exp(jnp.sum(lp[0:1] * lp[1:2], axis=-1, keepdims=True))
           - jnp.exp(jnp.sum(lp[2:3] * lp[3:4], axis=-1, keepdims=True)) + lam_init)
    hd = DIFF_HEAD_DIM

    def scores(hh):
        hs = slice(hh * hd, (hh + 1) * hd)
        qh = q_ref[:, hs]
        s_new = jnp.dot(qh, kT_ref[hs, :], preferred_element_type=F32)
        s_old = None
        if has_cache:
            s_old = jnp.dot(qh, ckT_ref[0, hs, :].astype(BF16), preferred_element_type=F32)
        return s_old, s_new

    def probs(s_old, s_new):
        m = jnp.max(s_new, axis=-1, keepdims=True)
        if has_cache:
            m = jnp.maximum(m, jnp.max(s_old, axis=-1, keepdims=True))
            p_old = jnp.exp2(s_old - m)
        p_new = jnp.exp2(s_new - m)
        l = jnp.sum(p_new, axis=-1, keepdims=True)
        if has_cache:
            l = l + jnp.sum(p_old, axis=-1, keepdims=True)
            return p_old, p_new, l
        return None, p_new, l

    def diff_weights(pair_scores):
        po0, pn0, l0 = probs(*pair_scores[0])
        po1, pn1, l1 = probs(*pair_scores[1])
        ratio = lam * l0 / l1
        att_old = (po0 - po1 * ratio).astype(BF16) if has_cache else None
        return att_old, (pn0 - pn1 * ratio).astype(BF16), l0

    def weighted_values(hp, att_old, att, l0):
        vs = slice(hp * 2 * hd, (hp + 1) * 2 * hd)
        o = jnp.dot(att, v_ref[:, vs].astype(BF16), preferred_element_type=F32)
        if has_cache:
            o = o + jnp.dot(att_old, cv_ref[0, :, vs].astype(BF16), preferred_element_type=F32)
        o = o * (1.0 / l0)
        oall_ref[:, vs] = (_rms(o, sg_ref[...]) * (1.0 - lam_init)).astype(BF16)

    sc, att = {}, {}
    for step in range(DIFF_HEADS + 2):
        if step < DIFF_HEADS:
            sc[step] = (scores(2 * step), scores(2 * step + 1))
        if step >= 2:
            weighted_values(step - 2, *att.pop(step - 2))
        if 1 <= step <= DIFF_HEADS:
            att[step - 1] = diff_weights(sc.pop(step - 1))

    m_out = jnp.dot(oall_ref[...], wo_ref[...], preferred_element_type=F32)
    _, _, gt = _mod_slices(mod_ref[0], 0)
    o_ref[...] = x_ref[...] + gt * _rms(m_out, g_ref[1:2])


def _attention(q, kT, v, cache, x, mod, g, lam_p, subln_g, w_o, *, batch, lam_init):
    n = x.shape[0]
    seq = n // batch
    tq = min(ATTN_Q_TILE, seq)
    nq = seq // tq
    has_cache = cache is not None
    blk = kT.shape[2]
    kT_spec = pl.BlockSpec((None, D_MODEL, seq), lambda b, t: ((b * seq) // blk, 0, ((b * seq) % blk) // seq))
    v_spec = pl.BlockSpec((seq, D_MODEL), lambda b, t: (b, 0))
    in_specs = [pl.BlockSpec((tq, D_MODEL), lambda b, t: (b * nq + t, 0)), kT_spec, v_spec]
    args = [q, kT, v]
    if has_cache:
        past = cache[1].shape[1]
        in_specs += [pl.BlockSpec((1, D_MODEL, past), lambda b, t: (b, 0, 0)),
                     pl.BlockSpec((1, past, D_MODEL), lambda b, t: (b, 0, 0))]
        args += list(cache)
    n_mod = mod.shape[0]
    in_specs += [pl.BlockSpec((tq, D_MODEL), lambda b, t: (b * nq + t, 0)),
                 pl.BlockSpec((1, 1, 6 * D_MODEL),
                              (lambda b, t: (b, 0, 0)) if n_mod > 1 else (lambda b, t: (0, 0, 0))),
                 _const_spec(g.shape), _const_spec(lam_p.shape), _const_spec(subln_g.shape),
                 _const_spec(w_o.shape)]
    args += [x, mod, g, lam_p, subln_g, w_o]
    return pl.pallas_call(
        functools.partial(_attn_kernel, has_cache=has_cache, lam_init=lam_init),
        grid=(batch, nq),
        in_specs=in_specs,
        out_specs=pl.BlockSpec((tq, D_MODEL), lambda b, t: (b * nq + t, 0)),
        out_shape=jax.ShapeDtypeStruct((n, D_MODEL), F32),
        scratch_shapes=[pltpu.VMEM((tq, D_MODEL), BF16)],
        compiler_params=_cparams(2),
        name="diff_attention",
    )(*args)


def _rotary_tables(n_tokens):
    rows = n_tokens // GRID_W
    row = jnp.repeat(jnp.arange(rows, dtype=F32), GRID_W)
    col = jnp.tile(jnp.arange(GRID_W, dtype=F32), rows)
    inv = ROPE_THETA ** (-jnp.arange(ROT_PAIRS_PER_AXIS, dtype=F32) / ROT_PAIRS_PER_AXIS)
    ang = jnp.concatenate([row[:, None] * inv, col[:, None] * inv], axis=-1)
    cos, sin = jnp.cos(ang), jnp.sin(ang)
    reps = LANES // DIFF_HEAD_DIM
    return (jnp.tile(jnp.concatenate([cos, cos], axis=-1), (1, reps)),
            jnp.tile(jnp.concatenate([-sin, sin], axis=-1), (1, reps)))


def kernel(x_prompt, x_sample, state_ssm, cache_k, cache_v, c, c_ctx, w_mod, b_mod, norm_g, ssd_w_in, ssd_conv_w, ssd_conv_b, ssd_dt_bias, ssd_a_log, ssd_d, ssd_norm_g, ssd_w_out, sc_w_in, sc_conv_w, sc_w_out, da_w_qkv, da_lambda, da_subln_g, da_w_out, ffn_w_up, ffn_conv_w, ffn_w_down):
    bp, lp_, d = x_prompt.shape
    bs, ls, _ = x_sample.shape
    xp = x_prompt.reshape(bp * lp_, d)
    xs = x_sample.reshape(bs * ls, d)

    n_cond = 1 + bs
    cond = jnp.concatenate([c_ctx[None], c, jnp.zeros((HALO - n_cond, d), F32)], axis=0)
    mod_all = _modulation(cond, w_mod, b_mod)

    rot = _rotary_tables(ls)
    n_ssd_layers = (DEPTH + 2) // N_MIXERS
    pad = LANES - 2 * SSM_HEADS
    xbc_end = SSM_D_INNER + SSM_CONV_DIM
    ssd = {
        "w_in": ssd_w_in.astype(BF16),
        "w_dt": jnp.pad(ssd_w_in[:, :, xbc_end:], ((0, 0), (0, 0), (0, pad))).astype(BF16),
        "conv_w": ssd_conv_w,
        "conv_b": ssd_conv_b[:, None],
        "dt_bias": jnp.pad(ssd_dt_bias.reshape(n_ssd_layers, 1, -1), ((0, 0), (0, 0), (0, pad))),
        "a_log": jnp.pad(ssd_a_log.reshape(n_ssd_layers, 1, -1), ((0, 0), (0, 0), (0, pad))),
        "d_exp": jnp.broadcast_to(jnp.repeat(ssd_d, SSM_HEAD_DIM, axis=1)[:, :, None],
                                  (n_ssd_layers, SSM_D_INNER, SCAN_STEP)),
        "norm_g": ssd_norm_g[:, None],
        "w_out": ssd_w_out.astype(BF16),
    }
    h0 = state_ssm.reshape(bs, n_ssd_layers, 2, SSM_D_INNER, SSM_STATE)
    w_up_all, w_dn_all = ffn_w_up.astype(BF16), ffn_w_down.astype(BF16)
    ssm_buf = None
    new_k, new_v = [], []
    for l in range(DEPTH):
        j = l // N_MIXERS
        kind = l % N_MIXERS
        mod_p = mod_all[l, 0:1][:, None]
        mod_s = mod_all[l, 1:n_cond][:, None]
        g = norm_g[l]
        if kind == 0:
            xp, ssm_buf = _ssd_layer(xp, mod_p, g, ssd, j, None, (ssm_buf, n_ssd_layers), bp, lp_)
            xs, _ = _ssd_layer(xs, mod_s, g, ssd, j, h0, None, bs, ls)
        elif kind == 1:
            w_in, w_out = sc_w_in[j].astype(BF16), sc_w_out[j].astype(BF16)
            xp = _sconv(xp, mod_p, g, w_in, sc_conv_w[j], w_out, lp_)
            xs = _sconv(xs, mod_s, g, w_in, sc_conv_w[j], w_out, ls)
        else:
            lam_init = 0.8 - 0.6 * math.exp(-0.3 * l)
            w_qkv, w_o = da_w_qkv[j].astype(BF16), da_w_out[j].astype(BF16)
            sg = da_subln_g[j][None]
            qp, kp, vp, kp_heads = _qkv(xp, mod_p, g, w_qkv, None, lp_)
            xp = _attention(qp, kp, vp, None, xp, mod_p, g, da_lambda[j], sg, w_o,
                            batch=bp, lam_init=lam_init)
            new_k.append(kp_heads.reshape(bp, lp_, 2 * DIFF_HEADS, DIFF_HEAD_DIM))
            new_v.append(vp.reshape(bp, lp_, DIFF_HEADS, 2 * DIFF_HEAD_DIM))
            qs, ks, vs = _qkv(xs, mod_s, g, w_qkv, rot, ls)
            cache = (jnp.transpose(cache_k[:, j], (0, 2, 3, 1)).reshape(bs, d, -1),
                     cache_v[:, j].reshape(bs, -1, d))
            xs = _attention(qs, ks, vs, cache, xs, mod_s, g, da_lambda[j], sg, w_o,
                            batch=bs, lam_init=lam_init)
        xp = _ffn(xp, mod_p, g, l, w_up_all, ffn_conv_w, w_dn_all, lp_)
        xs = _ffn(xs, mod_s, g, l, w_up_all, ffn_conv_w, w_dn_all, ls)

    new_ssm = ssm_buf.reshape(bp, n_ssd_layers, 2, SSM_HEADS, SSM_HEAD_DIM, SSM_STATE)
    return (xp.reshape(bp, lp_, d), xs.reshape(bs, ls, d), new_ssm,
            jnp.stack(new_k, axis=1), jnp.stack(new_v, axis=1))
```

```python
import functools
import math

import jax
import jax.numpy as jnp
from jax import lax
from jax.experimental import pallas as pl
from jax.experimental.pallas import tpu as pltpu

D_MODEL = 1024
DEPTH = 4
GRID_W = 64
N_MIXERS = 3
SSM_D_INNER = 2 * D_MODEL
SSM_HEAD_DIM = 64
SSM_HEADS = SSM_D_INNER // SSM_HEAD_DIM
SSM_GROUPS = 8
SSM_STATE = 128
SSM_CONV_DIM = SSM_D_INNER + 2 * SSM_GROUPS * SSM_STATE
HEADS_PER_GROUP = SSM_HEADS // SSM_GROUPS
GROUP_ROWS = HEADS_PER_GROUP * SSM_HEAD_DIM
DIFF_HEAD_DIM = 64
DIFF_HEADS = D_MODEL // (2 * DIFF_HEAD_DIM)
ROPE_THETA = 10000.0
ROT_PAIRS_PER_AXIS = DIFF_HEAD_DIM // 4
FFN_DIM = 2816
NORM_EPS = 1e-6

LANES = 128
HALO = 8
SCAN_CHUNK = 128
SCAN_STEP = 256
ROW_TILE = 512
ATTN_Q_TILE = 256
COL_CHUNK = 256
FFN_DOWN_GROUP = 4
FFN_DOWN_LAG = 3
VMEM_LIMIT = 56 * 1024 * 1024

F32 = jnp.float32
BF16 = jnp.bfloat16


def _cparams(n_axes):
    return pltpu.CompilerParams(dimension_semantics=("arbitrary",) * n_axes,
                                vmem_limit_bytes=VMEM_LIMIT)


def _rms(x, g):
    ms = jnp.mean(x * x, axis=-1, keepdims=True)
    return x * lax.rsqrt(ms + NORM_EPS) * g


def _silu(x):
    return x * jax.nn.sigmoid(x)


def _mod_slices(mod, first):
    d = D_MODEL
    return (mod[:, first * d:(first + 1) * d], mod[:, (first + 1) * d:(first + 2) * d],
            mod[:, (first + 2) * d:(first + 3) * d])


def _ext_rows(tile_rows, seq_len):
    return tile_rows if seq_len <= tile_rows else tile_rows + 2 * HALO


def _main_rows(tile_rows, seq_len):
    return slice(0, tile_rows) if seq_len <= tile_rows else slice(HALO, HALO + tile_rows)


def _ext_norm(xp_ref, x_ref, xn_ref, g, sc, sh, seq_len):
    t = x_ref.shape[0]
    if seq_len <= t:
        return (_rms(x_ref[...], g) * (1.0 + sc) + sh).astype(BF16)
    tiles_per_seq = seq_len // t
    k = pl.program_id(0) % tiles_per_seq
    keep = (jnp.where(k == 0, 0.0, 1.0), None, jnp.where(k == tiles_per_seq - 1, 0.0, 1.0))
    pieces = []
    for r, kp in zip((xp_ref, x_ref, xn_ref), keep):
        h = _rms(r[...], g) * (1.0 + sc) + sh
        pieces.append(h if kp is None else h * kp)
    return jnp.concatenate(pieces, axis=0).astype(BF16)


def _conv3(u, w3, tile_rows, seq_len):
    if u.shape[0] == tile_rows:
        return _conv_rows(u, w3, 0, tile_rows, seq_len, first_row=0)
    return _conv_rows(u, w3, HALO, tile_rows, seq_len)


def _conv_rows(u, w3, off, n_out, seq_len, first_row=None):
    n = u.shape[0]
    rows = slice(off, off + n_out)
    up = pltpu.roll(u, 1, 0)[rows]
    un = pltpu.roll(u, n - 1, 0)[rows]
    if first_row is not None:
        pos = (lax.broadcasted_iota(jnp.int32, (n_out, u.shape[1]), 0) + first_row) & (seq_len - 1)
        up = jnp.where(pos != 0, up, 0.0)
        un = jnp.where(pos != seq_len - 1, un, 0.0)
    return up * w3[0:1] + u[rows] * w3[1:2] + un * w3[2:3]


def _halo_specs(n_rows, tile_rows, width):
    per = tile_rows // HALO
    last = n_rows // HALO - 1
    return [
        pl.BlockSpec((HALO, width), lambda i: (jnp.maximum(i * per - 1, 0), 0)),
        pl.BlockSpec((tile_rows, width), lambda i: (i, 0)),
        pl.BlockSpec((HALO, width), lambda i: (jnp.minimum((i + 1) * per, last), 0)),
    ]


def _mod_spec(n_mod, tile_rows, seq_len):
    if n_mod == 1:
        return pl.BlockSpec((1, 1, 6 * D_MODEL), lambda i: (0, 0, 0))
    return pl.BlockSpec((1, 1, 6 * D_MODEL), lambda i: ((i * tile_rows) // seq_len, 0, 0))


def _const_spec(shape):
    nd = len(shape)
    return pl.BlockSpec(shape, lambda *_: (0,) * nd)


def _layer_spec(stacked, layer):
    shape = stacked.shape[1:]
    idx = (layer,) + (0,) * len(shape)
    return pl.BlockSpec((None,) + shape, lambda *_: idx, pipeline_mode=pl.Buffered(1))


def _mod_kernel(cond_ref, w_ref, b_ref, o_ref):
    a = _silu(cond_ref[...]).astype(BF16)
    o_ref[0] = jnp.dot(a, w_ref[0].astype(BF16), preferred_element_type=F32) + b_ref[0]


def _modulation(cond, w_mod, b_mod):
    n_cols = 6 * D_MODEL
    blk = n_cols // 4
    return pl.pallas_call(
        _mod_kernel,
        grid=(DEPTH, n_cols // blk),
        in_specs=[_const_spec(cond.shape),
                  pl.BlockSpec((1, D_MODEL, blk), lambda l, j: (l, 0, j)),
                  pl.BlockSpec((1, 1, blk), lambda l, j: (l, 0, j))],
        out_specs=pl.BlockSpec((1, cond.shape[0], blk), lambda l, j: (l, 0, j)),
        out_shape=jax.ShapeDtypeStruct((DEPTH, cond.shape[0], n_cols), F32),
        compiler_params=_cparams(2),
        name="modulation",
    )(cond, w_mod, b_mod.reshape(DEPTH, 1, n_cols))


def _ffn_kernel(xp_ref, x_ref, xn_ref, mod_ref, g_ref, wup_ref, cw_ref, wdn_ref, o_ref,
                hext_ref, act_ref, *, seq_len):
    t = x_ref.shape[0]
    sh, sc, gt = _mod_slices(mod_ref[0], 3)
    hext_ref[...] = _ext_norm(xp_ref, x_ref, xn_ref, g_ref[2:3], sc, sh, seq_len)
    n_chunks = FFN_DIM // COL_CHUNK

    slabs = [(c0, min(c0 + FFN_DOWN_GROUP, n_chunks - 1)) for c0 in range(0, n_chunks - 1, FFN_DOWN_GROUP)]
    issue_at = {}
    for c0, c1 in slabs:
        issue_at.setdefault(min(c1 - 1 + FFN_DOWN_LAG, n_chunks - 1), []).append((c0, c1))

    acc = None

    def down(c0, c1):
        k0, k1 = c0 * COL_CHUNK, c1 * COL_CHUNK
        part = jnp.dot(act_ref[:, k0:k1], wdn_ref[k0:k1, :], preferred_element_type=F32)
        return part if acc is None else acc + part

    for j in range(n_chunks):
        cg = j * COL_CHUNK
        cv = FFN_DIM + cg
        h = hext_ref[...]
        ug = jnp.dot(h, wup_ref[:, cg:cg + COL_CHUNK], preferred_element_type=F32)
        uv = jnp.dot(h, wup_ref[:, cv:cv + COL_CHUNK], preferred_element_type=F32)
        for c0, c1 in issue_at.get(j, []):
            acc = down(c0, c1)
        gate = _conv3(ug, cw_ref[:, cg:cg + COL_CHUNK], t, seq_len)
        val = _conv3(uv, cw_ref[:, cv:cv + COL_CHUNK], t, seq_len)
        act_ref[:, cg:cg + COL_CHUNK] = (_silu(gate) * val).astype(BF16)
    acc = down(n_chunks - 1, n_chunks)
    o_ref[...] = x_ref[...] + gt * _rms(acc, g_ref[3:4])


def _ffn(x, mod, g, layer, w_up, conv_w, w_down, seq_len):
    n = x.shape[0]
    t = ROW_TILE
    return pl.pallas_call(
        functools.partial(_ffn_kernel, seq_len=seq_len),
        grid=(n // t,),
        in_specs=_halo_specs(n, t, D_MODEL) + [
            _mod_spec(mod.shape[0], t, seq_len), _const_spec(g.shape), _layer_spec(w_up, layer),
            _layer_spec(conv_w, layer), _layer_spec(w_down, layer)],
        out_specs=pl.BlockSpec((t, D_MODEL), lambda i: (i, 0)),
        out_shape=jax.ShapeDtypeStruct((n, D_MODEL), F32),
        scratch_shapes=[pltpu.VMEM((_ext_rows(t, seq_len), D_MODEL), BF16),
                        pltpu.VMEM((t, FFN_DIM), BF16)],
        compiler_params=_cparams(1),
        name="conv_ffn",
    )(x, x, x, mod, g, w_up, conv_w, w_down)


def _sconv_kernel(xp_ref, x_ref, xn_ref, mod_ref, g_ref, win_ref, cw_ref, wout_ref, o_ref,
                  hext_ref, act_ref, *, seq_len):
    t = x_ref.shape[0]
    d = D_MODEL
    sh, sc, gt = _mod_slices(mod_ref[0], 0)
    hext_ref[...] = _ext_norm(xp_ref, x_ref, xn_ref, g_ref[0:1], sc, sh, seq_len)
    for j in range(d // COL_CHUNK):
        c0 = j * COL_CHUNK
        bg = jnp.dot(hext_ref[_main_rows(t, seq_len), :], win_ref[:, c0:c0 + COL_CHUNK],
                     preferred_element_type=F32)
        h = hext_ref[...]
        cg = jnp.dot(h, win_ref[:, d + c0:d + c0 + COL_CHUNK], preferred_element_type=F32)
        u = jnp.dot(h, win_ref[:, 2 * d + c0:2 * d + c0 + COL_CHUNK], preferred_element_type=F32)
        conv = _conv3(cg * u, cw_ref[:, c0:c0 + COL_CHUNK], t, seq_len)
        act_ref[:, c0:c0 + COL_CHUNK] = (bg * conv).astype(BF16)
    m = jnp.dot(act_ref[...], wout_ref[...], preferred_element_type=F32)
    o_ref[...] = x_ref[...] + gt * _rms(m, g_ref[1:2])


def _sconv(x, mod, g, w_in, conv_w, w_out, seq_len):
    n = x.shape[0]
    t = ROW_TILE
    return pl.pallas_call(
        functools.partial(_sconv_kernel, seq_len=seq_len),
        grid=(n // t,),
        in_specs=_halo_specs(n, t, D_MODEL) + [
            _mod_spec(mod.shape[0], t, seq_len), _const_spec(g.shape), _const_spec(w_in.shape),
            _const_spec(conv_w.shape), _const_spec(w_out.shape)],
        out_specs=pl.BlockSpec((t, D_MODEL), lambda i: (i, 0)),
        out_shape=jax.ShapeDtypeStruct((n, D_MODEL), F32),
        scratch_shapes=[pltpu.VMEM((_ext_rows(t, seq_len), D_MODEL), BF16),
                        pltpu.VMEM((t, D_MODEL), BF16)],
        compiler_params=_cparams(1),
        name="short_conv_mixer",
    )(x, x, x, mod, g, w_in, conv_w, w_out)


def _ssd_in_kernel(xp_ref, x_ref, xn_ref, mod_ref, g_ref, win_ref, wdt_ref, cw_ref, cb_ref, dtb_ref,
                   z_ref, xT_ref, bm_ref, cmT_ref, dt_ref, hext_ref, *, seq_len):
    t = x_ref.shape[0]
    di = SSM_D_INNER
    gn = SSM_GROUPS * SSM_STATE
    sh, sc, _ = _mod_slices(mod_ref[0], 0)
    hext_ref[...] = _ext_norm(xp_ref, x_ref, xn_ref, g_ref[0:1], sc, sh, seq_len)

    raw = jnp.dot(hext_ref[_main_rows(t, seq_len), :], wdt_ref[...],
                  preferred_element_type=F32) + dtb_ref[...]
    dt_ref[...] = jnp.maximum(raw, 0.0) + jnp.log1p(jnp.exp(-jnp.abs(raw)))

    for j in range(di // COL_CHUNK):
        c0 = j * COL_CHUNK
        z_ref[:, c0:c0 + COL_CHUNK] = jnp.dot(hext_ref[_main_rows(t, seq_len), :],
                                              win_ref[:, c0:c0 + COL_CHUNK], preferred_element_type=F32)

    for j in range(SSM_CONV_DIM // COL_CHUNK):
        c0 = j * COL_CHUNK
        u = jnp.dot(hext_ref[...], win_ref[:, di + c0:di + c0 + COL_CHUNK], preferred_element_type=F32)
        act = _silu(_conv3(u, cw_ref[:, c0:c0 + COL_CHUNK], t, seq_len) + cb_ref[:, c0:c0 + COL_CHUNK])
        if c0 < di:
            xT_ref[c0:c0 + COL_CHUNK, :] = act.T
        elif c0 < di + gn:
            bm_ref[:, c0 - di:c0 - di + COL_CHUNK] = act.astype(BF16)
        else:
            cmT_ref[c0 - di - gn:c0 - di - gn + COL_CHUNK, :] = act.T.astype(BF16)


def _ssd_in(x, mod, g, p, layer, seq_len):
    n = x.shape[0]
    t = SCAN_STEP
    gn = SSM_GROUPS * SSM_STATE
    weights = [p[k] for k in ("w_in", "w_dt", "conv_w", "conv_b", "dt_bias")]
    return pl.pallas_call(
        functools.partial(_ssd_in_kernel, seq_len=seq_len),
        grid=(n // t,),
        in_specs=_halo_specs(n, t, D_MODEL) + [
            _mod_spec(mod.shape[0], t, seq_len), _const_spec(g.shape)]
        + [_layer_spec(w, layer) for w in weights],
        out_specs=[pl.BlockSpec((t, SSM_D_INNER), lambda i: (i, 0)),
                   pl.BlockSpec((None, SSM_D_INNER, t), lambda i: (i, 0, 0)),
                   pl.BlockSpec((t, gn), lambda i: (i, 0)),
                   pl.BlockSpec((None, gn, t), lambda i: (i, 0, 0)),
                   pl.BlockSpec((t, LANES), lambda i: (i, 0))],
        out_shape=[jax.ShapeDtypeStruct((n, SSM_D_INNER), F32),
                   jax.ShapeDtypeStruct((n // t, SSM_D_INNER, t), F32),
                   jax.ShapeDtypeStruct((n, gn), BF16),
                   jax.ShapeDtypeStruct((n // t, gn, t), BF16),
                   jax.ShapeDtypeStruct((n, LANES), F32)],
        scratch_shapes=[pltpu.VMEM((_ext_rows(t, seq_len), D_MODEL), BF16)],
        compiler_params=_cparams(1),
        name="ssd_in_proj",
    )(x, x, x, mod, g, *weights)


def _split3(v):
    hi = v.astype(BF16)
    r1 = v - hi.astype(F32)
    mid = r1.astype(BF16)
    lo = (r1 - mid.astype(F32)).astype(BF16)
    return hi, mid, lo


def _chunk_decays(dt_ref, alog_ref, off, reverse):
    q = SCAN_CHUNK
    dt = dt_ref[off:off + q, :]
    da = dt * (-jnp.exp(alog_ref[...]))
    r_i = lax.broadcasted_iota(jnp.int32, (q, q), 0)
    c_i = lax.broadcasted_iota(jnp.int32, (q, q), 1)
    tri = jnp.where((c_i >= r_i) if reverse else (c_i <= r_i), 1.0, 0.0).astype(BF16)
    acs = sum(jnp.dot(tri, part, preferred_element_type=F32) for part in _split3(da))
    mask = (r_i >= c_i) if reverse else (r_i <= c_i)
    acsT = acs.T
    dtT = dt.T
    src = acs - jnp.log(dt)
    last = 0 if reverse else q - 1
    totT = acsT[:, last:last + 1]
    wT = jnp.exp(totT - acsT) * dtT
    eaT = jnp.exp(acsT)
    etot = jnp.broadcast_to(jnp.exp(totT), (LANES, SSM_STATE))
    return mask, acsT, src, wT, eaT, etot


def _group_products(bm_ref, cmT_ref, st_ref, off, g):
    tok = slice(off, off + SCAN_CHUNK)
    ns = slice(g * SSM_STATE, (g + 1) * SSM_STATE)
    bm_g = bm_ref[tok, ns]
    cmT_g = cmT_ref[ns, tok]
    cbT = jnp.dot(bm_g, cmT_g, preferred_element_type=F32)
    st_g = st_ref[g * GROUP_ROWS:(g + 1) * GROUP_ROWS, :]
    y_off = jnp.dot(st_g.astype(BF16), cmT_g, preferred_element_type=F32)
    return bm_g, cbT, st_g, y_off


def _group_heads(decays, products, xT_ref, yT_ref, st_ref, off, g, reverse):
    mask, acsT, src, wT, eaT, etot = decays
    bm_g, cbT, st_g, y_off = products
    tok = slice(off, off + SCAN_CHUNK)
    xT_g = xT_ref[g * GROUP_ROWS:(g + 1) * GROUP_ROWS, tok]
    head0 = (SSM_HEADS if reverse else 0) + g * HEADS_PER_GROUP
    xw, scale = [], []
    for r in range(HEADS_PER_GROUP):
        col = head0 + r
        hr = slice(r * SSM_HEAD_DIM, (r + 1) * SSM_HEAD_DIM)
        mT = jnp.where(mask, cbT * jnp.exp(acsT[col:col + 1, :] - src[:, col:col + 1]), 0.0)
        xh = xT_g[hr, :]
        yh = jnp.dot(xh.astype(BF16), mT.astype(BF16), preferred_element_type=F32)
        yh = yh + y_off[hr, :] * eaT[col:col + 1, :]
        yT_ref[g * GROUP_ROWS + r * SSM_HEAD_DIM:g * GROUP_ROWS + (r + 1) * SSM_HEAD_DIM, tok] = yh
        xw.append((xh * wT[col:col + 1, :]).astype(BF16))
        scale.append(jnp.broadcast_to(etot[col:col + 1, :], (SSM_HEAD_DIM, SSM_STATE)))
    d_state = jnp.dot(jnp.concatenate(xw, axis=0), bm_g, preferred_element_type=F32)
    st_ref[g * GROUP_ROWS:(g + 1) * GROUP_ROWS, :] = st_g * jnp.concatenate(scale, axis=0) + d_state


def _ssd_scan_kernel(*refs, reverse, has_h0, finish, emit_state, state_aliased, n_steps):
    refs = list(refs)
    xT_ref, bm_ref, cmT_ref, dt_ref, alog_ref = refs[:5]
    pos = 5
    h0_ref = None
    if has_h0:
        h0_ref = refs[pos]
        pos += 1
    if finish:
        (z_ref, ybT_ref, x_ref, mod_ref, g_ref, dexp_ref, ng_ref, wout_ref) = refs[pos:pos + 8]
        pos += 8
    if state_aliased:
        pos += 1
    y_out_ref = refs[pos]
    pos += 1
    state_out_ref = None
    if emit_state:
        state_out_ref = refs[pos]
        pos += 1
    st_ref = refs[pos]
    if finish:
        yT_ref, ybf_ref = refs[pos + 1:pos + 3]
    else:
        yT_ref = y_out_ref

    c = pl.program_id(1)

    @pl.when(c == 0)
    def _():
        if has_h0:
            st_ref[...] = h0_ref[0, 0, 0]
        else:
            st_ref[...] = jnp.zeros_like(st_ref)

    offsets = [k * SCAN_CHUNK for k in range(SCAN_STEP // SCAN_CHUNK)]
    if reverse:
        offsets.reverse()
    decays = [_chunk_decays(dt_ref, alog_ref, off, reverse) for off in offsets]
    for off, dec in zip(offsets, decays):
        ahead = _group_products(bm_ref, cmT_ref, st_ref, off, 0)
        for g in range(SSM_GROUPS):
            cur = ahead
            if g + 1 < SSM_GROUPS:
                ahead = _group_products(bm_ref, cmT_ref, st_ref, off, g + 1)
            _group_heads(dec, cur, xT_ref, yT_ref, st_ref, off, g, reverse)
        if finish:
            tok = slice(off, off + SCAN_CHUNK)
            yT = yT_ref[:, tok] + ybT_ref[:, tok] + dexp_ref[:, tok] * xT_ref[:, tok]
            y = yT.T * _silu(z_ref[tok, :])
            ybf_ref[tok, :] = _rms(y, ng_ref[...]).astype(BF16)

    if emit_state:
        @pl.when(c == n_steps - 1)
        def _():
            state_out_ref[0, 0, 0] = st_ref[...]

    if finish:
        out = jnp.dot(ybf_ref[...], wout_ref[...], preferred_element_type=F32)
        _, _, gt = _mod_slices(mod_ref[0], 0)
        y_out_ref[...] = x_ref[...] + gt * _rms(out, g_ref[1:2])


def _ssd_scan(xT, bm, cmT, dt, p, layer, h0, fin, state, *, batch, reverse):
    q = SCAN_STEP
    n = xT.shape[0] * q
    direction = 1 if reverse else 0
    emit_state = state is not None
    nc = n // batch // q
    gn = SSM_GROUPS * SSM_STATE
    finish = fin is not None

    def tok(b, c):
        return b * nc + ((nc - 1 - c) if reverse else c)

    def row_blk(b, c):
        return (tok(b, c), 0)

    t_spec = pl.BlockSpec((None, SSM_D_INNER, q), lambda b, c: (tok(b, c), 0, 0))
    in_specs = [t_spec,
                pl.BlockSpec((q, gn), row_blk),
                pl.BlockSpec((None, gn, q), lambda b, c: (tok(b, c), 0, 0)),
                pl.BlockSpec((q, LANES), row_blk),
                _layer_spec(p["a_log"], layer)]
    args = [xT, bm, cmT, dt, p["a_log"]]
    if h0 is not None:
        in_specs.append(pl.BlockSpec((1, 1, 1, SSM_D_INNER, SSM_STATE),
                                     lambda b, c: (b, layer, direction, 0, 0)))
        args.append(h0)
    if finish:
        z, ybT, x, mod, g = fin
        n_mod = mod.shape[0]
        in_specs += [pl.BlockSpec((q, SSM_D_INNER), row_blk),
                     t_spec,
                     pl.BlockSpec((q, D_MODEL), row_blk),
                     pl.BlockSpec((1, 1, 6 * D_MODEL),
                                  (lambda b, c: (b, 0, 0)) if n_mod > 1 else (lambda b, c: (0, 0, 0))),
                     _const_spec(g.shape), _layer_spec(p["d_exp"], layer),
                     _layer_spec(p["norm_g"], layer), _layer_spec(p["w_out"], layer)]
        args += [z, ybT, x, mod, g, p["d_exp"], p["norm_g"], p["w_out"]]
        out_specs = [pl.BlockSpec((q, D_MODEL), row_blk)]
        out_shape = [jax.ShapeDtypeStruct((n, D_MODEL), F32)]
    else:
        out_specs = [t_spec]
        out_shape = [jax.ShapeDtypeStruct((n // q, SSM_D_INNER, q), F32)]
    aliases = {}
    if emit_state:
        buf, n_layers = state
        out_specs.append(pl.BlockSpec((1, 1, 1, SSM_D_INNER, SSM_STATE),
                                      lambda b, c: (b, layer, direction, 0, 0)))
        out_shape.append(jax.ShapeDtypeStruct((batch, n_layers, 2, SSM_D_INNER, SSM_STATE), F32))
        if buf is not None:
            in_specs.append(pl.BlockSpec(memory_space=pl.ANY))
            args.append(buf)
            aliases = {len(args) - 1: 1}
    scratch = [pltpu.VMEM((SSM_D_INNER, SSM_STATE), F32)]
    if finish:
        scratch += [pltpu.VMEM((SSM_D_INNER, q), F32), pltpu.VMEM((q, SSM_D_INNER), BF16)]
    outs = pl.pallas_call(
        functools.partial(_ssd_scan_kernel, reverse=reverse, has_h0=h0 is not None, finish=finish,
                          emit_state=emit_state, state_aliased=bool(aliases), n_steps=nc),
        grid=(batch, nc),
        in_specs=in_specs, out_specs=out_specs, out_shape=out_shape,
        scratch_shapes=scratch,
        input_output_aliases=aliases,
        compiler_params=_cparams(2),
        name="ssd_scan_bwd" if reverse else "ssd_scan_fwd",
    )(*args)
    return outs if emit_state else (outs[0], None)


def _ssd_layer(x, mod, g, p, layer, h0, state, batch, seq_len):
    z, xT, bm, cmT, dt = _ssd_in(x, mod, g, p, layer, seq_len)
    ybT, buf = _ssd_scan(xT, bm, cmT, dt, p, layer, h0, None, state, batch=batch, reverse=True)
    if state is not None:
        state = (buf, state[1])
    x_new, buf = _ssd_scan(xT, bm, cmT, dt, p, layer, h0, (z, ybT, x, mod, g), state,
                           batch=batch, reverse=False)
    return x_new, buf


def _qkv_kernel(*refs, rotary):
    kh_ref = None
    if rotary:
        x_ref, mod_ref, g_ref, w_ref, cos_ref, sin_ref, q_ref, k_ref, v_ref = refs
    else:
        x_ref, mod_ref, g_ref, w_ref, q_ref, k_ref, v_ref, kh_ref = refs
    d = D_MODEL
    sh, sc, _ = _mod_slices(mod_ref[0], 0)
    h = (_rms(x_ref[...], g_ref[0:1]) * (1.0 + sc) + sh).astype(BF16)
    if rotary:
        lane = lax.broadcasted_iota(jnp.int32, (x_ref.shape[0], LANES), 1)
        first_half = (lane & (DIFF_HEAD_DIM - 1)) < DIFF_HEAD_DIM // 2
        cos = cos_ref[...]
        sin = sin_ref[...]

    def rot(a):
        if not rotary:
            return a
        partner = jnp.where(first_half, pltpu.roll(a, LANES - DIFF_HEAD_DIM // 2, 1),
                            pltpu.roll(a, DIFF_HEAD_DIM // 2, 1))
        return a * cos + partner * sin

    scale = DIFF_HEAD_DIM ** -0.5 * math.log2(math.e)
    for j in range(d // LANES):
        cs = slice(j * LANES, (j + 1) * LANES)
        qj = jnp.dot(h, w_ref[:, j * LANES:(j + 1) * LANES], preferred_element_type=F32)
        kj = jnp.dot(h, w_ref[:, d + j * LANES:d + (j + 1) * LANES], preferred_element_type=F32)
        q_ref[:, cs] = (rot(qj) * scale).astype(q_ref.dtype)
        kj = rot(kj)
        if rotary:
            k_ref[:, cs] = kj.astype(k_ref.dtype)
        else:
            k_ref[cs, :] = kj.T.astype(k_ref.dtype)
        if kh_ref is not None:
            for i in range(LANES // DIFF_HEAD_DIM):
                kh_ref[:, j * (LANES // DIFF_HEAD_DIM) + i, :] = (
                    kj[:, i * DIFF_HEAD_DIM:(i + 1) * DIFF_HEAD_DIM])
    v_ref[...] = jnp.dot(h, w_ref[:, 2 * d:3 * d], preferred_element_type=F32).astype(v_ref.dtype)


def _qkv(x, mod, g, w_qkv, rot_tables, seq_len):
    n = x.shape[0]
    t = ROW_TILE
    rotary = rot_tables is not None
    in_specs = [pl.BlockSpec((t, D_MODEL), lambda i: (i, 0)), _mod_spec(mod.shape[0], t, seq_len),
                _const_spec(g.shape), _const_spec(w_qkv.shape)]
    args = [x, mod, g, w_qkv]
    if rotary:
        per_seq = seq_len // t
        in_specs += [pl.BlockSpec((t, LANES), lambda i: (i % per_seq, 0))] * 2
        args += list(rot_tables)
    row_spec = pl.BlockSpec((t, D_MODEL), lambda i: (i, 0))
    blk = max(seq_len, t)
    kT_spec = pl.BlockSpec((None, D_MODEL, t), lambda i: ((i * t) // blk, 0, ((i * t) % blk) // t))
    out_specs = [row_spec, row_spec if rotary else kT_spec, row_spec]
    out_shape = [jax.ShapeDtypeStruct((n, D_MODEL), BF16),
                 jax.ShapeDtypeStruct((n, D_MODEL) if rotary else (n // blk, D_MODEL, blk), BF16),
                 jax.ShapeDtypeStruct((n, D_MODEL), BF16 if rotary else F32)]
    if not rotary:
        out_specs.append(pl.BlockSpec((t, 2 * DIFF_HEADS, DIFF_HEAD_DIM), lambda i: (i, 0, 0)))
        out_shape.append(jax.ShapeDtypeStruct((n, 2 * DIFF_HEADS, DIFF_HEAD_DIM), F32))
    return pl.pallas_call(
        functools.partial(_qkv_kernel, rotary=rotary),
        grid=(n // t,),
        in_specs=in_specs,
        out_specs=out_specs,
        out_shape=out_shape,
        compiler_params=_cparams(1),
        name="diff_qkv",
    )(*args)


def _attn_kernel(*refs, has_cache, keys_transposed, lam_init):
    if has_cache:
        (q_ref, k_ref, v_ref, ck_ref, cv_ref, x_ref, mod_ref, g_ref, lp_ref, sg_ref, wo_ref,
         o_ref, oall_ref) = refs
    else:
        q_ref, k_ref, v_ref, x_ref, mod_ref, g_ref, lp_ref, sg_ref, wo_ref, o_ref, oall_ref = refs
    dn = (((1,), (1,)), ((), ()))
    lp = lp_ref[...]
    lam = (jnp.exp(jnp.sum(lp[0:1] * lp[1:2], axis=-1, keepdims=True))
           - jnp.exp(jnp.sum(lp[2:3] * lp[3:4], axis=-1, keepdims=True)) + lam_init)
    hd = DIFF_HEAD_DIM

    def scores(hh):
        hs = slice(hh * hd, (hh + 1) * hd)
        qh = q_ref[:, hs]
        if keys_transposed:
            s_new = jnp.dot(qh, k_ref[hs, :], preferred_element_type=F32)
        else:
            s_new = lax.dot_general(qh, k_ref[:, hs], dn, preferred_element_type=F32)
        s_old = None
        if has_cache:
            s_old = lax.dot_general(qh, ck_ref[0, :, hs].astype(BF16), dn, preferred_element_type=F32)
        return s_old, s_new

    def probs(s_old, s_new):
        m = jnp.max(s_new, axis=-1, keepdims=True)
        if has_cache:
            m = jnp.maximum(m, jnp.max(s_old, axis=-1, keepdims=True))
            p_old = jnp.exp2(s_old - m)
        p_new = jnp.exp2(s_new - m)
        l = jnp.sum(p_new, axis=-1, keepdims=True)
        if has_cache:
            l = l + jnp.sum(p_old, axis=-1, keepdims=True)
            return p_old, p_new, l
        return None, p_new, l

    def diff_weights(pair_scores):
        po0, pn0, l0 = probs(*pair_scores[0])
        po1, pn1, l1 = probs(*pair_scores[1])
        ratio = lam * l0 / l1
        att_old = (po0 - po1 * ratio).astype(BF16) if has_cache else None
        return att_old, (pn0 - pn1 * ratio).astype(BF16), l0

    def weighted_values(hp, att_old, att, l0):
        vs = slice(hp * 2 * hd, (hp + 1) * 2 * hd)
        o = jnp.dot(att, v_ref[:, vs].astype(BF16), preferred_element_type=F32)
        if has_cache:
            o = o + jnp.dot(att_old, cv_ref[0, :, vs].astype(BF16), preferred_element_type=F32)
        o = o * (1.0 / l0)
        oall_ref[:, vs] = (_rms(o, sg_ref[...]) * (1.0 - lam_init)).astype(BF16)

    sc, att = {}, {}
    for step in range(DIFF_HEADS + 2):
        if step < DIFF_HEADS:
            sc[step] = (scores(2 * step), scores(2 * step + 1))
        if step >= 2:
            weighted_values(step - 2, *att.pop(step - 2))
        if 1 <= step <= DIFF_HEADS:
            att[step - 1] = diff_weights(sc.pop(step - 1))

    m_out = jnp.dot(oall_ref[...], wo_ref[...], preferred_element_type=F32)
    _, _, gt = _mod_slices(mod_ref[0], 0)
    o_ref[...] = x_ref[...] + gt * _rms(m_out, g_ref[1:2])


def _attention(q, k, v, cache, x, mod, g, lam_p, subln_g, w_o, *, batch, lam_init):
    n = x.shape[0]
    seq = n // batch
    tq = min(ATTN_Q_TILE, seq)
    nq = seq // tq
    has_cache = cache is not None
    keys_transposed = k.ndim == 3
    v_spec = pl.BlockSpec((seq, D_MODEL), lambda b, t: (b, 0))
    if keys_transposed:
        blk = k.shape[2]
        k_spec = pl.BlockSpec((None, D_MODEL, seq),
                              lambda b, t: ((b * seq) // blk, 0, ((b * seq) % blk) // seq))
    else:
        k_spec = v_spec
    in_specs = [pl.BlockSpec((tq, D_MODEL), lambda b, t: (b * nq + t, 0)), k_spec, v_spec]
    args = [q, k, v]
    if has_cache:
        past = cache[0].shape[1]
        c_spec = pl.BlockSpec((1, past, D_MODEL), lambda b, t: (b, 0, 0))
        in_specs += [c_spec, c_spec]
        args += list(cache)
    n_mod = mod.shape[0]
    in_specs += [pl.BlockSpec((tq, D_MODEL), lambda b, t: (b * nq + t, 0)),
                 pl.BlockSpec((1, 1, 6 * D_MODEL),
                              (lambda b, t: (b, 0, 0)) if n_mod > 1 else (lambda b, t: (0, 0, 0))),
                 _const_spec(g.shape), _const_spec(lam_p.shape), _const_spec(subln_g.shape),
                 _const_spec(w_o.shape)]
    args += [x, mod, g, lam_p, subln_g, w_o]
    return pl.pallas_call(
        functools.partial(_attn_kernel, has_cache=has_cache, keys_transposed=keys_transposed,
                          lam_init=lam_init),
        grid=(batch, nq),
        in_specs=in_specs,
        out_specs=pl.BlockSpec((tq, D_MODEL), lambda b, t: (b * nq + t, 0)),
        out_shape=jax.ShapeDtypeStruct((n, D_MODEL), F32),
        scratch_shapes=[pltpu.VMEM((tq, D_MODEL), BF16)],
        compiler_params=_cparams(2),
        name="diff_attention",
    )(*args)


def _rotary_tables(n_tokens):
    rows = n_tokens // GRID_W
    row = jnp.repeat(jnp.arange(rows, dtype=F32), GRID_W)
    col = jnp.tile(jnp.arange(GRID_W, dtype=F32), rows)
    inv = ROPE_THETA ** (-jnp.arange(ROT_PAIRS_PER_AXIS, dtype=F32) / ROT_PAIRS_PER_AXIS)
    ang = jnp.concatenate([row[:, None] * inv, col[:, None] * inv], axis=-1)
    cos, sin = jnp.cos(ang), jnp.sin(ang)
    reps = LANES // DIFF_HEAD_DIM
    return (jnp.tile(jnp.concatenate([cos, cos], axis=-1), (1, reps)),
            jnp.tile(jnp.concatenate([-sin, sin], axis=-1), (1, reps)))


def kernel(x_prompt, x_sample, state_ssm, cache_k, cache_v, c, c_ctx, w_mod, b_mod, norm_g, ssd_w_in, ssd_conv_w, ssd_conv_b, ssd_dt_bias, ssd_a_log, ssd_d, ssd_norm_g, ssd_w_out, sc_w_in, sc_conv_w, sc_w_out, da_w_qkv, da_lambda, da_subln_g, da_w_out, ffn_w_up, ffn_conv_w, ffn_w_down):
    bp, lp_, d = x_prompt.shape
    bs, ls, _ = x_sample.shape
    xp = x_prompt.reshape(bp * lp_, d)
    xs = x_sample.reshape(bs * ls, d)

    n_cond = 1 + bs
    cond = jnp.concatenate([c_ctx[None], c, jnp.zeros((HALO - n_cond, d), F32)], axis=0)
    mod_all = _modulation(cond, w_mod, b_mod)

    rot = _rotary_tables(ls)
    n_ssd_layers = (DEPTH + 2) // N_MIXERS
    pad = LANES - 2 * SSM_HEADS
    xbc_end = SSM_D_INNER + SSM_CONV_DIM
    ssd = {
        "w_in": ssd_w_in.astype(BF16),
        "w_dt": jnp.pad(ssd_w_in[:, :, xbc_end:], ((0, 0), (0, 0), (0, pad))).astype(BF16),
        "conv_w": ssd_conv_w,
        "conv_b": ssd_conv_b[:, None],
        "dt_bias": jnp.pad(ssd_dt_bias.reshape(n_ssd_layers, 1, -1), ((0, 0), (0, 0), (0, pad))),
        "a_log": jnp.pad(ssd_a_log.reshape(n_ssd_layers, 1, -1), ((0, 0), (0, 0), (0, pad))),
        "d_exp": jnp.broadcast_to(jnp.repeat(ssd_d, SSM_HEAD_DIM, axis=1)[:, :, None],
                                  (n_ssd_layers, SSM_D_INNER, SCAN_STEP)),
        "norm_g": ssd_norm_g[:, None],
        "w_out": ssd_w_out.astype(BF16),
    }
    h0 = state_ssm.reshape(bs, n_ssd_layers, 2, SSM_D_INNER, SSM_STATE)
    w_up_all, w_dn_all = ffn_w_up.astype(BF16), ffn_w_down.astype(BF16)
    ssm_buf = None
    new_k, new_v = [], []
    for l in range(DEPTH):
        j = l // N_MIXERS
        kind = l % N_MIXERS
        mod_p = mod_all[l, 0:1][:, None]
        mod_s = mod_all[l, 1:n_cond][:, None]
        g = norm_g[l]
        if kind == 0:
            xp, ssm_buf = _ssd_layer(xp, mod_p, g, ssd, j, None, (ssm_buf, n_ssd_layers), bp, lp_)
            xs, _ = _ssd_layer(xs, mod_s, g, ssd, j, h0, None, bs, ls)
        elif kind == 1:
            w_in, w_out = sc_w_in[j].astype(BF16), sc_w_out[j].astype(BF16)
            xp = _sconv(xp, mod_p, g, w_in, sc_conv_w[j], w_out, lp_)
            xs = _sconv(xs, mod_s, g, w_in, sc_conv_w[j], w_out, ls)
        else:
            lam_init = 0.8 - 0.6 * math.exp(-0.3 * l)
            w_qkv, w_o = da_w_qkv[j].astype(BF16), da_w_out[j].astype(BF16)
            sg = da_subln_g[j][None]
            qp, kp, vp, kp_heads = _qkv(xp, mod_p, g, w_qkv, None, lp_)
            xp = _attention(qp, kp, vp, None, xp, mod_p, g, da_lambda[j], sg, w_o,
                            batch=bp, lam_init=lam_init)
            new_k.append(kp_heads.reshape(bp, lp_, 2 * DIFF_HEADS, DIFF_HEAD_DIM))
            new_v.append(vp.reshape(bp, lp_, DIFF_HEADS, 2 * DIFF_HEAD_DIM))
            qs, ks, vs = _qkv(xs, mod_s, g, w_qkv, rot, ls)
            cache = (cache_k[:, j].reshape(bs, -1, d), cache_v[:, j].reshape(bs, -1, d))
            xs = _attention(qs, ks, vs, cache, xs, mod_s, g, da_lambda[j], sg, w_o,
                            batch=bs, lam_init=lam_init)
        xp = _ffn(xp, mod_p, g, l, w_up_all, ffn_conv_w, w_dn_all, lp_)
        xs = _ffn(xs, mod_s, g, l, w_up_all, ffn_conv_w, w_dn_all, ls)

    new_ssm = ssm_buf.reshape(bp, n_ssd_layers, 2, SSM_HEADS, SSM_HEAD_DIM, SSM_STATE)
    return (xp.reshape(bp, lp_, d), xs.reshape(bs, ls, d), new_ssm,
            jnp.stack(new_k, axis=1), jnp.stack(new_v, axis=1))
```

```python
import functools
import math

import jax
import jax.numpy as jnp
from jax import lax
from jax.experimental import pallas as pl
from jax.experimental.pallas import tpu as pltpu

D_MODEL = 1024
DEPTH = 4
GRID_W = 64
N_MIXERS = 3
SSM_D_INNER = 2 * D_MODEL
SSM_HEAD_DIM = 64
SSM_HEADS = SSM_D_INNER // SSM_HEAD_DIM
SSM_GROUPS = 8
SSM_STATE = 128
SSM_CONV_DIM = SSM_D_INNER + 2 * SSM_GROUPS * SSM_STATE
HEADS_PER_GROUP = SSM_HEADS // SSM_GROUPS
GROUP_ROWS = HEADS_PER_GROUP * SSM_HEAD_DIM
DIFF_HEAD_DIM = 64
DIFF_HEADS = D_MODEL // (2 * DIFF_HEAD_DIM)
ROPE_THETA = 10000.0
ROT_PAIRS_PER_AXIS = DIFF_HEAD_DIM // 4
FFN_DIM = 2816
NORM_EPS = 1e-6

LANES = 128
HALO = 8
SCAN_CHUNK = 128
SCAN_STEP = 256
ROW_TILE = 512
ATTN_Q_TILE = 256
COL_CHUNK = 256
FFN_DOWN_GROUP = 4
FFN_DOWN_LAG = 3
VMEM_LIMIT = 56 * 1024 * 1024

F32 = jnp.float32
BF16 = jnp.bfloat16


def _cparams(n_axes):
    return pltpu.CompilerParams(dimension_semantics=("arbitrary",) * n_axes,
                                vmem_limit_bytes=VMEM_LIMIT)


def _rms(x, g):
    ms = jnp.mean(x * x, axis=-1, keepdims=True)
    return x * lax.rsqrt(ms + NORM_EPS) * g


def _silu(x):
    return x * jax.nn.sigmoid(x)


def _mod_slices(mod, first):
    d = D_MODEL
    return (mod[:, first * d:(first + 1) * d], mod[:, (first + 1) * d:(first + 2) * d],
            mod[:, (first + 2) * d:(first + 3) * d])


def _ext_rows(tile_rows, seq_len):
    return tile_rows if seq_len <= tile_rows else tile_rows + 2 * HALO


def _main_rows(tile_rows, seq_len):
    return slice(0, tile_rows) if seq_len <= tile_rows else slice(HALO, HALO + tile_rows)


def _ext_norm(xp_ref, x_ref, xn_ref, g, sc, sh, seq_len):
    t = x_ref.shape[0]
    if seq_len <= t:
        return (_rms(x_ref[...], g) * (1.0 + sc) + sh).astype(BF16)
    tiles_per_seq = seq_len // t
    k = pl.program_id(0) % tiles_per_seq
    keep = (jnp.where(k == 0, 0.0, 1.0), None, jnp.where(k == tiles_per_seq - 1, 0.0, 1.0))
    pieces = []
    for r, kp in zip((xp_ref, x_ref, xn_ref), keep):
        h = _rms(r[...], g) * (1.0 + sc) + sh
        pieces.append(h if kp is None else h * kp)
    return jnp.concatenate(pieces, axis=0).astype(BF16)


def _conv3(u, w3, tile_rows, seq_len):
    if u.shape[0] == tile_rows:
        return _conv_rows(u, w3, 0, tile_rows, seq_len, first_row=0)
    return _conv_rows(u, w3, HALO, tile_rows, seq_len)


def _conv_rows(u, w3, off, n_out, seq_len, first_row=None):
    n = u.shape[0]
    rows = slice(off, off + n_out)
    up = pltpu.roll(u, 1, 0)[rows]
    un = pltpu.roll(u, n - 1, 0)[rows]
    if first_row is not None:
        pos = (lax.broadcasted_iota(jnp.int32, (n_out, u.shape[1]), 0) + first_row) & (seq_len - 1)
        up = jnp.where(pos != 0, up, 0.0)
        un = jnp.where(pos != seq_len - 1, un, 0.0)
    return up * w3[0:1] + u[rows] * w3[1:2] + un * w3[2:3]


def _halo_specs(n_rows, tile_rows, width):
    per = tile_rows // HALO
    last = n_rows // HALO - 1
    return [
        pl.BlockSpec((HALO, width), lambda i: (jnp.maximum(i * per - 1, 0), 0)),
        pl.BlockSpec((tile_rows, width), lambda i: (i, 0)),
        pl.BlockSpec((HALO, width), lambda i: (jnp.minimum((i + 1) * per, last), 0)),
    ]


def _mod_spec(n_mod, tile_rows, seq_len):
    if n_mod == 1:
        return pl.BlockSpec((1, 1, 6 * D_MODEL), lambda i: (0, 0, 0))
    return pl.BlockSpec((1, 1, 6 * D_MODEL), lambda i: ((i * tile_rows) // seq_len, 0, 0))


def _const_spec(shape):
    nd = len(shape)
    return pl.BlockSpec(shape, lambda *_: (0,) * nd)


def _layer_spec(stacked, layer):
    shape = stacked.shape[1:]
    idx = (layer,) + (0,) * len(shape)
    return pl.BlockSpec((None,) + shape, lambda *_: idx, pipeline_mode=pl.Buffered(1))


def _mod_kernel(cond_ref, w_ref, b_ref, o_ref):
    a = _silu(cond_ref[...]).astype(BF16)
    o_ref[0] = jnp.dot(a, w_ref[0].astype(BF16), preferred_element_type=F32) + b_ref[0]


def _modulation(cond, w_mod, b_mod):
    n_cols = 6 * D_MODEL
    blk = n_cols // 4
    return pl.pallas_call(
        _mod_kernel,
        grid=(DEPTH, n_cols // blk),
        in_specs=[_const_spec(cond.shape),
                  pl.BlockSpec((1, D_MODEL, blk), lambda l, j: (l, 0, j)),
                  pl.BlockSpec((1, 1, blk), lambda l, j: (l, 0, j))],
        out_specs=pl.BlockSpec((1, cond.shape[0], blk), lambda l, j: (l, 0, j)),
        out_shape=jax.ShapeDtypeStruct((DEPTH, cond.shape[0], n_cols), F32),
        compiler_params=_cparams(2),
        name="modulation",
    )(cond, w_mod, b_mod.reshape(DEPTH, 1, n_cols))


def _ffn_kernel(xp_ref, x_ref, xn_ref, mod_ref, g_ref, wup_ref, cw_ref, wdn_ref, o_ref,
                hext_ref, act_ref, *, seq_len):
    t = x_ref.shape[0]
    sh, sc, gt = _mod_slices(mod_ref[0], 3)
    hext_ref[...] = _ext_norm(xp_ref, x_ref, xn_ref, g_ref[2:3], sc, sh, seq_len)
    n_chunks = FFN_DIM // COL_CHUNK

    slabs = [(c0, min(c0 + FFN_DOWN_GROUP, n_chunks - 1)) for c0 in range(0, n_chunks - 1, FFN_DOWN_GROUP)]
    issue_at = {}
    for c0, c1 in slabs:
        issue_at.setdefault(min(c1 - 1 + FFN_DOWN_LAG, n_chunks - 1), []).append((c0, c1))

    acc = None

    def down(c0, c1):
        k0, k1 = c0 * COL_CHUNK, c1 * COL_CHUNK
        part = jnp.dot(act_ref[:, k0:k1], wdn_ref[k0:k1, :], preferred_element_type=F32)
        return part if acc is None else acc + part

    for j in range(n_chunks):
        cg = j * COL_CHUNK
        cv = FFN_DIM + cg
        h = hext_ref[...]
        ug = jnp.dot(h, wup_ref[:, cg:cg + COL_CHUNK], preferred_element_type=F32)
        uv = jnp.dot(h, wup_ref[:, cv:cv + COL_CHUNK], preferred_element_type=F32)
        for c0, c1 in issue_at.get(j, []):
            acc = down(c0, c1)
        gate = _conv3(ug, cw_ref[:, cg:cg + COL_CHUNK], t, seq_len)
        val = _conv3(uv, cw_ref[:, cv:cv + COL_CHUNK], t, seq_len)
        act_ref[:, cg:cg + COL_CHUNK] = (_silu(gate) * val).astype(BF16)
    acc = down(n_chunks - 1, n_chunks)
    o_ref[...] = x_ref[...] + gt * _rms(acc, g_ref[3:4])


def _ffn(x, mod, g, layer, w_up, conv_w, w_down, seq_len):
    n = x.shape[0]
    t = ROW_TILE
    return pl.pallas_call(
        functools.partial(_ffn_kernel, seq_len=seq_len),
        grid=(n // t,),
        in_specs=_halo_specs(n, t, D_MODEL) + [
            _mod_spec(mod.shape[0], t, seq_len), _const_spec(g.shape), _layer_spec(w_up, layer),
            _layer_spec(conv_w, layer), _layer_spec(w_down, layer)],
        out_specs=pl.BlockSpec((t, D_MODEL), lambda i: (i, 0)),
        out_shape=jax.ShapeDtypeStruct((n, D_MODEL), F32),
        scratch_shapes=[pltpu.VMEM((_ext_rows(t, seq_len), D_MODEL), BF16),
                        pltpu.VMEM((t, FFN_DIM), BF16)],
        compiler_params=_cparams(1),
        name="conv_ffn",
    )(x, x, x, mod, g, w_up, conv_w, w_down)


def _sconv_kernel(xp_ref, x_ref, xn_ref, mod_ref, g_ref, win_ref, cw_ref, wout_ref, o_ref,
                  hext_ref, act_ref, *, seq_len):
    t = x_ref.shape[0]
    d = D_MODEL
    sh, sc, gt = _mod_slices(mod_ref[0], 0)
    hext_ref[...] = _ext_norm(xp_ref, x_ref, xn_ref, g_ref[0:1], sc, sh, seq_len)
    for j in range(d // COL_CHUNK):
        c0 = j * COL_CHUNK
        bg = jnp.dot(hext_ref[_main_rows(t, seq_len), :], win_ref[:, c0:c0 + COL_CHUNK],
                     preferred_element_type=F32)
        h = hext_ref[...]
        cg = jnp.dot(h, win_ref[:, d + c0:d + c0 + COL_CHUNK], preferred_element_type=F32)
        u = jnp.dot(h, win_ref[:, 2 * d + c0:2 * d + c0 + COL_CHUNK], preferred_element_type=F32)
        conv = _conv3(cg * u, cw_ref[:, c0:c0 + COL_CHUNK], t, seq_len)
        act_ref[:, c0:c0 + COL_CHUNK] = (bg * conv).astype(BF16)
    m = jnp.dot(act_ref[...], wout_ref[...], preferred_element_type=F32)
    o_ref[...] = x_ref[...] + gt * _rms(m, g_ref[1:2])


def _sconv(x, mod, g, w_in, conv_w, w_out, seq_len):
    n = x.shape[0]
    t = ROW_TILE
    return pl.pallas_call(
        functools.partial(_sconv_kernel, seq_len=seq_len),
        grid=(n // t,),
        in_specs=_halo_specs(n, t, D_MODEL) + [
            _mod_spec(mod.shape[0], t, seq_len), _const_spec(g.shape), _const_spec(w_in.shape),
            _const_spec(conv_w.shape), _const_spec(w_out.shape)],
        out_specs=pl.BlockSpec((t, D_MODEL), lambda i: (i, 0)),
        out_shape=jax.ShapeDtypeStruct((n, D_MODEL), F32),
        scratch_shapes=[pltpu.VMEM((_ext_rows(t, seq_len), D_MODEL), BF16),
                        pltpu.VMEM((t, D_MODEL), BF16)],
        compiler_params=_cparams(1),
        name="short_conv_mixer",
    )(x, x, x, mod, g, w_in, conv_w, w_out)


def _ssd_in_kernel(xp_ref, x_ref, xn_ref, mod_ref, g_ref, win_ref, wdt_ref, cw_ref, cb_ref, dtb_ref,
                   z_ref, xT_ref, bm_ref, cmT_ref, dt_ref, hext_ref, *, seq_len):
    t = x_ref.shape[0]
    di = SSM_D_INNER
    gn = SSM_GROUPS * SSM_STATE
    sh, sc, _ = _mod_slices(mod_ref[0], 0)
    hext_ref[...] = _ext_norm(xp_ref, x_ref, xn_ref, g_ref[0:1], sc, sh, seq_len)

    raw = jnp.dot(hext_ref[_main_rows(t, seq_len), :], wdt_ref[...],
                  preferred_element_type=F32) + dtb_ref[...]
    dt_ref[...] = jnp.maximum(raw, 0.0) + jnp.log1p(jnp.exp(-jnp.abs(raw)))

    n_conv = SSM_CONV_DIM // COL_CHUNK
    n_z = di // COL_CHUNK
    for j in range(n_conv):
        c0 = j * COL_CHUNK
        u = jnp.dot(hext_ref[...], win_ref[:, di + c0:di + c0 + COL_CHUNK], preferred_element_type=F32)
        if j % (n_conv // n_z) == 0:
            zc = j // (n_conv // n_z) * COL_CHUNK
            z_ref[:, zc:zc + COL_CHUNK] = jnp.dot(hext_ref[_main_rows(t, seq_len), :],
                                                  win_ref[:, zc:zc + COL_CHUNK], preferred_element_type=F32)
        act = _silu(_conv3(u, cw_ref[:, c0:c0 + COL_CHUNK], t, seq_len) + cb_ref[:, c0:c0 + COL_CHUNK])
        if c0 < di:
            xT_ref[c0:c0 + COL_CHUNK, :] = act.T
        elif c0 < di + gn:
            bm_ref[:, c0 - di:c0 - di + COL_CHUNK] = act.astype(BF16)
        else:
            cmT_ref[c0 - di - gn:c0 - di - gn + COL_CHUNK, :] = act.T.astype(BF16)


def _ssd_in(x, mod, g, p, layer, seq_len):
    n = x.shape[0]
    t = SCAN_STEP
    gn = SSM_GROUPS * SSM_STATE
    weights = [p[k] for k in ("w_in", "w_dt", "conv_w", "conv_b", "dt_bias")]
    return pl.pallas_call(
        functools.partial(_ssd_in_kernel, seq_len=seq_len),
        grid=(n // t,),
        in_specs=_halo_specs(n, t, D_MODEL) + [
            _mod_spec(mod.shape[0], t, seq_len), _const_spec(g.shape)]
        + [_layer_spec(w, layer) for w in weights],
        out_specs=[pl.BlockSpec((t, SSM_D_INNER), lambda i: (i, 0)),
                   pl.BlockSpec((None, SSM_D_INNER, t), lambda i: (i, 0, 0)),
                   pl.BlockSpec((t, gn), lambda i: (i, 0)),
                   pl.BlockSpec((None, gn, t), lambda i: (i, 0, 0)),
                   pl.BlockSpec((t, LANES), lambda i: (i, 0))],
        out_shape=[jax.ShapeDtypeStruct((n, SSM_D_INNER), F32),
                   jax.ShapeDtypeStruct((n // t, SSM_D_INNER, t), F32),
                   jax.ShapeDtypeStruct((n, gn), BF16),
                   jax.ShapeDtypeStruct((n // t, gn, t), BF16),
                   jax.ShapeDtypeStruct((n, LANES), F32)],
        scratch_shapes=[pltpu.VMEM((_ext_rows(t, seq_len), D_MODEL), BF16)],
        compiler_params=_cparams(1),
        name="ssd_in_proj",
    )(x, x, x, mod, g, *weights)


def _split3(v):
    hi = v.astype(BF16)
    r1 = v - hi.astype(F32)
    mid = r1.astype(BF16)
    lo = (r1 - mid.astype(F32)).astype(BF16)
    return hi, mid, lo


def _chunk_decays(dt_ref, alog_ref, off, reverse):
    q = SCAN_CHUNK
    dt = dt_ref[off:off + q, :]
    da = dt * (-jnp.exp(alog_ref[...]))
    r_i = lax.broadcasted_iota(jnp.int32, (q, q), 0)
    c_i = lax.broadcasted_iota(jnp.int32, (q, q), 1)
    tri = jnp.where((c_i >= r_i) if reverse else (c_i <= r_i), 1.0, 0.0).astype(BF16)
    acs = sum(jnp.dot(tri, part, preferred_element_type=F32) for part in _split3(da))
    mask = (r_i >= c_i) if reverse else (r_i <= c_i)
    acsT = acs.T
    dtT = dt.T
    src = acs - jnp.log(dt)
    last = 0 if reverse else q - 1
    totT = acsT[:, last:last + 1]
    wT = jnp.exp(totT - acsT) * dtT
    eaT = jnp.exp(acsT)
    etot = jnp.broadcast_to(jnp.exp(totT), (LANES, SSM_STATE))
    return mask, acsT, src, wT, eaT, etot


def _group_products(bm_ref, cmT_ref, st_ref, off, g):
    tok = slice(off, off + SCAN_CHUNK)
    ns = slice(g * SSM_STATE, (g + 1) * SSM_STATE)
    bm_g = bm_ref[tok, ns]
    cmT_g = cmT_ref[ns, tok]
    cbT = jnp.dot(bm_g, cmT_g, preferred_element_type=F32)
    st_g = st_ref[g * GROUP_ROWS:(g + 1) * GROUP_ROWS, :]
    y_off = jnp.dot(st_g.astype(BF16), cmT_g, preferred_element_type=F32)
    return bm_g, cbT, st_g, y_off


def _group_heads(decays, products, xT_ref, yT_ref, st_ref, off, g, reverse):
    mask, acsT, src, wT, eaT, etot = decays
    bm_g, cbT, st_g, y_off = products
    tok = slice(off, off + SCAN_CHUNK)
    xT_g = xT_ref[g * GROUP_ROWS:(g + 1) * GROUP_ROWS, tok]
    head0 = (SSM_HEADS if reverse else 0) + g * HEADS_PER_GROUP
    xw, scale = [], []
    for r in range(HEADS_PER_GROUP):
        col = head0 + r
        hr = slice(r * SSM_HEAD_DIM, (r + 1) * SSM_HEAD_DIM)
        mT = jnp.where(mask, cbT * jnp.exp(acsT[col:col + 1, :] - src[:, col:col + 1]), 0.0)
        xh = xT_g[hr, :]
        yh = jnp.dot(xh.astype(BF16), mT.astype(BF16), preferred_element_type=F32)
        yh = yh + y_off[hr, :] * eaT[col:col + 1, :]
        yT_ref[g * GROUP_ROWS + r * SSM_HEAD_DIM:g * GROUP_ROWS + (r + 1) * SSM_HEAD_DIM, tok] = yh
        xw.append((xh * wT[col:col + 1, :]).astype(BF16))
        scale.append(jnp.broadcast_to(etot[col:col + 1, :], (SSM_HEAD_DIM, SSM_STATE)))
    d_state = jnp.dot(jnp.concatenate(xw, axis=0), bm_g, preferred_element_type=F32)
    st_ref[g * GROUP_ROWS:(g + 1) * GROUP_ROWS, :] = st_g * jnp.concatenate(scale, axis=0) + d_state


def _ssd_scan_kernel(*refs, reverse, has_h0, finish, emit_state, state_aliased, state_slab, n_steps):
    refs = list(refs)
    xT_ref, bm_ref, cmT_ref, dt_ref, alog_ref = refs[:5]
    pos = 5
    h0_ref = None
    if has_h0:
        h0_ref = refs[pos]
        pos += 1
    if finish:
        (z_ref, ybT_ref, x_ref, mod_ref, g_ref, dexp_ref, ng_ref, wout_ref) = refs[pos:pos + 8]
        pos += 8
    if state_aliased:
        pos += 1
    y_out_ref = refs[pos]
    pos += 1
    state_out_ref = None
    if emit_state:
        state_out_ref = refs[pos]
        pos += 1
    st_ref = refs[pos]
    if finish:
        yT_ref, ybf_ref = refs[pos + 1:pos + 3]
    else:
        yT_ref = y_out_ref

    c = pl.program_id(1)

    @pl.when(c == 0)
    def _():
        if has_h0:
            st_ref[...] = h0_ref[0, 0, 0]
        else:
            st_ref[...] = jnp.zeros_like(st_ref)

    offsets = [k * SCAN_CHUNK for k in range(SCAN_STEP // SCAN_CHUNK)]
    if reverse:
        offsets.reverse()
    decays = [_chunk_decays(dt_ref, alog_ref, off, reverse) for off in offsets]
    for off, dec in zip(offsets, decays):
        ahead = _group_products(bm_ref, cmT_ref, st_ref, off, 0)
        for g in range(SSM_GROUPS):
            cur = ahead
            if g + 1 < SSM_GROUPS:
                ahead = _group_products(bm_ref, cmT_ref, st_ref, off, g + 1)
            _group_heads(dec, cur, xT_ref, yT_ref, st_ref, off, g, reverse)
        if finish:
            tok = slice(off, off + SCAN_CHUNK)
            yT = yT_ref[:, tok] + ybT_ref[:, tok] + dexp_ref[:, tok] * xT_ref[:, tok]
            y = yT.T * _silu(z_ref[tok, :])
            ybf_ref[tok, :] = _rms(y, ng_ref[...]).astype(BF16)

    if emit_state:
        @pl.when(c == n_steps - 1)
        def _():
            if state_slab is None:
                state_out_ref[0, 0, 0] = st_ref[...]
            else:
                state_out_ref[...] = jnp.zeros_like(state_out_ref)
                state_out_ref[0, state_slab[0], state_slab[1]] = st_ref[...]

    if finish:
        out = jnp.dot(ybf_ref[...], wout_ref[...], preferred_element_type=F32)
        _, _, gt = _mod_slices(mod_ref[0], 0)
        y_out_ref[...] = x_ref[...] + gt * _rms(out, g_ref[1:2])


def _ssd_scan(xT, bm, cmT, dt, p, layer, h0, fin, state, *, batch, reverse):
    q = SCAN_STEP
    n = xT.shape[0] * q
    direction = 1 if reverse else 0
    emit_state = state is not None
    nc = n // batch // q
    gn = SSM_GROUPS * SSM_STATE
    finish = fin is not None

    def tok(b, c):
        return b * nc + ((nc - 1 - c) if reverse else c)

    def row_blk(b, c):
        return (tok(b, c), 0)

    t_spec = pl.BlockSpec((None, SSM_D_INNER, q), lambda b, c: (tok(b, c), 0, 0))
    in_specs = [t_spec,
                pl.BlockSpec((q, gn), row_blk),
                pl.BlockSpec((None, gn, q), lambda b, c: (tok(b, c), 0, 0)),
                pl.BlockSpec((q, LANES), row_blk),
                _layer_spec(p["a_log"], layer)]
    args = [xT, bm, cmT, dt, p["a_log"]]
    if h0 is not None:
        in_specs.append(pl.BlockSpec((1, 1, 1, SSM_D_INNER, SSM_STATE),
                                     lambda b, c: (b, layer, direction, 0, 0)))
        args.append(h0)
    if finish:
        z, ybT, x, mod, g = fin
        n_mod = mod.shape[0]
        in_specs += [pl.BlockSpec((q, SSM_D_INNER), row_blk),
                     t_spec,
                     pl.BlockSpec((q, D_MODEL), row_blk),
                     pl.BlockSpec((1, 1, 6 * D_MODEL),
                                  (lambda b, c: (b, 0, 0)) if n_mod > 1 else (lambda b, c: (0, 0, 0))),
                     _const_spec(g.shape), _layer_spec(p["d_exp"], layer),
                     _layer_spec(p["norm_g"], layer), _layer_spec(p["w_out"], layer)]
        args += [z, ybT, x, mod, g, p["d_exp"], p["norm_g"], p["w_out"]]
        out_specs = [pl.BlockSpec((q, D_MODEL), row_blk)]
        out_shape = [jax.ShapeDtypeStruct((n, D_MODEL), F32)]
    else:
        out_specs = [t_spec]
        out_shape = [jax.ShapeDtypeStruct((n // q, SSM_D_INNER, q), F32)]
    aliases = {}
    state_slab = None
    if emit_state:
        buf, n_layers = state
        out_shape.append(jax.ShapeDtypeStruct((batch, n_layers, 2, SSM_D_INNER, SSM_STATE), F32))
        if buf is None:
            state_slab = (layer, direction)
            out_specs.append(pl.BlockSpec((1, n_layers, 2, SSM_D_INNER, SSM_STATE),
                                          lambda b, c: (b, 0, 0, 0, 0)))
        else:
            out_specs.append(pl.BlockSpec((1, 1, 1, SSM_D_INNER, SSM_STATE),
                                          lambda b, c: (b, layer, direction, 0, 0)))
            in_specs.append(pl.BlockSpec(memory_space=pl.ANY))
            args.append(buf)
            aliases = {len(args) - 1: 1}
    scratch = [pltpu.VMEM((SSM_D_INNER, SSM_STATE), F32)]
    if finish:
        scratch += [pltpu.VMEM((SSM_D_INNER, q), F32), pltpu.VMEM((q, SSM_D_INNER), BF16)]
    outs = pl.pallas_call(
        functools.partial(_ssd_scan_kernel, reverse=reverse, has_h0=h0 is not None, finish=finish,
                          emit_state=emit_state, state_aliased=bool(aliases), state_slab=state_slab,
                          n_steps=nc),
        grid=(batch, nc),
        in_specs=in_specs, out_specs=out_specs, out_shape=out_shape,
        scratch_shapes=scratch,
        input_output_aliases=aliases,
        compiler_params=_cparams(2),
        name="ssd_scan_bwd" if reverse else "ssd_scan_fwd",
    )(*args)
    return outs if emit_state else (outs[0], None)


def _ssd_layer(x, mod, g, p, layer, h0, state, batch, seq_len):
    z, xT, bm, cmT, dt = _ssd_in(x, mod, g, p, layer, seq_len)
    ybT, buf = _ssd_scan(xT, bm, cmT, dt, p, layer, h0, None, state, batch=batch, reverse=True)
    if state is not None:
        state = (buf, state[1])
    x_new, buf = _ssd_scan(xT, bm, cmT, dt, p, layer, h0, (z, ybT, x, mod, g), state,
                           batch=batch, reverse=False)
    return x_new, buf


def _qkv_kernel(*refs, rotary):
    kh_ref = None
    if rotary:
        x_ref, mod_ref, g_ref, w_ref, cos_ref, sin_ref, q_ref, k_ref, v_ref = refs
    else:
        x_ref, mod_ref, g_ref, w_ref, q_ref, k_ref, v_ref, kh_ref = refs
    d = D_MODEL
    sh, sc, _ = _mod_slices(mod_ref[0], 0)
    h = (_rms(x_ref[...], g_ref[0:1]) * (1.0 + sc) + sh).astype(BF16)
    if rotary:
        lane = lax.broadcasted_iota(jnp.int32, (x_ref.shape[0], LANES), 1)
        first_half = (lane & (DIFF_HEAD_DIM - 1)) < DIFF_HEAD_DIM // 2
        cos = cos_ref[...]
        sin = sin_ref[...]

    def rot(a):
        if not rotary:
            return a
        partner = jnp.where(first_half, pltpu.roll(a, LANES - DIFF_HEAD_DIM // 2, 1),
                            pltpu.roll(a, DIFF_HEAD_DIM // 2, 1))
        return a * cos + partner * sin

    scale = DIFF_HEAD_DIM ** -0.5 * math.log2(math.e)
    for j in range(d // LANES):
        cs = slice(j * LANES, (j + 1) * LANES)
        qj = jnp.dot(h, w_ref[:, j * LANES:(j + 1) * LANES], preferred_element_type=F32)
        kj = jnp.dot(h, w_ref[:, d + j * LANES:d + (j + 1) * LANES], preferred_element_type=F32)
        q_ref[:, cs] = (rot(qj) * scale).astype(q_ref.dtype)
        kj = rot(kj)
        if rotary:
            k_ref[:, cs] = kj.astype(k_ref.dtype)
        else:
            k_ref[cs, :] = kj.T.astype(k_ref.dtype)
        if kh_ref is not None:
            for i in range(LANES // DIFF_HEAD_DIM):
                kh_ref[:, j * (LANES // DIFF_HEAD_DIM) + i, :] = (
                    kj[:, i * DIFF_HEAD_DIM:(i + 1) * DIFF_HEAD_DIM])
    v_ref[...] = jnp.dot(h, w_ref[:, 2 * d:3 * d], preferred_element_type=F32).astype(v_ref.dtype)


def _qkv(x, mod, g, w_qkv, rot_tables, seq_len):
    n = x.shape[0]
    t = ROW_TILE
    rotary = rot_tables is not None
    in_specs = [pl.BlockSpec((t, D_MODEL), lambda i: (i, 0)), _mod_spec(mod.shape[0], t, seq_len),
                _const_spec(g.shape), _const_spec(w_qkv.shape)]
    args = [x, mod, g, w_qkv]
    if rotary:
        per_seq = seq_len // t
        in_specs += [pl.BlockSpec((t, LANES), lambda i: (i % per_seq, 0))] * 2
        args += list(rot_tables)
    row_spec = pl.BlockSpec((t, D_MODEL), lambda i: (i, 0))
    blk = max(seq_len, t)
    kT_spec = pl.BlockSpec((None, D_MODEL, t), lambda i: ((i * t) // blk, 0, ((i * t) % blk) // t))
    out_specs = [row_spec, row_spec if rotary else kT_spec, row_spec]
    out_shape = [jax.ShapeDtypeStruct((n, D_MODEL), BF16),
                 jax.ShapeDtypeStruct((n, D_MODEL) if rotary else (n // blk, D_MODEL, blk), BF16),
                 jax.ShapeDtypeStruct((n, D_MODEL), BF16 if rotary else F32)]
    if not rotary:
        out_specs.append(pl.BlockSpec((t, 2 * DIFF_HEADS, DIFF_HEAD_DIM), lambda i: (i, 0, 0)))
        out_shape.append(jax.ShapeDtypeStruct((n, 2 * DIFF_HEADS, DIFF_HEAD_DIM), F32))
    return pl.pallas_call(
        functools.partial(_qkv_kernel, rotary=rotary),
        grid=(n // t,),
        in_specs=in_specs,
        out_specs=out_specs,
        out_shape=out_shape,
        compiler_params=_cparams(1),
        name="diff_qkv",
    )(*args)


def _attn_kernel(*refs, has_cache, keys_transposed, lam_init):
    if has_cache:
        (q_ref, k_ref, v_ref, ck_ref, cv_ref, x_ref, mod_ref, g_ref, lp_ref, sg_ref, wo_ref,
         o_ref, oall_ref) = refs
    else:
        q_ref, k_ref, v_ref, x_ref, mod_ref, g_ref, lp_ref, sg_ref, wo_ref, o_ref, oall_ref = refs
    dn = (((1,), (1,)), ((), ()))
    lp = lp_ref[...]
    lam = (jnp.exp(jnp.sum(lp[0:1] * lp[1:2], axis=-1, keepdims=True))
           - jnp.exp(jnp.sum(lp[2:3] * lp[3:4], axis=-1, keepdims=True)) + lam_init)
    hd = DIFF_HEAD_DIM

    def scores(hh):
        hs = slice(hh * hd, (hh + 1) * hd)
        qh = q_ref[:, hs]
        if keys_transposed:
            s_new = jnp.dot(qh, k_ref[hs, :], preferred_element_type=F32)
        else:
            s_new = lax.dot_general(qh, k_ref[:, hs], dn, preferred_element_type=F32)
        s_old = None
        if has_cache:
            s_old = lax.dot_general(qh, ck_ref[0, :, hs].astype(BF16), dn, preferred_element_type=F32)
        return s_old, s_new

    def probs(s_old, s_new):
        m = jnp.max(s_new, axis=-1, keepdims=True)
        if has_cache:
            m = jnp.maximum(m, jnp.max(s_old, axis=-1, keepdims=True))
            p_old = jnp.exp2(s_old - m)
        p_new = jnp.exp2(s_new - m)
        l = jnp.sum(p_new, axis=-1, keepdims=True)
        if has_cache:
            l = l + jnp.sum(p_old, axis=-1, keepdims=True)
            return p_old, p_new, l
        return None, p_new, l

    def diff_weights(pair_scores):
        po0, pn0, l0 = probs(*pair_scores[0])
        po1, pn1, l1 = probs(*pair_scores[1])
        ratio = lam * l0 / l1
        att_old = (po0 - po1 * ratio).astype(BF16) if has_cache else None
        return att_old, (pn0 - pn1 * ratio).astype(BF16), l0

    def weighted_values(hp, att_old, att, l0):
        vs = slice(hp * 2 * hd, (hp + 1) * 2 * hd)
        o = jnp.dot(att, v_ref[:, vs].astype(BF16), preferred_element_type=F32)
        if has_cache:
            o = o + jnp.dot(att_old, cv_ref[0, :, vs].astype(BF16), preferred_element_type=F32)
        o = o * (1.0 / l0)
        oall_ref[:, vs] = (_rms(o, sg_ref[...]) * (1.0 - lam_init)).astype(BF16)

    sc, att = {}, {}
    for step in range(DIFF_HEADS + 2):
        if step < DIFF_HEADS:
            sc[step] = (scores(2 * step), scores(2 * step + 1))
        if step >= 2:
            weighted_values(step - 2, *att.pop(step - 2))
        if 1 <= step <= DIFF_HEADS:
            att[step - 1] = diff_weights(sc.pop(step - 1))

    m_out = jnp.dot(oall_ref[...], wo_ref[...], preferred_element_type=F32)
    _, _, gt = _mod_slices(mod_ref[0], 0)
    o_ref[...] = x_ref[...] + gt * _rms(m_out, g_ref[1:2])


def _attention(q, k, v, cache, x, mod, g, lam_p, subln_g, w_o, *, batch, lam_init):
    n = x.shape[0]
    seq = n // batch
    tq = min(ATTN_Q_TILE, seq)
    nq = seq // tq
    has_cache = cache is not None
    keys_transposed = k.ndim == 3
    v_spec = pl.BlockSpec((seq, D_MODEL), lambda b, t: (b, 0))
    if keys_transposed:
        blk = k.shape[2]
        k_spec = pl.BlockSpec((None, D_MODEL, seq),
                              lambda b, t: ((b * seq) // blk, 0, ((b * seq) % blk) // seq))
    else:
        k_spec = v_spec
    in_specs = [pl.BlockSpec((tq, D_MODEL), lambda b, t: (b * nq + t, 0)), k_spec, v_spec]
    args = [q, k, v]
    if has_cache:
        past = cache[0].shape[1]
        c_spec = pl.BlockSpec((1, past, D_MODEL), lambda b, t: (b, 0, 0))
        in_specs += [c_spec, c_spec]
        args += list(cache)
    n_mod = mod.shape[0]
    in_specs += [pl.BlockSpec((tq, D_MODEL), lambda b, t: (b * nq + t, 0)),
                 pl.BlockSpec((1, 1, 6 * D_MODEL),
                              (lambda b, t: (b, 0, 0)) if n_mod > 1 else (lambda b, t: (0, 0, 0))),
                 _const_spec(g.shape), _const_spec(lam_p.shape), _const_spec(subln_g.shape),
                 _const_spec(w_o.shape)]
    args += [x, mod, g, lam_p, subln_g, w_o]
    return pl.pallas_call(
        functools.partial(_attn_kernel, has_cache=has_cache, keys_transposed=keys_transposed,
                          lam_init=lam_init),
        grid=(batch, nq),
        in_specs=in_specs,
        out_specs=pl.BlockSpec((tq, D_MODEL), lambda b, t: (b * nq + t, 0)),
        out_shape=jax.ShapeDtypeStruct((n, D_MODEL), F32),
        scratch_shapes=[pltpu.VMEM((tq, D_MODEL), BF16)],
        compiler_params=_cparams(2),
        name="diff_attention",
    )(*args)


def _rotary_tables(n_tokens):
    rows = n_tokens // GRID_W
    row = jnp.repeat(jnp.arange(rows, dtype=F32), GRID_W)
    col = jnp.tile(jnp.arange(GRID_W, dtype=F32), rows)
    inv = ROPE_THETA ** (-jnp.arange(ROT_PAIRS_PER_AXIS, dtype=F32) / ROT_PAIRS_PER_AXIS)
    ang = jnp.concatenate([row[:, None] * inv, col[:, None] * inv], axis=-1)
    cos, sin = jnp.cos(ang), jnp.sin(ang)
    reps = LANES // DIFF_HEAD_DIM
    return (jnp.tile(jnp.concatenate([cos, cos], axis=-1), (1, reps)),
            jnp.tile(jnp.concatenate([-sin, sin], axis=-1), (1, reps)))


def kernel(x_prompt, x_sample, state_ssm, cache_k, cache_v, c, c_ctx, w_mod, b_mod, norm_g, ssd_w_in, ssd_conv_w, ssd_conv_b, ssd_dt_bias, ssd_a_log, ssd_d, ssd_norm_g, ssd_w_out, sc_w_in, sc_conv_w, sc_w_out, da_w_qkv, da_lambda, da_subln_g, da_w_out, ffn_w_up, ffn_conv_w, ffn_w_down):
    bp, lp_, d = x_prompt.shape
    bs, ls, _ = x_sample.shape
    xp = x_prompt.reshape(bp * lp_, d)
    xs = x_sample.reshape(bs * ls, d)

    n_cond = 1 + bs
    cond = jnp.concatenate([c_ctx[None], c, jnp.zeros((HALO - n_cond, d), F32)], axis=0)
    mod_all = _modulation(cond, w_mod, b_mod)

    rot = _rotary_tables(ls)
    n_ssd_layers = (DEPTH + 2) // N_MIXERS
    pad = LANES - 2 * SSM_HEADS
    xbc_end = SSM_D_INNER + SSM_CONV_DIM
    ssd = {
        "w_in": ssd_w_in.astype(BF16),
        "w_dt": jnp.pad(ssd_w_in[:, :, xbc_end:], ((0, 0), (0, 0), (0, pad))).astype(BF16),
        "conv_w": ssd_conv_w,
        "conv_b": ssd_conv_b[:, None],
        "dt_bias": jnp.pad(ssd_dt_bias.reshape(n_ssd_layers, 1, -1), ((0, 0), (0, 0), (0, pad))),
        "a_log": jnp.pad(ssd_a_log.reshape(n_ssd_layers, 1, -1), ((0, 0), (0, 0), (0, pad))),
        "d_exp": jnp.broadcast_to(jnp.repeat(ssd_d, SSM_HEAD_DIM, axis=1)[:, :, None],
                                  (n_ssd_layers, SSM_D_INNER, SCAN_STEP)),
        "norm_g": ssd_norm_g[:, None],
        "w_out": ssd_w_out.astype(BF16),
    }
    h0 = state_ssm.reshape(bs, n_ssd_layers, 2, SSM_D_INNER, SSM_STATE)
    w_up_all, w_dn_all = ffn_w_up.astype(BF16), ffn_w_down.astype(BF16)
    ssm_buf = None
    new_k, new_v = [], []
    for l in range(DEPTH):
        j = l // N_MIXERS
        kind = l % N_MIXERS
        mod_p = mod_all[l, 0:1][:, None]
        mod_s = mod_all[l, 1:n_cond][:, None]
        g = norm_g[l]
        if kind == 0:
            xp, ssm_buf = _ssd_layer(xp, mod_p, g, ssd, j, None, (ssm_buf, n_ssd_layers), bp, lp_)
            xs, _ = _ssd_layer(xs, mod_s, g, ssd, j, h0, None, bs, ls)
        elif kind == 1:
            w_in, w_out = sc_w_in[j].astype(BF16), sc_w_out[j].astype(BF16)
            xp = _sconv(xp, mod_p, g, w_in, sc_conv_w[j], w_out, lp_)
            xs = _sconv(xs, mod_s, g, w_in, sc_conv_w[j], w_out, ls)
        else:
            lam_init = 0.8 - 0.6 * math.exp(-0.3 * l)
            w_qkv, w_o = da_w_qkv[j].astype(BF16), da_w_out[j].astype(BF16)
            sg = da_subln_g[j][None]
            qp, kp, vp, kp_heads = _qkv(xp, mod_p, g, w_qkv, None, lp_)
            xp = _attention(qp, kp, vp, None, xp, mod_p, g, da_lambda[j], sg, w_o,
                            batch=bp, lam_init=lam_init)
            new_k.append(kp_heads.reshape(bp, lp_, 2 * DIFF_HEADS, DIFF_HEAD_DIM))
            new_v.append(vp.reshape(bp, lp_, DIFF_HEADS, 2 * DIFF_HEAD_DIM))
            qs, ks, vs = _qkv(xs, mod_s, g, w_qkv, rot, ls)
            cache = (cache_k[:, j].reshape(bs, -1, d), cache_v[:, j].reshape(bs, -1, d))
            xs = _attention(qs, ks, vs, cache, xs, mod_s, g, da_lambda[j], sg, w_o,
                            batch=bs, lam_init=lam_init)
        xp = _ffn(xp, mod_p, g, l, w_up_all, ffn_conv_w, w_dn_all, lp_)
        xs = _ffn(xs, mod_s, g, l, w_up_all, ffn_conv_w, w_dn_all, ls)

    new_ssm = ssm_buf.reshape(bp, n_ssd_layers, 2, SSM_HEADS, SSM_HEAD_DIM, SSM_STATE)
    return (xp.reshape(bp, lp_, d), xs.reshape(bs, ls, d), new_ssm,
            jnp.stack(new_k, axis=1), jnp.stack(new_v, axis=1))
```

```python
import functools
import math

import jax
import jax.numpy as jnp
from jax import lax
from jax.experimental import pallas as pl
from jax.experimental.pallas import tpu as pltpu

D_MODEL = 1024
DEPTH = 4
GRID_W = 64
N_MIXERS = 3
SSM_D_INNER = 2 * D_MODEL
SSM_HEAD_DIM = 64
SSM_HEADS = SSM_D_INNER // SSM_HEAD_DIM
SSM_GROUPS = 8
SSM_STATE = 128
SSM_CONV_DIM = SSM_D_INNER + 2 * SSM_GROUPS * SSM_STATE
HEADS_PER_GROUP = SSM_HEADS // SSM_GROUPS
GROUP_ROWS = HEADS_PER_GROUP * SSM_HEAD_DIM
DIFF_HEAD_DIM = 64
DIFF_HEADS = D_MODEL // (2 * DIFF_HEAD_DIM)
ROPE_THETA = 10000.0
ROT_PAIRS_PER_AXIS = DIFF_HEAD_DIM // 4
FFN_DIM = 2816
NORM_EPS = 1e-6

LANES = 128
HALO = 8
SCAN_CHUNK = 128
SCAN_STEP = 256
ROW_TILE = 512
ATTN_Q_TILE = 256
COL_CHUNK = 256
FFN_DOWN_GROUP = 4
FFN_DOWN_LAG = 3
VMEM_LIMIT = 56 * 1024 * 1024

F32 = jnp.float32
BF16 = jnp.bfloat16


def _cparams(n_axes):
    return pltpu.CompilerParams(dimension_semantics=("arbitrary",) * n_axes,
                                vmem_limit_bytes=VMEM_LIMIT)


def _rms(x, g):
    ms = jnp.mean(x * x, axis=-1, keepdims=True)
    return x * lax.rsqrt(ms + NORM_EPS) * g


def _silu(x):
    return x * jax.nn.sigmoid(x)


def _mod_slices(mod, first):
    d = D_MODEL
    return (mod[:, first * d:(first + 1) * d], mod[:, (first + 1) * d:(first + 2) * d],
            mod[:, (first + 2) * d:(first + 3) * d])


def _ext_rows(tile_rows, seq_len):
    return tile_rows if seq_len <= tile_rows else tile_rows + 2 * HALO


def _main_rows(tile_rows, seq_len):
    return slice(0, tile_rows) if seq_len <= tile_rows else slice(HALO, HALO + tile_rows)


def _ext_norm(xp_ref, x_ref, xn_ref, g, sc, sh, seq_len):
    t = x_ref.shape[0]
    gs = g * (1.0 + sc)
    if seq_len <= t:
        return (_rms(x_ref[...], gs) + sh).astype(BF16)
    tiles_per_seq = seq_len // t
    k = pl.program_id(0) % tiles_per_seq
    keep = (jnp.where(k == 0, 0.0, 1.0), None, jnp.where(k == tiles_per_seq - 1, 0.0, 1.0))
    pieces = []
    for r, kp in zip((xp_ref, x_ref, xn_ref), keep):
        h = _rms(r[...], gs) + sh
        pieces.append(h if kp is None else h * kp)
    return jnp.concatenate(pieces, axis=0).astype(BF16)


def _conv3(u, w3, tile_rows, seq_len):
    if u.shape[0] == tile_rows:
        return _conv_rows(u, w3, 0, tile_rows, seq_len, first_row=0)
    return _conv_rows(u, w3, HALO, tile_rows, seq_len)


def _conv_rows(u, w3, off, n_out, seq_len, first_row=None):
    n = u.shape[0]
    rows = slice(off, off + n_out)
    up = pltpu.roll(u, 1, 0)[rows]
    un = pltpu.roll(u, n - 1, 0)[rows]
    if first_row is not None:
        pos = (lax.broadcasted_iota(jnp.int32, (n_out, u.shape[1]), 0) + first_row) & (seq_len - 1)
        up = jnp.where(pos != 0, up, 0.0)
        un = jnp.where(pos != seq_len - 1, un, 0.0)
    return up * w3[0:1] + u[rows] * w3[1:2] + un * w3[2:3]


def _halo_specs(n_rows, tile_rows, width):
    per = tile_rows // HALO
    last = n_rows // HALO - 1
    return [
        pl.BlockSpec((HALO, width), lambda i: (jnp.maximum(i * per - 1, 0), 0)),
        pl.BlockSpec((tile_rows, width), lambda i: (i, 0)),
        pl.BlockSpec((HALO, width), lambda i: (jnp.minimum((i + 1) * per, last), 0)),
    ]


def _mod_spec(n_mod, tile_rows, seq_len):
    if n_mod == 1:
        return pl.BlockSpec((1, 1, 6 * D_MODEL), lambda i: (0, 0, 0))
    return pl.BlockSpec((1, 1, 6 * D_MODEL), lambda i: ((i * tile_rows) // seq_len, 0, 0))


def _const_spec(shape):
    nd = len(shape)
    return pl.BlockSpec(shape, lambda *_: (0,) * nd)


def _layer_spec(stacked, layer):
    shape = stacked.shape[1:]
    idx = (layer,) + (0,) * len(shape)
    return pl.BlockSpec((None,) + shape, lambda *_: idx, pipeline_mode=pl.Buffered(1))


def _mod_kernel(cond_ref, w_ref, b_ref, o_ref):
    a = _silu(cond_ref[...]).astype(BF16)
    o_ref[0] = jnp.dot(a, w_ref[0].astype(BF16), preferred_element_type=F32) + b_ref[0]


def _modulation(cond, w_mod, b_mod):
    n_cols = 6 * D_MODEL
    blk = n_cols // 4
    return pl.pallas_call(
        _mod_kernel,
        grid=(DEPTH, n_cols // blk),
        in_specs=[_const_spec(cond.shape),
                  pl.BlockSpec((1, D_MODEL, blk), lambda l, j: (l, 0, j)),
                  pl.BlockSpec((1, 1, blk), lambda l, j: (l, 0, j))],
        out_specs=pl.BlockSpec((1, cond.shape[0], blk), lambda l, j: (l, 0, j)),
        out_shape=jax.ShapeDtypeStruct((DEPTH, cond.shape[0], n_cols), F32),
        compiler_params=_cparams(2),
        name="modulation",
    )(cond, w_mod, b_mod.reshape(DEPTH, 1, n_cols))


def _ffn_kernel(xp_ref, x_ref, xn_ref, mod_ref, g_ref, wup_ref, cw_ref, wdn_ref, o_ref,
                hext_ref, act_ref, *, seq_len):
    t = x_ref.shape[0]
    sh, sc, gt = _mod_slices(mod_ref[0], 3)
    hext_ref[...] = _ext_norm(xp_ref, x_ref, xn_ref, g_ref[2:3], sc, sh, seq_len)
    n_chunks = FFN_DIM // COL_CHUNK

    slabs = [(c0, min(c0 + FFN_DOWN_GROUP, n_chunks - 1)) for c0 in range(0, n_chunks - 1, FFN_DOWN_GROUP)]
    issue_at = {}
    for c0, c1 in slabs:
        issue_at.setdefault(min(c1 - 1 + FFN_DOWN_LAG, n_chunks - 1), []).append((c0, c1))

    acc = None

    def down(c0, c1):
        k0, k1 = c0 * COL_CHUNK, c1 * COL_CHUNK
        part = jnp.dot(act_ref[:, k0:k1], wdn_ref[k0:k1, :], preferred_element_type=F32)
        return part if acc is None else acc + part

    for j in range(n_chunks):
        cg = j * COL_CHUNK
        cv = FFN_DIM + cg
        h = hext_ref[...]
        ug = jnp.dot(h, wup_ref[:, cg:cg + COL_CHUNK], preferred_element_type=F32)
        uv = jnp.dot(h, wup_ref[:, cv:cv + COL_CHUNK], preferred_element_type=F32)
        for c0, c1 in issue_at.get(j, []):
            acc = down(c0, c1)
        gate = _conv3(ug, cw_ref[:, cg:cg + COL_CHUNK], t, seq_len)
        val = _conv3(uv, cw_ref[:, cv:cv + COL_CHUNK], t, seq_len)
        act_ref[:, cg:cg + COL_CHUNK] = (_silu(gate) * val).astype(BF16)
    acc = down(n_chunks - 1, n_chunks)
    o_ref[...] = x_ref[...] + _rms(acc, gt * g_ref[3:4])


def _ffn(x, mod, g, layer, w_up, conv_w, w_down, seq_len):
    n = x.shape[0]
    t = ROW_TILE
    return pl.pallas_call(
        functools.partial(_ffn_kernel, seq_len=seq_len),
        grid=(n // t,),
        in_specs=_halo_specs(n, t, D_MODEL) + [
            _mod_spec(mod.shape[0], t, seq_len), _const_spec(g.shape), _layer_spec(w_up, layer),
            _layer_spec(conv_w, layer), _layer_spec(w_down, layer)],
        out_specs=pl.BlockSpec((t, D_MODEL), lambda i: (i, 0)),
        out_shape=jax.ShapeDtypeStruct((n, D_MODEL), F32),
        scratch_shapes=[pltpu.VMEM((_ext_rows(t, seq_len), D_MODEL), BF16),
                        pltpu.VMEM((t, FFN_DIM), BF16)],
        compiler_params=_cparams(1),
        name="conv_ffn",
    )(x, x, x, mod, g, w_up, conv_w, w_down)


def _sconv_kernel(xp_ref, x_ref, xn_ref, mod_ref, g_ref, win_ref, cw_ref, wout_ref, o_ref,
                  hext_ref, act_ref, *, seq_len):
    t = x_ref.shape[0]
    d = D_MODEL
    sh, sc, gt = _mod_slices(mod_ref[0], 0)
    hext_ref[...] = _ext_norm(xp_ref, x_ref, xn_ref, g_ref[0:1], sc, sh, seq_len)
    for j in range(d // COL_CHUNK):
        c0 = j * COL_CHUNK
        bg = jnp.dot(hext_ref[_main_rows(t, seq_len), :], win_ref[:, c0:c0 + COL_CHUNK],
                     preferred_element_type=F32)
        h = hext_ref[...]
        cg = jnp.dot(h, win_ref[:, d + c0:d + c0 + COL_CHUNK], preferred_element_type=F32)
        u = jnp.dot(h, win_ref[:, 2 * d + c0:2 * d + c0 + COL_CHUNK], preferred_element_type=F32)
        conv = _conv3(cg * u, cw_ref[:, c0:c0 + COL_CHUNK], t, seq_len)
        act_ref[:, c0:c0 + COL_CHUNK] = (bg * conv).astype(BF16)
    m = jnp.dot(act_ref[...], wout_ref[...], preferred_element_type=F32)
    o_ref[...] = x_ref[...] + _rms(m, gt * g_ref[1:2])


def _sconv(x, mod, g, w_in, conv_w, w_out, seq_len):
    n = x.shape[0]
    t = ROW_TILE
    return pl.pallas_call(
        functools.partial(_sconv_kernel, seq_len=seq_len),
        grid=(n // t,),
        in_specs=_halo_specs(n, t, D_MODEL) + [
            _mod_spec(mod.shape[0], t, seq_len), _const_spec(g.shape), _const_spec(w_in.shape),
            _const_spec(conv_w.shape), _const_spec(w_out.shape)],
        out_specs=pl.BlockSpec((t, D_MODEL), lambda i: (i, 0)),
        out_shape=jax.ShapeDtypeStruct((n, D_MODEL), F32),
        scratch_shapes=[pltpu.VMEM((_ext_rows(t, seq_len), D_MODEL), BF16),
                        pltpu.VMEM((t, D_MODEL), BF16)],
        compiler_params=_cparams(1),
        name="short_conv_mixer",
    )(x, x, x, mod, g, w_in, conv_w, w_out)


def _ssd_in_kernel(xp_ref, x_ref, xn_ref, mod_ref, g_ref, win_ref, wdt_ref, cw_ref, cb_ref, dtb_ref,
                   z_ref, xT_ref, bm_ref, cmT_ref, dt_ref, hext_ref, *, seq_len):
    t = x_ref.shape[0]
    di = SSM_D_INNER
    gn = SSM_GROUPS * SSM_STATE
    sh, sc, _ = _mod_slices(mod_ref[0], 0)
    hext_ref[...] = _ext_norm(xp_ref, x_ref, xn_ref, g_ref[0:1], sc, sh, seq_len)

    raw = jnp.dot(hext_ref[_main_rows(t, seq_len), :], wdt_ref[...],
                  preferred_element_type=F32) + dtb_ref[...]
    dt_ref[...] = jnp.maximum(raw, 0.0) + jnp.log1p(jnp.exp(-jnp.abs(raw)))

    n_conv = SSM_CONV_DIM // COL_CHUNK
    n_z = di // COL_CHUNK
    for j in range(n_conv):
        c0 = j * COL_CHUNK
        u = jnp.dot(hext_ref[...], win_ref[:, di + c0:di + c0 + COL_CHUNK], preferred_element_type=F32)
        if j % (n_conv // n_z) == 0:
            zc = j // (n_conv // n_z) * COL_CHUNK
            z_ref[:, zc:zc + COL_CHUNK] = jnp.dot(hext_ref[_main_rows(t, seq_len), :],
                                                  win_ref[:, zc:zc + COL_CHUNK], preferred_element_type=F32)
        act = _silu(_conv3(u, cw_ref[:, c0:c0 + COL_CHUNK], t, seq_len) + cb_ref[:, c0:c0 + COL_CHUNK])
        if c0 < di:
            xT_ref[c0:c0 + COL_CHUNK, :] = act.T
        elif c0 < di + gn:
            bm_ref[:, c0 - di:c0 - di + COL_CHUNK] = act.astype(BF16)
        else:
            cmT_ref[c0 - di - gn:c0 - di - gn + COL_CHUNK, :] = act.T.astype(BF16)


def _ssd_in(x, mod, g, p, layer, seq_len):
    n = x.shape[0]
    t = SCAN_STEP
    gn = SSM_GROUPS * SSM_STATE
    weights = [p[k] for k in ("w_in", "w_dt", "conv_w", "conv_b", "dt_bias")]
    return pl.pallas_call(
        functools.partial(_ssd_in_kernel, seq_len=seq_len),
        grid=(n // t,),
        in_specs=_halo_specs(n, t, D_MODEL) + [
            _mod_spec(mod.shape[0], t, seq_len), _const_spec(g.shape)]
        + [_layer_spec(w, layer) for w in weights],
        out_specs=[pl.BlockSpec((t, SSM_D_INNER), lambda i: (i, 0)),
                   pl.BlockSpec((None, SSM_D_INNER, t), lambda i: (i, 0, 0)),
                   pl.BlockSpec((t, gn), lambda i: (i, 0)),
                   pl.BlockSpec((None, gn, t), lambda i: (i, 0, 0)),
                   pl.BlockSpec((t, LANES), lambda i: (i, 0))],
        out_shape=[jax.ShapeDtypeStruct((n, SSM_D_INNER), F32),
                   jax.ShapeDtypeStruct((n // t, SSM_D_INNER, t), F32),
                   jax.ShapeDtypeStruct((n, gn), BF16),
                   jax.ShapeDtypeStruct((n // t, gn, t), BF16),
                   jax.ShapeDtypeStruct((n, LANES), F32)],
        scratch_shapes=[pltpu.VMEM((_ext_rows(t, seq_len), D_MODEL), BF16)],
        compiler_params=_cparams(1),
        name="ssd_in_proj",
    )(x, x, x, mod, g, *weights)


def _split3(v):
    hi = v.astype(BF16)
    r1 = v - hi.astype(F32)
    mid = r1.astype(BF16)
    lo = (r1 - mid.astype(F32)).astype(BF16)
    return hi, mid, lo


def _chunk_decays(dt_ref, alog_ref, off, reverse):
    q = SCAN_CHUNK
    dt = dt_ref[off:off + q, :]
    da = dt * (-jnp.exp(alog_ref[...]))
    r_i = lax.broadcasted_iota(jnp.int32, (q, q), 0)
    c_i = lax.broadcasted_iota(jnp.int32, (q, q), 1)
    tri = jnp.where((c_i >= r_i) if reverse else (c_i <= r_i), 1.0, 0.0).astype(BF16)
    acs = sum(jnp.dot(tri, part, preferred_element_type=F32) for part in _split3(da))
    mask = (r_i >= c_i) if reverse else (r_i <= c_i)
    acsT = acs.T
    dtT = dt.T
    src = acs - jnp.log(dt)
    last = 0 if reverse else q - 1
    totT = acsT[:, last:last + 1]
    wT = jnp.exp(totT - acsT) * dtT
    eaT = jnp.exp(acsT)
    etot = jnp.broadcast_to(jnp.exp(totT), (LANES, SSM_STATE))
    return mask, acsT, src, wT, eaT, etot


def _group_products(bm_ref, cmT_ref, st_ref, off, g):
    tok = slice(off, off + SCAN_CHUNK)
    ns = slice(g * SSM_STATE, (g + 1) * SSM_STATE)
    bm_g = bm_ref[tok, ns]
    cmT_g = cmT_ref[ns, tok]
    cbT = jnp.dot(bm_g, cmT_g, preferred_element_type=F32)
    st_g = st_ref[g * GROUP_ROWS:(g + 1) * GROUP_ROWS, :]
    y_off = jnp.dot(st_g.astype(BF16), cmT_g, preferred_element_type=F32)
    return bm_g, cbT, st_g, y_off


def _group_heads(decays, products, xT_ref, yT_ref, st_ref, off, g, reverse):
    mask, acsT, src, wT, eaT, etot = decays
    bm_g, cbT, st_g, y_off = products
    tok = slice(off, off + SCAN_CHUNK)
    xT_g = xT_ref[g * GROUP_ROWS:(g + 1) * GROUP_ROWS, tok]
    head0 = (SSM_HEADS if reverse else 0) + g * HEADS_PER_GROUP
    xw, scale = [], []
    for r in range(HEADS_PER_GROUP):
        col = head0 + r
        hr = slice(r * SSM_HEAD_DIM, (r + 1) * SSM_HEAD_DIM)
        mT = jnp.where(mask, cbT * jnp.exp(acsT[col:col + 1, :] - src[:, col:col + 1]), 0.0)
        xh = xT_g[hr, :]
        yh = jnp.dot(xh.astype(BF16), mT.astype(BF16), preferred_element_type=F32)
        yh = yh + y_off[hr, :] * eaT[col:col + 1, :]
        yT_ref[g * GROUP_ROWS + r * SSM_HEAD_DIM:g * GROUP_ROWS + (r + 1) * SSM_HEAD_DIM, tok] = yh
        xw.append((xh * wT[col:col + 1, :]).astype(BF16))
        scale.append(jnp.broadcast_to(etot[col:col + 1, :], (SSM_HEAD_DIM, SSM_STATE)))
    d_state = jnp.dot(jnp.concatenate(xw, axis=0), bm_g, preferred_element_type=F32)
    st_ref[g * GROUP_ROWS:(g + 1) * GROUP_ROWS, :] = st_g * jnp.concatenate(scale, axis=0) + d_state


def _ssd_scan_kernel(*refs, reverse, has_h0, finish, emit_state, state_aliased, state_slab, n_steps):
    refs = list(refs)
    xT_ref, bm_ref, cmT_ref, dt_ref, alog_ref = refs[:5]
    pos = 5
    h0_ref = None
    if has_h0:
        h0_ref = refs[pos]
        pos += 1
    if finish:
        (z_ref, ybT_ref, x_ref, mod_ref, g_ref, dexp_ref, ng_ref, wout_ref) = refs[pos:pos + 8]
        pos += 8
    if state_aliased:
        pos += 1
    y_out_ref = refs[pos]
    pos += 1
    state_out_ref = None
    if emit_state:
        state_out_ref = refs[pos]
        pos += 1
    st_ref = refs[pos]
    if finish:
        yT_ref, ybf_ref = refs[pos + 1:pos + 3]
    else:
        yT_ref = y_out_ref

    c = pl.program_id(1)

    @pl.when(c == 0)
    def _():
        if has_h0:
            st_ref[...] = h0_ref[0, 0, 0]
        else:
            st_ref[...] = jnp.zeros_like(st_ref)

    offsets = [k * SCAN_CHUNK for k in range(SCAN_STEP // SCAN_CHUNK)]
    if reverse:
        offsets.reverse()
    decays = [_chunk_decays(dt_ref, alog_ref, off, reverse) for off in offsets]
    for off, dec in zip(offsets, decays):
        ahead = _group_products(bm_ref, cmT_ref, st_ref, off, 0)
        for g in range(SSM_GROUPS):
            cur = ahead
            if g + 1 < SSM_GROUPS:
                ahead = _group_products(bm_ref, cmT_ref, st_ref, off, g + 1)
            _group_heads(dec, cur, xT_ref, yT_ref, st_ref, off, g, reverse)
        if finish:
            tok = slice(off, off + SCAN_CHUNK)
            yT = yT_ref[:, tok] + ybT_ref[:, tok] + dexp_ref[:, tok] * xT_ref[:, tok]
            y = yT.T * _silu(z_ref[tok, :])
            ybf_ref[tok, :] = _rms(y, ng_ref[...]).astype(BF16)

    if emit_state:
        @pl.when(c == n_steps - 1)
        def _():
            if state_slab is None:
                state_out_ref[0, 0, 0] = st_ref[...]
            else:
                state_out_ref[...] = jnp.zeros_like(state_out_ref)
                state_out_ref[0, state_slab[0], state_slab[1]] = st_ref[...]

    if finish:
        out = jnp.dot(ybf_ref[...], wout_ref[...], preferred_element_type=F32)
        _, _, gt = _mod_slices(mod_ref[0], 0)
        y_out_ref[...] = x_ref[...] + _rms(out, gt * g_ref[1:2])


def _ssd_scan(xT, bm, cmT, dt, p, layer, h0, fin, state, *, batch, reverse):
    q = SCAN_STEP
    n = xT.shape[0] * q
    direction = 1 if reverse else 0
    emit_state = state is not None
    nc = n // batch // q
    gn = SSM_GROUPS * SSM_STATE
    finish = fin is not None

    def tok(b, c):
        return b * nc + ((nc - 1 - c) if reverse else c)

    def row_blk(b, c):
        return (tok(b, c), 0)

    t_spec = pl.BlockSpec((None, SSM_D_INNER, q), lambda b, c: (tok(b, c), 0, 0))
    in_specs = [t_spec,
                pl.BlockSpec((q, gn), row_blk),
                pl.BlockSpec((None, gn, q), lambda b, c: (tok(b, c), 0, 0)),
                pl.BlockSpec((q, LANES), row_blk),
                _layer_spec(p["a_log"], layer)]
    args = [xT, bm, cmT, dt, p["a_log"]]
    if h0 is not None:
        in_specs.append(pl.BlockSpec((1, 1, 1, SSM_D_INNER, SSM_STATE),
                                     lambda b, c: (b, layer, direction, 0, 0)))
        args.append(h0)
    if finish:
        z, ybT, x, mod, g = fin
        n_mod = mod.shape[0]
        in_specs += [pl.BlockSpec((q, SSM_D_INNER), row_blk),
                     t_spec,
                     pl.BlockSpec((q, D_MODEL), row_blk),
                     pl.BlockSpec((1, 1, 6 * D_MODEL),
                                  (lambda b, c: (b, 0, 0)) if n_mod > 1 else (lambda b, c: (0, 0, 0))),
                     _const_spec(g.shape), _layer_spec(p["d_exp"], layer),
                     _layer_spec(p["norm_g"], layer), _layer_spec(p["w_out"], layer)]
        args += [z, ybT, x, mod, g, p["d_exp"], p["norm_g"], p["w_out"]]
        out_specs = [pl.BlockSpec((q, D_MODEL), row_blk)]
        out_shape = [jax.ShapeDtypeStruct((n, D_MODEL), F32)]
    else:
        out_specs = [t_spec]
        out_shape = [jax.ShapeDtypeStruct((n // q, SSM_D_INNER, q), F32)]
    aliases = {}
    state_slab = None
    if emit_state:
        buf, n_layers = state
        out_shape.append(jax.ShapeDtypeStruct((batch, n_layers, 2, SSM_D_INNER, SSM_STATE), F32))
        if buf is None:
            state_slab = (layer, direction)
            out_specs.append(pl.BlockSpec((1, n_layers, 2, SSM_D_INNER, SSM_STATE),
                                          lambda b, c: (b, 0, 0, 0, 0)))
        else:
            out_specs.append(pl.BlockSpec((1, 1, 1, SSM_D_INNER, SSM_STATE),
                                          lambda b, c: (b, layer, direction, 0, 0)))
            in_specs.append(pl.BlockSpec(memory_space=pl.ANY))
            args.append(buf)
            aliases = {len(args) - 1: 1}
    scratch = [pltpu.VMEM((SSM_D_INNER, SSM_STATE), F32)]
    if finish:
        scratch += [pltpu.VMEM((SSM_D_INNER, q), F32), pltpu.VMEM((q, SSM_D_INNER), BF16)]
    outs = pl.pallas_call(
        functools.partial(_ssd_scan_kernel, reverse=reverse, has_h0=h0 is not None, finish=finish,
                          emit_state=emit_state, state_aliased=bool(aliases), state_slab=state_slab,
                          n_steps=nc),
        grid=(batch, nc),
        in_specs=in_specs, out_specs=out_specs, out_shape=out_shape,
        scratch_shapes=scratch,
        input_output_aliases=aliases,
        compiler_params=_cparams(2),
        name="ssd_scan_bwd" if reverse else "ssd_scan_fwd",
    )(*args)
    return outs if emit_state else (outs[0], None)


def _ssd_layer(x, mod, g, p, layer, h0, state, batch, seq_len):
    z, xT, bm, cmT, dt = _ssd_in(x, mod, g, p, layer, seq_len)
    ybT, buf = _ssd_scan(xT, bm, cmT, dt, p, layer, h0, None, state, batch=batch, reverse=True)
    if state is not None:
        state = (buf, state[1])
    x_new, buf = _ssd_scan(xT, bm, cmT, dt, p, layer, h0, (z, ybT, x, mod, g), state,
                           batch=batch, reverse=False)
    return x_new, buf


def _qkv_kernel(*refs, rotary):
    kh_ref = None
    if rotary:
        x_ref, mod_ref, g_ref, w_ref, cos_ref, sin_ref, q_ref, k_ref, v_ref = refs
    else:
        x_ref, mod_ref, g_ref, w_ref, q_ref, k_ref, v_ref, kh_ref = refs
    d = D_MODEL
    sh, sc, _ = _mod_slices(mod_ref[0], 0)
    h = (_rms(x_ref[...], g_ref[0:1] * (1.0 + sc)) + sh).astype(BF16)
    if rotary:
        lane = lax.broadcasted_iota(jnp.int32, (x_ref.shape[0], LANES), 1)
        first_half = (lane & (DIFF_HEAD_DIM - 1)) < DIFF_HEAD_DIM // 2
        cos = cos_ref[...]
        sin = sin_ref[...]

    def rot(a):
        if not rotary:
            return a
        partner = jnp.where(first_half, pltpu.roll(a, LANES - DIFF_HEAD_DIM // 2, 1),
                            pltpu.roll(a, DIFF_HEAD_DIM // 2, 1))
        return a * cos + partner * sin

    scale = DIFF_HEAD_DIM ** -0.5 * math.log2(math.e)
    for j in range(d // LANES):
        cs = slice(j * LANES, (j + 1) * LANES)
        qj = jnp.dot(h, w_ref[:, j * LANES:(j + 1) * LANES], preferred_element_type=F32)
        kj = jnp.dot(h, w_ref[:, d + j * LANES:d + (j + 1) * LANES], preferred_element_type=F32)
        q_ref[:, cs] = (rot(qj) * scale).astype(q_ref.dtype)
        kj = rot(kj)
        if rotary:
            k_ref[:, cs] = kj.astype(k_ref.dtype)
        else:
            k_ref[cs, :] = kj.T.astype(k_ref.dtype)
        if kh_ref is not None:
            for i in range(LANES // DIFF_HEAD_DIM):
                kh_ref[:, j * (LANES // DIFF_HEAD_DIM) + i, :] = (
                    kj[:, i * DIFF_HEAD_DIM:(i + 1) * DIFF_HEAD_DIM])
    v_ref[...] = jnp.dot(h, w_ref[:, 2 * d:3 * d], preferred_element_type=F32).astype(v_ref.dtype)


def _qkv(x, mod, g, w_qkv, rot_tables, seq_len):
    n = x.shape[0]
    t = ROW_TILE
    rotary = rot_tables is not None
    in_specs = [pl.BlockSpec((t, D_MODEL), lambda i: (i, 0)), _mod_spec(mod.shape[0], t, seq_len),
                _const_spec(g.shape), _const_spec(w_qkv.shape)]
    args = [x, mod, g, w_qkv]
    if rotary:
        per_seq = seq_len // t
        in_specs += [pl.BlockSpec((t, LANES), lambda i: (i % per_seq, 0))] * 2
        args += list(rot_tables)
    row_spec = pl.BlockSpec((t, D_MODEL), lambda i: (i, 0))
    blk = max(seq_len, t)
    kT_spec = pl.BlockSpec((None, D_MODEL, t), lambda i: ((i * t) // blk, 0, ((i * t) % blk) // t))
    out_specs = [row_spec, row_spec if rotary else kT_spec, row_spec]
    out_shape = [jax.ShapeDtypeStruct((n, D_MODEL), BF16),
                 jax.ShapeDtypeStruct((n, D_MODEL) if rotary else (n // blk, D_MODEL, blk), BF16),
                 jax.ShapeDtypeStruct((n, D_MODEL), BF16 if rotary else F32)]
    if not rotary:
        out_specs.append(pl.BlockSpec((t, 2 * DIFF_HEADS, DIFF_HEAD_DIM), lambda i: (i, 0, 0)))
        out_shape.append(jax.ShapeDtypeStruct((n, 2 * DIFF_HEADS, DIFF_HEAD_DIM), F32))
    return pl.pallas_call(
        functools.partial(_qkv_kernel, rotary=rotary),
        grid=(n // t,),
        in_specs=in_specs,
        out_specs=out_specs,
        out_shape=out_shape,
        compiler_params=_cparams(1),
        name="diff_qkv",
    )(*args)


def _attn_kernel(*refs, has_cache, keys_transposed, lam_init):
    if has_cache:
        (q_ref, k_ref, v_ref, ck_ref, cv_ref, x_ref, mod_ref, g_ref, lp_ref, sg_ref, wo_ref,
         o_ref, oall_ref) = refs
    else:
        q_ref, k_ref, v_ref, x_ref, mod_ref, g_ref, lp_ref, sg_ref, wo_ref, o_ref, oall_ref = refs
    dn = (((1,), (1,)), ((), ()))
    lp = lp_ref[...]
    lam = (jnp.exp(jnp.sum(lp[0:1] * lp[1:2], axis=-1, keepdims=True))
           - jnp.exp(jnp.sum(lp[2:3] * lp[3:4], axis=-1, keepdims=True)) + lam_init)
    hd = DIFF_HEAD_DIM

    def scores(hh):
        hs = slice(hh * hd, (hh + 1) * hd)
        qh = q_ref[:, hs]
        if keys_transposed:
            s_new = jnp.dot(qh, k_ref[hs, :], preferred_element_type=F32)
        else:
            s_new = lax.dot_general(qh, k_ref[:, hs], dn, preferred_element_type=F32)
        s_old = None
        if has_cache:
            s_old = lax.dot_general(qh, ck_ref[0, :, hs].astype(BF16), dn, preferred_element_type=F32)
        return s_old, s_new

    def probs(s_old, s_new):
        m = jnp.max(s_new, axis=-1, keepdims=True)
        if has_cache:
            m = jnp.maximum(m, jnp.max(s_old, axis=-1, keepdims=True))
            p_old = jnp.exp2(s_old - m)
        p_new = jnp.exp2(s_new - m)
        l = jnp.sum(p_new, axis=-1, keepdims=True)
        if has_cache:
            l = l + jnp.sum(p_old, axis=-1, keepdims=True)
            return p_old, p_new, l
        return None, p_new, l

    def diff_weights(pair_scores):
        po0, pn0, l0 = probs(*pair_scores[0])
        po1, pn1, l1 = probs(*pair_scores[1])
        ratio = lam * l0 / l1
        att_old = (po0 - po1 * ratio).astype(BF16) if has_cache else None
        return att_old, (pn0 - pn1 * ratio).astype(BF16), l0

    def weighted_values(hp, att_old, att, l0):
        vs = slice(hp * 2 * hd, (hp + 1) * 2 * hd)
        o = jnp.dot(att, v_ref[:, vs].astype(BF16), preferred_element_type=F32)
        if has_cache:
            o = o + jnp.dot(att_old, cv_ref[0, :, vs].astype(BF16), preferred_element_type=F32)
        o = o * (1.0 / l0)
        oall_ref[:, vs] = (_rms(o, sg_ref[...]) * (1.0 - lam_init)).astype(BF16)

    sc, att = {}, {}
    for step in range(DIFF_HEADS + 2):
        if step < DIFF_HEADS:
            sc[step] = (scores(2 * step), scores(2 * step + 1))
        if step >= 2:
            weighted_values(step - 2, *att.pop(step - 2))
        if 1 <= step <= DIFF_HEADS:
            att[step - 1] = diff_weights(sc.pop(step - 1))

    m_out = jnp.dot(oall_ref[...], wo_ref[...], preferred_element_type=F32)
    _, _, gt = _mod_slices(mod_ref[0], 0)
    o_ref[...] = x_ref[...] + _rms(m_out, gt * g_ref[1:2])


def _attention(q, k, v, cache, x, mod, g, lam_p, subln_g, w_o, *, batch, lam_init):
    n = x.shape[0]
    seq = n // batch
    tq = min(ATTN_Q_TILE, seq)
    nq = seq // tq
    has_cache = cache is not None
    keys_transposed = k.ndim == 3
    v_spec = pl.BlockSpec((seq, D_MODEL), lambda b, t: (b, 0))
    if keys_transposed:
        blk = k.shape[2]
        k_spec = pl.BlockSpec((None, D_MODEL, seq),
                              lambda b, t: ((b * seq) // blk, 0, ((b * seq) % blk) // seq))
    else:
        k_spec = v_spec
    in_specs = [pl.BlockSpec((tq, D_MODEL), lambda b, t: (b * nq + t, 0)), k_spec, v_spec]
    args = [q, k, v]
    if has_cache:
        past = cache[0].shape[1]
        c_spec = pl.BlockSpec((1, past, D_MODEL), lambda b, t: (b, 0, 0))
        in_specs += [c_spec, c_spec]
        args += list(cache)
    n_mod = mod.shape[0]
    in_specs += [pl.BlockSpec((tq, D_MODEL), lambda b, t: (b * nq + t, 0)),
                 pl.BlockSpec((1, 1, 6 * D_MODEL),
                              (lambda b, t: (b, 0, 0)) if n_mod > 1 else (lambda b, t: (0, 0, 0))),
                 _const_spec(g.shape), _const_spec(lam_p.shape), _const_spec(subln_g.shape),
                 _const_spec(w_o.shape)]
    args += [x, mod, g, lam_p, subln_g, w_o]
    return pl.pallas_call(
        functools.partial(_attn_kernel, has_cache=has_cache, keys_transposed=keys_transposed,
                          lam_init=lam_init),
        grid=(batch, nq),
        in_specs=in_specs,
        out_specs=pl.BlockSpec((tq, D_MODEL), lambda b, t: (b * nq + t, 0)),
        out_shape=jax.ShapeDtypeStruct((n, D_MODEL), F32),
        scratch_shapes=[pltpu.VMEM((tq, D_MODEL), BF16)],
        compiler_params=_cparams(2),
        name="diff_attention",
    )(*args)


def _rotary_tables(n_tokens):
    rows = n_tokens // GRID_W
    row = jnp.repeat(jnp.arange(rows, dtype=F32), GRID_W)
    col = jnp.tile(jnp.arange(GRID_W, dtype=F32), rows)
    inv = ROPE_THETA ** (-jnp.arange(ROT_PAIRS_PER_AXIS, dtype=F32) / ROT_PAIRS_PER_AXIS)
    ang = jnp.concatenate([row[:, None] * inv, col[:, None] * inv], axis=-1)
    cos, sin = jnp.cos(ang), jnp.sin(ang)
    reps = LANES // DIFF_HEAD_DIM
    return (jnp.tile(jnp.concatenate([cos, cos], axis=-1), (1, reps)),
            jnp.tile(jnp.concatenate([-sin, sin], axis=-1), (1, reps)))


def kernel(x_prompt, x_sample, state_ssm, cache_k, cache_v, c, c_ctx, w_mod, b_mod, norm_g, ssd_w_in, ssd_conv_w, ssd_conv_b, ssd_dt_bias, ssd_a_log, ssd_d, ssd_norm_g, ssd_w_out, sc_w_in, sc_conv_w, sc_w_out, da_w_qkv, da_lambda, da_subln_g, da_w_out, ffn_w_up, ffn_conv_w, ffn_w_down):
    bp, lp_, d = x_prompt.shape
    bs, ls, _ = x_sample.shape
    xp = x_prompt.reshape(bp * lp_, d)
    xs = x_sample.reshape(bs * ls, d)

    n_cond = 1 + bs
    cond = jnp.concatenate([c_ctx[None], c, jnp.zeros((HALO - n_cond, d), F32)], axis=0)
    mod_all = _modulation(cond, w_mod, b_mod)

    rot = _rotary_tables(ls)
    n_ssd_layers = (DEPTH + 2) // N_MIXERS
    pad = LANES - 2 * SSM_HEADS
    xbc_end = SSM_D_INNER + SSM_CONV_DIM
    ssd = {
        "w_in": ssd_w_in[:, :, :xbc_end].astype(BF16),
        "w_dt": jnp.pad(ssd_w_in[:, :, xbc_end:], ((0, 0), (0, 0), (0, pad))).astype(BF16),
        "conv_w": ssd_conv_w,
        "conv_b": ssd_conv_b[:, None],
        "dt_bias": jnp.pad(ssd_dt_bias.reshape(n_ssd_layers, 1, -1), ((0, 0), (0, 0), (0, pad))),
        "a_log": jnp.pad(ssd_a_log.reshape(n_ssd_layers, 1, -1), ((0, 0), (0, 0), (0, pad))),
        "d_exp": jnp.broadcast_to(jnp.repeat(ssd_d, SSM_HEAD_DIM, axis=1)[:, :, None],
                                  (n_ssd_layers, SSM_D_INNER, SCAN_STEP)),
        "norm_g": ssd_norm_g[:, None],
        "w_out": ssd_w_out.astype(BF16),
    }
    h0 = state_ssm.reshape(bs, n_ssd_layers, 2, SSM_D_INNER, SSM_STATE)
    w_up_all, w_dn_all = ffn_w_up.astype(BF16), ffn_w_down.astype(BF16)
    ssm_buf = None
    new_k, new_v = [], []
    for l in range(DEPTH):
        j = l // N_MIXERS
        kind = l % N_MIXERS
        mod_p = mod_all[l, 0:1][:, None]
        mod_s = mod_all[l, 1:n_cond][:, None]
        g = norm_g[l]
        if kind == 0:
            xp, ssm_buf = _ssd_layer(xp, mod_p, g, ssd, j, None, (ssm_buf, n_ssd_layers), bp, lp_)
            xs, _ = _ssd_layer(xs, mod_s, g, ssd, j, h0, None, bs, ls)
        elif kind == 1:
            w_in, w_out = sc_w_in[j].astype(BF16), sc_w_out[j].astype(BF16)
            xp = _sconv(xp, mod_p, g, w_in, sc_conv_w[j], w_out, lp_)
            xs = _sconv(xs, mod_s, g, w_in, sc_conv_w[j], w_out, ls)
        else:
            lam_init = 0.8 - 0.6 * math.exp(-0.3 * l)
            w_qkv, w_o = da_w_qkv[j].astype(BF16), da_w_out[j].astype(BF16)
            sg = da_subln_g[j][None]
            qp, kp, vp, kp_heads = _qkv(xp, mod_p, g, w_qkv, None, lp_)
            xp = _attention(qp, kp, vp, None, xp, mod_p, g, da_lambda[j], sg, w_o,
                            batch=bp, lam_init=lam_init)
            new_k.append(kp_heads.reshape(bp, lp_, 2 * DIFF_HEADS, DIFF_HEAD_DIM))
            new_v.append(vp.reshape(bp, lp_, DIFF_HEADS, 2 * DIFF_HEAD_DIM))
            qs, ks, vs = _qkv(xs, mod_s, g, w_qkv, rot, ls)
            cache = (cache_k[:, j].reshape(bs, -1, d), cache_v[:, j].reshape(bs, -1, d))
            xs = _attention(qs, ks, vs, cache, xs, mod_s, g, da_lambda[j], sg, w_o,
                            batch=bs, lam_init=lam_init)
        xp = _ffn(xp, mod_p, g, l, w_up_all, ffn_conv_w, w_dn_all, lp_)
        xs = _ffn(xs, mod_s, g, l, w_up_all, ffn_conv_w, w_dn_all, ls)

    new_ssm = ssm_buf.reshape(bp, n_ssd_layers, 2, SSM_HEADS, SSM_HEAD_DIM, SSM_STATE)
    return (xp.reshape(bp, lp_, d), xs.reshape(bs, ls, d), new_ssm,
            jnp.stack(new_k, axis=1), jnp.stack(new_v, axis=1))
```

```python
import functools
import math

import jax
import jax.numpy as jnp
from jax import lax
from jax.experimental import pallas as pl
from jax.experimental.pallas import tpu as pltpu

D_MODEL = 1024
DEPTH = 4
GRID_W = 64
N_MIXERS = 3
SSM_D_INNER = 2 * D_MODEL
SSM_HEAD_DIM = 64
SSM_HEADS = SSM_D_INNER // SSM_HEAD_DIM
SSM_GROUPS = 8
SSM_STATE = 128
SSM_CONV_DIM = SSM_D_INNER + 2 * SSM_GROUPS * SSM_STATE
HEADS_PER_GROUP = SSM_HEADS // SSM_GROUPS
GROUP_ROWS = HEADS_PER_GROUP * SSM_HEAD_DIM
DIFF_HEAD_DIM = 64
DIFF_HEADS = D_MODEL // (2 * DIFF_HEAD_DIM)
ROPE_THETA = 10000.0
ROT_PAIRS_PER_AXIS = DIFF_HEAD_DIM // 4
FFN_DIM = 2816
NORM_EPS = 1e-6

LANES = 128
HALO = 8
SCAN_CHUNK = 128
SCAN_STEP = 256
ROW_TILE = 512
ATTN_Q_TILE = 256
COL_CHUNK = 256
FFN_DOWN_GROUP = 4
FFN_DOWN_LAG = 3
VMEM_LIMIT = 56 * 1024 * 1024

F32 = jnp.float32
BF16 = jnp.bfloat16


def _cparams(n_axes):
    return pltpu.CompilerParams(dimension_semantics=("arbitrary",) * n_axes,
                                vmem_limit_bytes=VMEM_LIMIT)


def _rms(x, g):
    ms = jnp.mean(x * x, axis=-1, keepdims=True)
    return x * lax.rsqrt(ms + NORM_EPS) * g


def _silu(x):
    return x * jax.nn.sigmoid(x)


def _mod_slices(mod, first):
    d = D_MODEL
    return (mod[:, first * d:(first + 1) * d], mod[:, (first + 1) * d:(first + 2) * d],
            mod[:, (first + 2) * d:(first + 3) * d])


def _ext_rows(tile_rows, seq_len):
    return tile_rows if seq_len <= tile_rows else tile_rows + 2 * HALO


def _main_rows(tile_rows, seq_len):
    return slice(0, tile_rows) if seq_len <= tile_rows else slice(HALO, HALO + tile_rows)


def _ext_norm(xp_ref, x_ref, xn_ref, g, sc, sh, seq_len):
    t = x_ref.shape[0]
    gs = g * (1.0 + sc)
    if seq_len <= t:
        return (_rms(x_ref[...], gs) + sh).astype(BF16)
    tiles_per_seq = seq_len // t
    k = pl.program_id(0) % tiles_per_seq
    keep = (jnp.where(k == 0, 0.0, 1.0), None, jnp.where(k == tiles_per_seq - 1, 0.0, 1.0))
    pieces = []
    for r, kp in zip((xp_ref, x_ref, xn_ref), keep):
        h = _rms(r[...], gs) + sh
        pieces.append(h if kp is None else h * kp)
    return jnp.concatenate(pieces, axis=0).astype(BF16)


def _conv3(u, w3, tile_rows, seq_len):
    if u.shape[0] == tile_rows:
        return _conv_rows(u, w3, 0, tile_rows, seq_len, first_row=0)
    return _conv_rows(u, w3, HALO, tile_rows, seq_len)


def _conv_rows(u, w3, off, n_out, seq_len, first_row=None):
    n = u.shape[0]
    rows = slice(off, off + n_out)
    up = pltpu.roll(u, 1, 0)[rows]
    un = pltpu.roll(u, n - 1, 0)[rows]
    if first_row is not None:
        pos = (lax.broadcasted_iota(jnp.int32, (n_out, u.shape[1]), 0) + first_row) & (seq_len - 1)
        up = jnp.where(pos != 0, up, 0.0)
        un = jnp.where(pos != seq_len - 1, un, 0.0)
    return up * w3[0:1] + u[rows] * w3[1:2] + un * w3[2:3]


def _halo_specs(n_rows, tile_rows, width):
    per = tile_rows // HALO
    last = n_rows // HALO - 1
    return [
        pl.BlockSpec((HALO, width), lambda i: (jnp.maximum(i * per - 1, 0), 0)),
        pl.BlockSpec((tile_rows, width), lambda i: (i, 0)),
        pl.BlockSpec((HALO, width), lambda i: (jnp.minimum((i + 1) * per, last), 0)),
    ]


def _mod_spec(n_mod, tile_rows, seq_len):
    if n_mod == 1:
        return pl.BlockSpec((1, 1, 6 * D_MODEL), lambda i: (0, 0, 0))
    return pl.BlockSpec((1, 1, 6 * D_MODEL), lambda i: ((i * tile_rows) // seq_len, 0, 0))


def _const_spec(shape):
    nd = len(shape)
    return pl.BlockSpec(shape, lambda *_: (0,) * nd)


def _layer_spec(stacked, layer):
    shape = stacked.shape[1:]
    idx = (layer,) + (0,) * len(shape)
    return pl.BlockSpec((None,) + shape, lambda *_: idx, pipeline_mode=pl.Buffered(1))


def _mod_kernel(cond_ref, w_ref, b_ref, o_ref):
    a = _silu(cond_ref[...]).astype(BF16)
    o_ref[0] = jnp.dot(a, w_ref[0].astype(BF16), preferred_element_type=F32) + b_ref[0]


def _modulation(cond, w_mod, b_mod):
    n_cols = 6 * D_MODEL
    blk = n_cols // 4
    return pl.pallas_call(
        _mod_kernel,
        grid=(DEPTH, n_cols // blk),
        in_specs=[_const_spec(cond.shape),
                  pl.BlockSpec((1, D_MODEL, blk), lambda l, j: (l, 0, j)),
                  pl.BlockSpec((1, 1, blk), lambda l, j: (l, 0, j))],
        out_specs=pl.BlockSpec((1, cond.shape[0], blk), lambda l, j: (l, 0, j)),
        out_shape=jax.ShapeDtypeStruct((DEPTH, cond.shape[0], n_cols), F32),
        compiler_params=_cparams(2),
        name="modulation",
    )(cond, w_mod, b_mod.reshape(DEPTH, 1, n_cols))


def _ffn_kernel(xp_ref, x_ref, xn_ref, mod_ref, g_ref, wup_ref, cw_ref, wdn_ref, o_ref,
                hext_ref, act_ref, *, seq_len):
    t = x_ref.shape[0]
    sh, sc, gt = _mod_slices(mod_ref[0], 3)
    hext_ref[...] = _ext_norm(xp_ref, x_ref, xn_ref, g_ref[2:3], sc, sh, seq_len)
    n_chunks = FFN_DIM // COL_CHUNK

    slabs = [(c0, min(c0 + FFN_DOWN_GROUP, n_chunks - 1)) for c0 in range(0, n_chunks - 1, FFN_DOWN_GROUP)]
    issue_at = {}
    for c0, c1 in slabs:
        issue_at.setdefault(min(c1 - 1 + FFN_DOWN_LAG, n_chunks - 1), []).append((c0, c1))

    acc = None

    def down(c0, c1):
        k0, k1 = c0 * COL_CHUNK, c1 * COL_CHUNK
        part = jnp.dot(act_ref[:, k0:k1], wdn_ref[k0:k1, :], preferred_element_type=F32)
        return part if acc is None else acc + part

    for j in range(n_chunks):
        cg = j * COL_CHUNK
        cv = FFN_DIM + cg
        h = hext_ref[...]
        ug = jnp.dot(h, wup_ref[:, cg:cg + COL_CHUNK], preferred_element_type=F32)
        uv = jnp.dot(h, wup_ref[:, cv:cv + COL_CHUNK], preferred_element_type=F32)
        for c0, c1 in issue_at.get(j, []):
            acc = down(c0, c1)
        gate = _conv3(ug, cw_ref[:, cg:cg + COL_CHUNK], t, seq_len)
        val = _conv3(uv, cw_ref[:, cv:cv + COL_CHUNK], t, seq_len)
        act_ref[:, cg:cg + COL_CHUNK] = (_silu(gate) * val).astype(BF16)
    acc = down(n_chunks - 1, n_chunks)
    o_ref[...] = x_ref[...] + _rms(acc, gt * g_ref[3:4])


def _ffn(x, mod, g, layer, w_up, conv_w, w_down, seq_len):
    n = x.shape[0]
    t = ROW_TILE
    return pl.pallas_call(
        functools.partial(_ffn_kernel, seq_len=seq_len),
        grid=(n // t,),
        in_specs=_halo_specs(n, t, D_MODEL) + [
            _mod_spec(mod.shape[0], t, seq_len), _const_spec(g.shape), _layer_spec(w_up, layer),
            _layer_spec(conv_w, layer), _layer_spec(w_down, layer)],
        out_specs=pl.BlockSpec((t, D_MODEL), lambda i: (i, 0)),
        out_shape=jax.ShapeDtypeStruct((n, D_MODEL), F32),
        scratch_shapes=[pltpu.VMEM((_ext_rows(t, seq_len), D_MODEL), BF16),
                        pltpu.VMEM((t, FFN_DIM), BF16)],
        compiler_params=_cparams(1),
        name="conv_ffn",
    )(x, x, x, mod, g, w_up, conv_w, w_down)


def _sconv_kernel(xp_ref, x_ref, xn_ref, mod_ref, g_ref, win_ref, cw_ref, wout_ref, o_ref,
                  hext_ref, act_ref, *, seq_len):
    t = x_ref.shape[0]
    d = D_MODEL
    sh, sc, gt = _mod_slices(mod_ref[0], 0)
    hext_ref[...] = _ext_norm(xp_ref, x_ref, xn_ref, g_ref[0:1], sc, sh, seq_len)
    for j in range(d // COL_CHUNK):
        c0 = j * COL_CHUNK
        bg = jnp.dot(hext_ref[_main_rows(t, seq_len), :], win_ref[:, c0:c0 + COL_CHUNK],
                     preferred_element_type=F32)
        h = hext_ref[...]
        cg = jnp.dot(h, win_ref[:, d + c0:d + c0 + COL_CHUNK], preferred_element_type=F32)
        u = jnp.dot(h, win_ref[:, 2 * d + c0:2 * d + c0 + COL_CHUNK], preferred_element_type=F32)
        conv = _conv3(cg * u, cw_ref[:, c0:c0 + COL_CHUNK], t, seq_len)
        act_ref[:, c0:c0 + COL_CHUNK] = (bg * conv).astype(BF16)
    m = jnp.dot(act_ref[...], wout_ref[...], preferred_element_type=F32)
    o_ref[...] = x_ref[...] + _rms(m, gt * g_ref[1:2])


def _sconv(x, mod, g, w_in, conv_w, w_out, seq_len):
    n = x.shape[0]
    t = ROW_TILE
    return pl.pallas_call(
        functools.partial(_sconv_kernel, seq_len=seq_len),
        grid=(n // t,),
        in_specs=_halo_specs(n, t, D_MODEL) + [
            _mod_spec(mod.shape[0], t, seq_len), _const_spec(g.shape), _const_spec(w_in.shape),
            _const_spec(conv_w.shape), _const_spec(w_out.shape)],
        out_specs=pl.BlockSpec((t, D_MODEL), lambda i: (i, 0)),
        out_shape=jax.ShapeDtypeStruct((n, D_MODEL), F32),
        scratch_shapes=[pltpu.VMEM((_ext_rows(t, seq_len), D_MODEL), BF16),
                        pltpu.VMEM((t, D_MODEL), BF16)],
        compiler_params=_cparams(1),
        name="short_conv_mixer",
    )(x, x, x, mod, g, w_in, conv_w, w_out)


def _ssd_in_kernel(xp_ref, x_ref, xn_ref, mod_ref, g_ref, win_ref, wdt_ref, cw_ref, cb_ref, dtb_ref,
                   z_ref, xT_ref, bm_ref, cmT_ref, dt_ref, hext_ref, *, seq_len):
    t = x_ref.shape[0]
    di = SSM_D_INNER
    gn = SSM_GROUPS * SSM_STATE
    sh, sc, _ = _mod_slices(mod_ref[0], 0)
    hext_ref[...] = _ext_norm(xp_ref, x_ref, xn_ref, g_ref[0:1], sc, sh, seq_len)

    raw = jnp.dot(hext_ref[_main_rows(t, seq_len), :], wdt_ref[...],
                  preferred_element_type=F32) + dtb_ref[...]
    dt_ref[...] = jnp.maximum(raw, 0.0) + jnp.log1p(jnp.exp(-jnp.abs(raw)))

    n_conv = SSM_CONV_DIM // COL_CHUNK
    n_z = di // COL_CHUNK
    for j in range(n_conv):
        c0 = j * COL_CHUNK
        u = jnp.dot(hext_ref[...], win_ref[:, di + c0:di + c0 + COL_CHUNK], preferred_element_type=F32)
        if j % (n_conv // n_z) == 0:
            zc = j // (n_conv // n_z) * COL_CHUNK
            z_ref[:, zc:zc + COL_CHUNK] = jnp.dot(hext_ref[_main_rows(t, seq_len), :],
                                                  win_ref[:, zc:zc + COL_CHUNK], preferred_element_type=F32)
        act = _silu(_conv3(u, cw_ref[:, c0:c0 + COL_CHUNK], t, seq_len) + cb_ref[:, c0:c0 + COL_CHUNK])
        if c0 < di:
            xT_ref[c0:c0 + COL_CHUNK, :] = act.T
        elif c0 < di + gn:
            bm_ref[:, c0 - di:c0 - di + COL_CHUNK] = act.astype(BF16)
        else:
            cmT_ref[c0 - di - gn:c0 - di - gn + COL_CHUNK, :] = act.T.astype(BF16)


def _ssd_in(x, mod, g, p, layer, seq_len):
    n = x.shape[0]
    t = SCAN_STEP
    gn = SSM_GROUPS * SSM_STATE
    weights = [p[k] for k in ("w_in", "w_dt", "conv_w", "conv_b", "dt_bias")]
    return pl.pallas_call(
        functools.partial(_ssd_in_kernel, seq_len=seq_len),
        grid=(n // t,),
        in_specs=_halo_specs(n, t, D_MODEL) + [
            _mod_spec(mod.shape[0], t, seq_len), _const_spec(g.shape)]
        + [_layer_spec(w, layer) for w in weights],
        out_specs=[pl.BlockSpec((t, SSM_D_INNER), lambda i: (i, 0)),
                   pl.BlockSpec((None, SSM_D_INNER, t), lambda i: (i, 0, 0)),
                   pl.BlockSpec((t, gn), lambda i: (i, 0)),
                   pl.BlockSpec((None, gn, t), lambda i: (i, 0, 0)),
                   pl.BlockSpec((t, LANES), lambda i: (i, 0))],
        out_shape=[jax.ShapeDtypeStruct((n, SSM_D_INNER), F32),
                   jax.ShapeDtypeStruct((n // t, SSM_D_INNER, t), F32),
                   jax.ShapeDtypeStruct((n, gn), BF16),
                   jax.ShapeDtypeStruct((n // t, gn, t), BF16),
                   jax.ShapeDtypeStruct((n, LANES), F32)],
        scratch_shapes=[pltpu.VMEM((_ext_rows(t, seq_len), D_MODEL), BF16)],
        compiler_params=_cparams(1),
        name="ssd_in_proj",
    )(x, x, x, mod, g, *weights)


def _split3(v):
    hi = v.astype(BF16)
    r1 = v - hi.astype(F32)
    mid = r1.astype(BF16)
    lo = (r1 - mid.astype(F32)).astype(BF16)
    return hi, mid, lo


def _chunk_decays(dt_ref, alog_ref, off, reverse):
    q = SCAN_CHUNK
    dt = dt_ref[off:off + q, :]
    da = dt * (-jnp.exp(alog_ref[...]))
    r_i = lax.broadcasted_iota(jnp.int32, (q, q), 0)
    c_i = lax.broadcasted_iota(jnp.int32, (q, q), 1)
    tri = jnp.where((c_i >= r_i) if reverse else (c_i <= r_i), 1.0, 0.0).astype(BF16)
    acs = sum(jnp.dot(tri, part, preferred_element_type=F32) for part in _split3(da))
    mask = (r_i >= c_i) if reverse else (r_i <= c_i)
    acsT = acs.T
    dtT = dt.T
    src = acs - jnp.log(dt)
    last = 0 if reverse else q - 1
    totT = acsT[:, last:last + 1]
    wT = jnp.exp(totT - acsT) * dtT
    eaT = jnp.exp(acsT)
    etot = jnp.broadcast_to(jnp.exp(totT), (LANES, SSM_STATE))
    return mask, acsT, src, wT, eaT, etot


def _group_products(bm_ref, cmT_ref, st_ref, off, g):
    tok = slice(off, off + SCAN_CHUNK)
    ns = slice(g * SSM_STATE, (g + 1) * SSM_STATE)
    bm_g = bm_ref[tok, ns]
    cmT_g = cmT_ref[ns, tok]
    cbT = jnp.dot(bm_g, cmT_g, preferred_element_type=F32)
    st_g = st_ref[g * GROUP_ROWS:(g + 1) * GROUP_ROWS, :]
    y_off = jnp.dot(st_g.astype(BF16), cmT_g, preferred_element_type=F32)
    return bm_g, cbT, st_g, y_off


def _group_heads(decays, products, xT_ref, yT_ref, st_ref, off, g, reverse):
    mask, acsT, src, wT, eaT, etot = decays
    bm_g, cbT, st_g, y_off = products
    tok = slice(off, off + SCAN_CHUNK)
    xT_g = xT_ref[g * GROUP_ROWS:(g + 1) * GROUP_ROWS, tok]
    head0 = (SSM_HEADS if reverse else 0) + g * HEADS_PER_GROUP
    xw, scale = [], []
    for r in range(HEADS_PER_GROUP):
        col = head0 + r
        hr = slice(r * SSM_HEAD_DIM, (r + 1) * SSM_HEAD_DIM)
        mT = jnp.where(mask, cbT * jnp.exp(acsT[col:col + 1, :] - src[:, col:col + 1]), 0.0)
        xh = xT_g[hr, :]
        yh = jnp.dot(xh.astype(BF16), mT.astype(BF16), preferred_element_type=F32)
        yh = yh + y_off[hr, :] * eaT[col:col + 1, :]
        yT_ref[g * GROUP_ROWS + r * SSM_HEAD_DIM:g * GROUP_ROWS + (r + 1) * SSM_HEAD_DIM, tok] = yh
        xw.append((xh * wT[col:col + 1, :]).astype(BF16))
        scale.append(jnp.broadcast_to(etot[col:col + 1, :], (SSM_HEAD_DIM, SSM_STATE)))
    d_state = jnp.dot(jnp.concatenate(xw, axis=0), bm_g, preferred_element_type=F32)
    st_ref[g * GROUP_ROWS:(g + 1) * GROUP_ROWS, :] = st_g * jnp.concatenate(scale, axis=0) + d_state


def _ssd_scan_kernel(*refs, reverse, has_h0, finish, emit_state, state_aliased, state_slab, n_steps):
    refs = list(refs)
    xT_ref, bm_ref, cmT_ref, dt_ref, alog_ref = refs[:5]
    pos = 5
    h0_ref = None
    if has_h0:
        h0_ref = refs[pos]
        pos += 1
    if finish:
        (z_ref, ybT_ref, x_ref, mod_ref, g_ref, dexp_ref, ng_ref, wout_ref) = refs[pos:pos + 8]
        pos += 8
    if state_aliased:
        pos += 1
    y_out_ref = refs[pos]
    pos += 1
    state_out_ref = None
    if emit_state:
        state_out_ref = refs[pos]
        pos += 1
    st_ref = refs[pos]
    if finish:
        yT_ref, ybf_ref = refs[pos + 1:pos + 3]
    else:
        yT_ref = y_out_ref

    c = pl.program_id(1)

    @pl.when(c == 0)
    def _():
        if has_h0:
            st_ref[...] = h0_ref[0, 0, 0]
        else:
            st_ref[...] = jnp.zeros_like(st_ref)

    offsets = [k * SCAN_CHUNK for k in range(SCAN_STEP // SCAN_CHUNK)]
    if reverse:
        offsets.reverse()
    decays = [_chunk_decays(dt_ref, alog_ref, off, reverse) for off in offsets]
    for off, dec in zip(offsets, decays):
        ahead = _group_products(bm_ref, cmT_ref, st_ref, off, 0)
        for g in range(SSM_GROUPS):
            cur = ahead
            if g + 1 < SSM_GROUPS:
                ahead = _group_products(bm_ref, cmT_ref, st_ref, off, g + 1)
            _group_heads(dec, cur, xT_ref, yT_ref, st_ref, off, g, reverse)
        if finish:
            tok = slice(off, off + SCAN_CHUNK)
            yT = yT_ref[:, tok] + ybT_ref[:, tok] + dexp_ref[:, tok] * xT_ref[:, tok]
            y = yT.T * _silu(z_ref[tok, :])
            ybf_ref[tok, :] = _rms(y, ng_ref[...]).astype(BF16)

    if emit_state:
        @pl.when(c == n_steps - 1)
        def _():
            if state_slab is None:
                state_out_ref[0, 0, 0] = st_ref[...]
            else:
                state_out_ref[...] = jnp.zeros_like(state_out_ref)
                state_out_ref[0, state_slab[0], state_slab[1]] = st_ref[...]

    if finish:
        out = jnp.dot(ybf_ref[...], wout_ref[...], preferred_element_type=F32)
        _, _, gt = _mod_slices(mod_ref[0], 0)
        y_out_ref[...] = x_ref[...] + _rms(out, gt * g_ref[1:2])


def _ssd_scan(xT, bm, cmT, dt, p, layer, h0, fin, state, *, batch, reverse):
    q = SCAN_STEP
    n = xT.shape[0] * q
    direction = 1 if reverse else 0
    emit_state = state is not None
    nc = n // batch // q
    gn = SSM_GROUPS * SSM_STATE
    finish = fin is not None

    def tok(b, c):
        return b * nc + ((nc - 1 - c) if reverse else c)

    def row_blk(b, c):
        return (tok(b, c), 0)

    t_spec = pl.BlockSpec((None, SSM_D_INNER, q), lambda b, c: (tok(b, c), 0, 0))
    in_specs = [t_spec,
                pl.BlockSpec((q, gn), row_blk),
                pl.BlockSpec((None, gn, q), lambda b, c: (tok(b, c), 0, 0)),
                pl.BlockSpec((q, LANES), row_blk),
                _layer_spec(p["a_log"], layer)]
    args = [xT, bm, cmT, dt, p["a_log"]]
    if h0 is not None:
        in_specs.append(pl.BlockSpec((1, 1, 1, SSM_D_INNER, SSM_STATE),
                                     lambda b, c: (b, layer, direction, 0, 0)))
        args.append(h0)
    if finish:
        z, ybT, x, mod, g = fin
        n_mod = mod.shape[0]
        in_specs += [pl.BlockSpec((q, SSM_D_INNER), row_blk),
                     t_spec,
                     pl.BlockSpec((q, D_MODEL), row_blk),
                     pl.BlockSpec((1, 1, 6 * D_MODEL),
                                  (lambda b, c: (b, 0, 0)) if n_mod > 1 else (lambda b, c: (0, 0, 0))),
                     _const_spec(g.shape), _layer_spec(p["d_exp"], layer),
                     _layer_spec(p["norm_g"], layer), _layer_spec(p["w_out"], layer)]
        args += [z, ybT, x, mod, g, p["d_exp"], p["norm_g"], p["w_out"]]
        out_specs = [pl.BlockSpec((q, D_MODEL), row_blk)]
        out_shape = [jax.ShapeDtypeStruct((n, D_MODEL), F32)]
    else:
        out_specs = [t_spec]
        out_shape = [jax.ShapeDtypeStruct((n // q, SSM_D_INNER, q), F32)]
    aliases = {}
    state_slab = None
    if emit_state:
        buf, n_layers = state
        out_shape.append(jax.ShapeDtypeStruct((batch, n_layers, 2, SSM_D_INNER, SSM_STATE), F32))
        if buf is None:
            state_slab = (layer, direction)
            out_specs.append(pl.BlockSpec((1, n_layers, 2, SSM_D_INNER, SSM_STATE),
                                          lambda b, c: (b, 0, 0, 0, 0)))
        else:
            out_specs.append(pl.BlockSpec((1, 1, 1, SSM_D_INNER, SSM_STATE),
                                          lambda b, c: (b, layer, direction, 0, 0)))
            in_specs.append(pl.BlockSpec(memory_space=pl.ANY))
            args.append(buf)
            aliases = {len(args) - 1: 1}
    scratch = [pltpu.VMEM((SSM_D_INNER, SSM_STATE), F32)]
    if finish:
        scratch += [pltpu.VMEM((SSM_D_INNER, q), F32), pltpu.VMEM((q, SSM_D_INNER), BF16)]
    outs = pl.pallas_call(
        functools.partial(_ssd_scan_kernel, reverse=reverse, has_h0=h0 is not None, finish=finish,
                          emit_state=emit_state, state_aliased=bool(aliases), state_slab=state_slab,
                          n_steps=nc),
        grid=(batch, nc),
        in_specs=in_specs, out_specs=out_specs, out_shape=out_shape,
        scratch_shapes=scratch,
        input_output_aliases=aliases,
        compiler_params=_cparams(2),
        name="ssd_scan_bwd" if reverse else "ssd_scan_fwd",
    )(*args)
    return outs if emit_state else (outs[0], None)


def _ssd_layer(x, mod, g, p, layer, h0, state, batch, seq_len):
    z, xT, bm, cmT, dt = _ssd_in(x, mod, g, p, layer, seq_len)
    ybT, buf = _ssd_scan(xT, bm, cmT, dt, p, layer, h0, None, state, batch=batch, reverse=True)
    if state is not None:
        state = (buf, state[1])
    x_new, buf = _ssd_scan(xT, bm, cmT, dt, p, layer, h0, (z, ybT, x, mod, g), state,
                           batch=batch, reverse=False)
    return x_new, buf


def _qkv_kernel(*refs, rotary):
    kh_ref = None
    if rotary:
        x_ref, mod_ref, g_ref, w_ref, cos_ref, sin_ref, q_ref, k_ref, v_ref = refs
    else:
        x_ref, mod_ref, g_ref, w_ref, q_ref, k_ref, v_ref, kh_ref = refs
    d = D_MODEL
    sh, sc, _ = _mod_slices(mod_ref[0], 0)
    h = (_rms(x_ref[...], g_ref[0:1] * (1.0 + sc)) + sh).astype(BF16)
    if rotary:
        lane = lax.broadcasted_iota(jnp.int32, (x_ref.shape[0], LANES), 1)
        first_half = (lane & (DIFF_HEAD_DIM - 1)) < DIFF_HEAD_DIM // 2
        cos = cos_ref[...]
        sin = sin_ref[...]

    def rot(a):
        if not rotary:
            return a
        partner = jnp.where(first_half, pltpu.roll(a, LANES - DIFF_HEAD_DIM // 2, 1),
                            pltpu.roll(a, DIFF_HEAD_DIM // 2, 1))
        return a * cos + partner * sin

    scale = DIFF_HEAD_DIM ** -0.5 * math.log2(math.e)
    for j in range(d // LANES):
        cs = slice(j * LANES, (j + 1) * LANES)
        qj = jnp.dot(h, w_ref[:, j * LANES:(j + 1) * LANES], preferred_element_type=F32)
        kj = jnp.dot(h, w_ref[:, d + j * LANES:d + (j + 1) * LANES], preferred_element_type=F32)
        q_ref[:, cs] = (rot(qj) * scale).astype(q_ref.dtype)
        kj = rot(kj)
        if rotary:
            k_ref[:, cs] = kj.astype(k_ref.dtype)
        else:
            k_ref[cs, :] = kj.T.astype(k_ref.dtype)
        if kh_ref is not None:
            for i in range(LANES // DIFF_HEAD_DIM):
                kh_ref[:, j * (LANES // DIFF_HEAD_DIM) + i, :] = (
                    kj[:, i * DIFF_HEAD_DIM:(i + 1) * DIFF_HEAD_DIM])
    v_ref[...] = jnp.dot(h, w_ref[:, 2 * d:3 * d], preferred_element_type=F32).astype(v_ref.dtype)


def _qkv(x, mod, g, w_qkv, rot_tables, seq_len):
    n = x.shape[0]
    t = ROW_TILE
    rotary = rot_tables is not None
    in_specs = [pl.BlockSpec((t, D_MODEL), lambda i: (i, 0)), _mod_spec(mod.shape[0], t, seq_len),
                _const_spec(g.shape), _const_spec(w_qkv.shape)]
    args = [x, mod, g, w_qkv]
    if rotary:
        per_seq = seq_len // t
        in_specs += [pl.BlockSpec((t, LANES), lambda i: (i % per_seq, 0))] * 2
        args += list(rot_tables)
    row_spec = pl.BlockSpec((t, D_MODEL), lambda i: (i, 0))
    blk = max(seq_len, t)
    kT_spec = pl.BlockSpec((None, D_MODEL, t), lambda i: ((i * t) // blk, 0, ((i * t) % blk) // t))
    out_specs = [row_spec, row_spec if rotary else kT_spec, row_spec]
    out_shape = [jax.ShapeDtypeStruct((n, D_MODEL), BF16),
                 jax.ShapeDtypeStruct((n, D_MODEL) if rotary else (n // blk, D_MODEL, blk), BF16),
                 jax.ShapeDtypeStruct((n, D_MODEL), BF16 if rotary else F32)]
    if not rotary:
        out_specs.append(pl.BlockSpec((t, 2 * DIFF_HEADS, DIFF_HEAD_DIM), lambda i: (i, 0, 0)))
        out_shape.append(jax.ShapeDtypeStruct((n, 2 * DIFF_HEADS, DIFF_HEAD_DIM), F32))
    return pl.pallas_call(
        functools.partial(_qkv_kernel, rotary=rotary),
        grid=(n // t,),
        in_specs=in_specs,
        out_specs=out_specs,
        out_shape=out_shape,
        compiler_params=_cparams(1),
        name="diff_qkv",
    )(*args)


def _attn_kernel(*refs, has_cache, keys_transposed, lam_init):
    if has_cache:
        (q_ref, k_ref, v_ref, ck_ref, cv_ref, x_ref, mod_ref, g_ref, lp_ref, sg_ref, wo_ref,
         o_ref, oall_ref) = refs
    else:
        q_ref, k_ref, v_ref, x_ref, mod_ref, g_ref, lp_ref, sg_ref, wo_ref, o_ref, oall_ref = refs
    dn = (((1,), (1,)), ((), ()))
    lp = lp_ref[...]
    lam = (jnp.exp(jnp.sum(lp[0:1] * lp[1:2], axis=-1, keepdims=True))
           - jnp.exp(jnp.sum(lp[2:3] * lp[3:4], axis=-1, keepdims=True)) + lam_init)
    hd = DIFF_HEAD_DIM

    def scores(hh):
        hs = slice(hh * hd, (hh + 1) * hd)
        qh = q_ref[:, hs]
        if keys_transposed:
            s_new = jnp.dot(qh, k_ref[hs, :], preferred_element_type=F32)
        else:
            s_new = lax.dot_general(qh, k_ref[:, hs], dn, preferred_element_type=F32)
        s_old = None
        if has_cache:
            s_old = lax.dot_general(qh, ck_ref[0, :, hs].astype(BF16), dn, preferred_element_type=F32)
        return s_old, s_new

    def probs(s_old, s_new):
        m = jnp.max(s_new, axis=-1, keepdims=True)
        if has_cache:
            m = jnp.maximum(m, jnp.max(s_old, axis=-1, keepdims=True))
            p_old = jnp.exp2(s_old - m)
        p_new = jnp.exp2(s_new - m)
        l = jnp.sum(p_new, axis=-1, keepdims=True)
        if has_cache:
            l = l + jnp.sum(p_old, axis=-1, keepdims=True)
            return p_old, p_new, l
        return None, p_new, l

    def diff_weights(pair_scores):
        po0, pn0, l0 = probs(*pair_scores[0])
        po1, pn1, l1 = probs(*pair_scores[1])
        ratio = lam * l0 / l1
        att_old = (po0 - po1 * ratio).astype(BF16) if has_cache else None
        return att_old, (pn0 - pn1 * ratio).astype(BF16), l0

    def weighted_values(hp, att_old, att, l0):
        vs = slice(hp * 2 * hd, (hp + 1) * 2 * hd)
        o = jnp.dot(att, v_ref[:, vs].astype(BF16), preferred_element_type=F32)
        if has_cache:
            o = o + jnp.dot(att_old, cv_ref[0, :, vs].astype(BF16), preferred_element_type=F32)
        o = o * (1.0 / l0)
        oall_ref[:, vs] = (_rms(o, sg_ref[...]) * (1.0 - lam_init)).astype(BF16)

    sc, att = {}, {}
    for step in range(DIFF_HEADS + 2):
        if step < DIFF_HEADS:
            sc[step] = (scores(2 * step), scores(2 * step + 1))
        if step >= 2:
            weighted_values(step - 2, *att.pop(step - 2))
        if 1 <= step <= DIFF_HEADS:
            att[step - 1] = diff_weights(sc.pop(step - 1))

    m_out = jnp.dot(oall_ref[...], wo_ref[...], preferred_element_type=F32)
    _, _, gt = _mod_slices(mod_ref[0], 0)
    o_ref[...] = x_ref[...] + _rms(m_out, gt * g_ref[1:2])


def _attention(q, k, v, cache, x, mod, g, lam_p, subln_g, w_o, *, batch, lam_init):
    n = x.shape[0]
    seq = n // batch
    tq = min(ATTN_Q_TILE, seq)
    nq = seq // tq
    has_cache = cache is not None
    keys_transposed = k.ndim == 3
    v_spec = pl.BlockSpec((seq, D_MODEL), lambda b, t: (b, 0))
    if keys_transposed:
        blk = k.shape[2]
        k_spec = pl.BlockSpec((None, D_MODEL, seq),
                              lambda b, t: ((b * seq) // blk, 0, ((b * seq) % blk) // seq))
    else:
        k_spec = v_spec
    in_specs = [pl.BlockSpec((tq, D_MODEL), lambda b, t: (b * nq + t, 0)), k_spec, v_spec]
    args = [q, k, v]
    if has_cache:
        past = cache[0].shape[1]
        c_spec = pl.BlockSpec((1, past, D_MODEL), lambda b, t: (b, 0, 0))
        in_specs += [c_spec, c_spec]
        args += list(cache)
    n_mod = mod.shape[0]
    in_specs += [pl.BlockSpec((tq, D_MODEL), lambda b, t: (b * nq + t, 0)),
                 pl.BlockSpec((1, 1, 6 * D_MODEL),
                              (lambda b, t: (b, 0, 0)) if n_mod > 1 else (lambda b, t: (0, 0, 0))),
                 _const_spec(g.shape), _const_spec(lam_p.shape), _const_spec(subln_g.shape),
                 _const_spec(w_o.shape)]
    args += [x, mod, g, lam_p, subln_g, w_o]
    return pl.pallas_call(
        functools.partial(_attn_kernel, has_cache=has_cache, keys_transposed=keys_transposed,
                          lam_init=lam_init),
        grid=(batch, nq),
        in_specs=in_specs,
        out_specs=pl.BlockSpec((tq, D_MODEL), lambda b, t: (b * nq + t, 0)),
        out_shape=jax.ShapeDtypeStruct((n, D_MODEL), F32),
        scratch_shapes=[pltpu.VMEM((tq, D_MODEL), BF16)],
        compiler_params=_cparams(2),
        name="diff_attention",
    )(*args)


def _rotary_tables(n_tokens):
    rows = n_tokens // GRID_W
    row = jnp.repeat(jnp.arange(rows, dtype=F32), GRID_W)
    col = jnp.tile(jnp.arange(GRID_W, dtype=F32), rows)
    inv = ROPE_THETA ** (-jnp.arange(ROT_PAIRS_PER_AXIS, dtype=F32) / ROT_PAIRS_PER_AXIS)
    ang = jnp.concatenate([row[:, None] * inv, col[:, None] * inv], axis=-1)
    cos, sin = jnp.cos(ang), jnp.sin(ang)
    reps = LANES // DIFF_HEAD_DIM
    return (jnp.tile(jnp.concatenate([cos, cos], axis=-1), (1, reps)),
            jnp.tile(jnp.concatenate([-sin, sin], axis=-1), (1, reps)))


def kernel(x_prompt, x_sample, state_ssm, cache_k, cache_v, c, c_ctx, w_mod, b_mod, norm_g, ssd_w_in, ssd_conv_w, ssd_conv_b, ssd_dt_bias, ssd_a_log, ssd_d, ssd_norm_g, ssd_w_out, sc_w_in, sc_conv_w, sc_w_out, da_w_qkv, da_lambda, da_subln_g, da_w_out, ffn_w_up, ffn_conv_w, ffn_w_down):
    bp, lp_, d = x_prompt.shape
    bs, ls, _ = x_sample.shape
    xp = x_prompt.reshape(bp * lp_, d)
    xs = x_sample.reshape(bs * ls, d)

    n_cond = 1 + bs
    cond = jnp.concatenate([c_ctx[None], c, jnp.zeros((HALO - n_cond, d), F32)], axis=0)
    mod_all = _modulation(cond, w_mod, b_mod)

    rot = _rotary_tables(ls)
    n_ssd_layers = (DEPTH + 2) // N_MIXERS
    pad = LANES - 2 * SSM_HEADS
    xbc_end = SSM_D_INNER + SSM_CONV_DIM
    ssd = {
        "w_in": ssd_w_in.astype(BF16),
        "w_dt": jnp.pad(ssd_w_in[:, :, xbc_end:], ((0, 0), (0, 0), (0, pad))).astype(BF16),
        "conv_w": ssd_conv_w,
        "conv_b": ssd_conv_b[:, None],
        "dt_bias": jnp.pad(ssd_dt_bias.reshape(n_ssd_layers, 1, -1), ((0, 0), (0, 0), (0, pad))),
        "a_log": jnp.pad(ssd_a_log.reshape(n_ssd_layers, 1, -1), ((0, 0), (0, 0), (0, pad))),
        "d_exp": jnp.broadcast_to(jnp.repeat(ssd_d, SSM_HEAD_DIM, axis=1)[:, :, None],
                                  (n_ssd_layers, SSM_D_INNER, SCAN_STEP)),
        "norm_g": ssd_norm_g[:, None],
        "w_out": ssd_w_out.astype(BF16),
    }
    h0 = state_ssm.reshape(bs, n_ssd_layers, 2, SSM_D_INNER, SSM_STATE)
    w_up_all, w_dn_all = ffn_w_up.astype(BF16), ffn_w_down.astype(BF16)
    ssm_buf = None
    new_k, new_v = [], []
    for l in range(DEPTH):
        j = l // N_MIXERS
        kind = l % N_MIXERS
        mod_p = mod_all[l, 0:1][:, None]
        mod_s = mod_all[l, 1:n_cond][:, None]
        g = norm_g[l]
        if kind == 0:
            xp, ssm_buf = _ssd_layer(xp, mod_p, g, ssd, j, None, (ssm_buf, n_ssd_layers), bp, lp_)
            xs, _ = _ssd_layer(xs, mod_s, g, ssd, j, h0, None, bs, ls)
        elif kind == 1:
            w_in, w_out = sc_w_in[j].astype(BF16), sc_w_out[j].astype(BF16)
            xp = _sconv(xp, mod_p, g, w_in, sc_conv_w[j], w_out, lp_)
            xs = _sconv(xs, mod_s, g, w_in, sc_conv_w[j], w_out, ls)
        else:
            lam_init = 0.8 - 0.6 * math.exp(-0.3 * l)
            w_qkv, w_o = da_w_qkv[j].astype(BF16), da_w_out[j].astype(BF16)
            sg = da_subln_g[j][None]
            qp, kp, vp, kp_heads = _qkv(xp, mod_p, g, w_qkv, None, lp_)
            xp = _attention(qp, kp, vp, None, xp, mod_p, g, da_lambda[j], sg, w_o,
                            batch=bp, lam_init=lam_init)
            new_k.append(kp_heads.reshape(bp, lp_, 2 * DIFF_HEADS, DIFF_HEAD_DIM))
            new_v.append(vp.reshape(bp, lp_, DIFF_HEADS, 2 * DIFF_HEAD_DIM))
            qs, ks, vs = _qkv(xs, mod_s, g, w_qkv, rot, ls)
            cache = (cache_k[:, j].reshape(bs, -1, d), cache_v[:, j].reshape(bs, -1, d))
            xs = _attention(qs, ks, vs, cache, xs, mod_s, g, da_lambda[j], sg, w_o,
                            batch=bs, lam_init=lam_init)
        xp = _ffn(xp, mod_p, g, l, w_up_all, ffn_conv_w, w_dn_all, lp_)
        xs = _ffn(xs, mod_s, g, l, w_up_all, ffn_conv_w, w_dn_all, ls)

    new_ssm = ssm_buf.reshape(bp, n_ssd_layers, 2, SSM_HEADS, SSM_HEAD_DIM, SSM_STATE)
    return (xp.reshape(bp, lp_, d), xs.reshape(bs, ls, d), new_ssm,
            jnp.stack(new_k, axis=1), jnp.stack(new_v, axis=1))
```
